```python
import jax, jax.numpy as jnp
from jax import lax
import numpy as np

D_MODEL = 2048
BATCH = 2
SEQ = 4096
DEPTH = 1

N_MEM = 256
ATT_HEAD_DIM = 128
ATT_WIDTH = D_MODEL // 2
ATT_HEADS = ATT_WIDTH // ATT_HEAD_DIM
MOBA_BLOCK = 256
MOBA_TOPK = 3
Q_CHUNK = 64
SGU_WIDTH = D_MODEL // 2
SGU_CHUNK = 128
SGU_GROUP_DIM = 128
SGU_GROUPS = SGU_WIDTH // SGU_GROUP_DIM
MEM_HEADS = 4
MEM_WIDTH = D_MODEL // 2
MEM_HEAD_DIM = MEM_WIDTH // MEM_HEADS
N_BRANCH = 3
DN_ALPHA = (2 * DEPTH) ** 0.25
DN_BETA = (8 * DEPTH) ** -0.25
LN_EPS = 1e-5
IN_WIDTHS = (ATT_WIDTH, ATT_WIDTH, ATT_WIDTH, ATT_WIDTH,
             SGU_WIDTH, SGU_WIDTH, SGU_WIDTH,
             MEM_WIDTH, MEM_WIDTH,
             N_BRANCH * D_MODEL)

kernel_name = 'moba_gmlp_memxattn_gated_hybrid'


def _layer_norm(x, g, b):
    xf = x.astype(jnp.float32)
    mu = xf.mean(-1, keepdims=True)
    var = jnp.square(xf - mu).mean(-1, keepdims=True)
    return ((xf - mu) * lax.rsqrt(var + LN_EPS) * g + b).astype(x.dtype)


def _moba_attention(q, k, v):
    b, s, h, dh = q.shape
    nb = -(-s // MOBA_BLOCK)
    n_sel = min(MOBA_TOPK, nb)
    pad = nb * MOBA_BLOCK - s
    q = q.transpose(0, 2, 1, 3)
    k = jnp.pad(k.transpose(0, 2, 1, 3), ((0, 0), (0, 0), (0, pad), (0, 0)))
    v = jnp.pad(v.transpose(0, 2, 1, 3), ((0, 0), (0, 0), (0, pad), (0, 0)))
    k_blocks = k.reshape(b, h, nb, MOBA_BLOCK, dh)
    v_blocks = v.reshape(b, h, nb, MOBA_BLOCK, dh)
    k_mean = k_blocks.mean(axis=3)
    scale = dh ** -0.5
    blk_ids = jnp.arange(nb)
    slot_ids = jnp.arange(n_sel)
    key_off = jnp.arange(MOBA_BLOCK)
    q_off = jnp.arange(Q_CHUNK)
    gather = jax.vmap(jax.vmap(lambda blocks, ids: blocks[ids]))

    def chunk(start):
        qc = lax.dynamic_slice_in_dim(q, start, Q_CHUNK, axis=2)
        blk = start // MOBA_BLOCK
        s_blk = jnp.einsum('bhqd,bhnd->bhqn', qc, k_mean)
        s_blk = jnp.where(blk_ids < blk, s_blk, -jnp.inf)
        _, sel = lax.top_k(s_blk, n_sel)
        k_sel = gather(k_blocks, sel)
        v_sel = gather(v_blocks, sel)
        s_sel = jnp.einsum('bhqd,bhqrkd->bhqrk', qc, k_sel) * scale
        s_sel = jnp.where((slot_ids < blk)[:, None], s_sel, -jnp.inf)
        k_own = lax.dynamic_slice_in_dim(k, blk * MOBA_BLOCK, MOBA_BLOCK, axis=2)
        v_own = lax.dynamic_slice_in_dim(v, blk * MOBA_BLOCK, MOBA_BLOCK, axis=2)
        s_own = jnp.einsum('bhqd,bhkd->bhqk', qc, k_own) * scale
        causal = (blk * MOBA_BLOCK + key_off)[None, :] <= (start + q_off)[:, None]
        s_own = jnp.where(causal, s_own, -jnp.inf)
        logits = jnp.concatenate([s_sel.reshape(b, h, Q_CHUNK, n_sel * MOBA_BLOCK), s_own], axis=-1)
        p = jax.nn.softmax(logits.astype(jnp.float32), axis=-1).astype(v.dtype)
        p_sel = p[..., :n_sel * MOBA_BLOCK].reshape(b, h, Q_CHUNK, n_sel, MOBA_BLOCK)
        p_own = p[..., n_sel * MOBA_BLOCK:]
        return (jnp.einsum('bhqrk,bhqrkd->bhqd', p_sel, v_sel)
                + jnp.einsum('bhqk,bhkd->bhqd', p_own, v_own))

    out = lax.map(chunk, jnp.arange(0, s, Q_CHUNK))
    return out.transpose(1, 0, 3, 2, 4).reshape(b, s, h * dh)


def _spatial_gating(u, v, w_s, b_s, ln_v_g, ln_v_b):
    b, s, _ = u.shape
    u = jax.nn.gelu(u)
    v = _layer_norm(jax.nn.gelu(v), ln_v_g, ln_v_b)
    vc = v.reshape(b, s // SGU_CHUNK, SGU_CHUNK, SGU_GROUPS, SGU_GROUP_DIM)
    tril = jnp.tril(jnp.ones((SGU_CHUNK, SGU_CHUNK), dtype=bool))
    w_causal = jnp.where(tril[None], w_s, 0.0)
    mixed = jnp.einsum('gts,bcsgd->bctgd', w_causal, vc) + b_s.T[None, None, :, :, None]
    return u * mixed.reshape(b, s, SGU_WIDTH)


def _memory_attention(q, mem_k, mem_v):
    b, s, h, dh = q.shape
    logits = jnp.einsum('bshd,bmhd->bhsm', q, mem_k) * (dh ** -0.5)
    p = jax.nn.softmax(logits.astype(jnp.float32), axis=-1).astype(mem_v.dtype)
    return jnp.einsum('bhsm,bmhd->bshd', p, mem_v).reshape(b, s, h * dh)


def setup_inputs(seed: int = 0) -> dict:
    key = jax.random.key(seed)
    ks = jax.random.split(key, 32)

    def nrm(k, shape, fan_in, scale=1.0):
        return jax.random.normal(k, shape, jnp.float32) * (scale * fan_in ** -0.5)

    in_scales = (1.0, 1.0, DN_BETA, 1.0, DN_BETA, DN_BETA, 1.0, 1.0, 1.0, 1.0)
    w_in = jnp.concatenate([nrm(ks[10 + i], (D_MODEL, w), D_MODEL, sc)
                            for i, (w, sc) in enumerate(zip(IN_WIDTHS, in_scales))], axis=1)
    return {
        'x': jax.random.normal(ks[0], (BATCH, SEQ, D_MODEL), jnp.float32),
        'mem': jax.random.normal(ks[1], (BATCH, N_MEM, D_MODEL), jnp.float32),
        'w_in': w_in,
        'w_mem_k': nrm(ks[2], (D_MODEL, MEM_WIDTH), D_MODEL),
        'w_mem_v': nrm(ks[3], (D_MODEL, MEM_WIDTH), D_MODEL, DN_BETA),
        'w_s': nrm(ks[4], (SGU_GROUPS, SGU_CHUNK, SGU_CHUNK), SGU_CHUNK),
        'b_s': 1.0 + 0.1 * jax.random.normal(ks[5], (SGU_GROUPS, SGU_CHUNK), jnp.float32),
        'ln_v_g': 1.0 + 0.02 * jax.random.normal(ks[6], (SGU_WIDTH,), jnp.float32),
        'ln_v_b': 0.02 * jax.random.normal(ks[7], (SGU_WIDTH,), jnp.float32),
        'w_branch_attn': nrm(ks[8], (ATT_WIDTH, D_MODEL), ATT_WIDTH, DN_BETA),
        'w_branch_sgu': nrm(ks[9], (SGU_WIDTH, D_MODEL), SGU_WIDTH, DN_BETA),
        'w_branch_mem': nrm(ks[20], (MEM_WIDTH, D_MODEL), MEM_WIDTH, DN_BETA),
        'w_out': nrm(ks[21], (D_MODEL, D_MODEL), D_MODEL, DN_BETA),
        'ln_g': 1.0 + 0.02 * jax.random.normal(ks[22], (D_MODEL,), jnp.float32),
        'ln_b': 0.02 * jax.random.normal(ks[23], (D_MODEL,), jnp.float32),
    }


def reference(x, mem, w_in, w_mem_k, w_mem_v, w_s, b_s, ln_v_g, ln_v_b,
              w_branch_attn, w_branch_sgu, w_branch_mem, w_out, ln_g, ln_b):
    b, s, d = x.shape
    m = mem.shape[1]
    split_at = np.cumsum(IN_WIDTHS)[:-1].tolist()
    for _ in range(DEPTH):
        proj = x @ w_in
        q_a, k_a, v_a, z_a, u_g, v_g, z_g, q_c, z_c, g_logit = jnp.split(proj, split_at, axis=-1)
        y_a = _moba_attention(q_a.reshape(b, s, ATT_HEADS, ATT_HEAD_DIM),
                              k_a.reshape(b, s, ATT_HEADS, ATT_HEAD_DIM),
                              v_a.reshape(b, s, ATT_HEADS, ATT_HEAD_DIM)) * jax.nn.silu(z_a)
        y_g = _spatial_gating(u_g, v_g, w_s, b_s, ln_v_g, ln_v_b) * jax.nn.silu(z_g)
        mem_k = (mem @ w_mem_k).reshape(b, m, MEM_HEADS, MEM_HEAD_DIM)
        mem_v = (mem @ w_mem_v).reshape(b, m, MEM_HEADS, MEM_HEAD_DIM)
        y_c = _memory_attention(q_c.reshape(b, s, MEM_HEADS, MEM_HEAD_DIM), mem_k, mem_v) * jax.nn.silu(z_c)
        gates = jax.nn.sigmoid(g_logit.astype(jnp.float32)).astype(x.dtype).reshape(b, s, N_BRANCH, d)
        merged = (gates[:, :, 0] * (y_a @ w_branch_attn)
                  + gates[:, :, 1] * (y_g @ w_branch_sgu)
                  + gates[:, :, 2] * (y_c @ w_branch_mem))
        y = merged @ w_out
        x = _layer_norm(DN_ALPHA * x + y, ln_g, ln_b)
    return x
```

```python
import functools

import jax
import jax.numpy as jnp
from jax import lax
from jax.experimental import pallas as pl
from jax.experimental.pallas import tpu as pltpu

D_MODEL = 2048
DEPTH = 1
N_MEM = 256
ATT_HEAD_DIM = 128
ATT_WIDTH = D_MODEL // 2
ATT_HEADS = ATT_WIDTH // ATT_HEAD_DIM
MOBA_BLOCK = 256
MOBA_TOPK = 3
SGU_WIDTH = D_MODEL // 2
SGU_CHUNK = 128
SGU_GROUP_DIM = 128
SGU_GROUPS = SGU_WIDTH // SGU_GROUP_DIM
MEM_HEADS = 4
MEM_WIDTH = D_MODEL // 2
MEM_HEAD_DIM = MEM_WIDTH // MEM_HEADS
N_BRANCH = 3
DN_ALPHA = (2 * DEPTH) ** 0.25
LN_EPS = 1e-5

BRANCH_IN_WIDTH = 9 * 1024
GATE_WIDTH = N_BRANCH * D_MODEL

VMEM_LIMIT_BYTES = 48 * 1024 * 1024

_NT = (((1,), (1,)), ((), ()))
_F32 = jnp.float32
_BF16 = jnp.bfloat16


def _sigmoid(x):
    return 1.0 / (1.0 + jnp.exp(-x))


def _silu(x):
    return x * _sigmoid(x)


def _gelu_tanh(x):
    c = 0.7978845608028654
    return x * (0.5 * (1.0 + jnp.tanh(c * (x + 0.044715 * (x * x * x)))))


def _matmul_kernel(x_ref, w_ref, o_ref, *, sigmoid_epilogue):
    acc = jnp.dot(x_ref[...], w_ref[...], preferred_element_type=_F32)
    if sigmoid_epilogue:
        acc = _sigmoid(acc)
    o_ref[...] = acc.astype(o_ref.dtype)


def _matmul(x, w, *, tm, tn, name, sigmoid_epilogue=False):
    m, k = x.shape
    _, n = w.shape
    assert m % tm == 0 and n % tn == 0
    return pl.pallas_call(
        functools.partial(_matmul_kernel, sigmoid_epilogue=sigmoid_epilogue),
        grid=(m // tm, n // tn),
        in_specs=[pl.BlockSpec((tm, k), lambda i, j: (i, 0)),
                  pl.BlockSpec((k, tn), lambda i, j: (0, j))],
        out_specs=pl.BlockSpec((tm, tn), lambda i, j: (i, j)),
        out_shape=jax.ShapeDtypeStruct((m, n), _BF16),
        compiler_params=pltpu.CompilerParams(
            dimension_semantics=("arbitrary", "arbitrary"), vmem_limit_bytes=VMEM_LIMIT_BYTES),
        name=name,
    )(x, w)


def _moba_kernel(q_ref, k_ref, v_ref, z_ref, o_ref, vt_ref, kmean_ref, bias_ref, *, n_blocks):
    qi = pl.program_id(2)
    blk = MOBA_BLOCK
    scale = ATT_HEAD_DIM ** -0.5

    @pl.when(qi == 0)
    def _per_head_setup():
        for j in range(n_blocks):
            rows = slice(j * blk, (j + 1) * blk)
            vt_ref[j] = v_ref[rows, :].astype(_F32).T.astype(_BF16)
            kmean_ref[j:j + 1, :] = jnp.sum(k_ref[rows, :].astype(_F32), axis=0, keepdims=True) * (1.0 / blk)

    q = q_ref[...]

    km = kmean_ref[...]
    km_hi = km.astype(_BF16)
    km_lo = (km - km_hi.astype(_F32)).astype(_BF16)
    sb = (lax.dot_general(km_hi, q, _NT, preferred_element_type=_F32)
          + lax.dot_general(km_lo, q, _NT, preferred_element_type=_F32))
    row_id = lax.broadcasted_iota(jnp.int32, sb.shape, 0)
    rank = jnp.zeros(sb.shape, jnp.int32)
    for j in range(n_blocks):
        other = sb[j:j + 1, :]
        beats = (other > sb) | ((other == sb) & (j < row_id))
        rank = rank + jnp.where(beats & (j < qi), 1, 0)
    selected = (row_id < qi) & (rank < MOBA_TOPK)
    bias_ref[...] = jnp.where(selected, 0.0, -jnp.inf)

    own = pl.multiple_of(qi * blk, blk)
    s = lax.dot_general(k_ref[pl.ds(own, blk), :], q, _NT, preferred_element_type=_F32) * scale
    key_pos = lax.broadcasted_iota(jnp.int32, s.shape, 0)
    q_pos = lax.broadcasted_iota(jnp.int32, s.shape, 1)
    s = jnp.where(key_pos <= q_pos, s, -jnp.inf)
    m0 = jnp.max(s, axis=0, keepdims=True)
    p = jnp.exp(s - m0)
    l0 = jnp.sum(p, axis=0, keepdims=True)
    acc0 = jnp.dot(vt_ref[qi], p.astype(_BF16), preferred_element_type=_F32)

    def past_block(j, carry):
        m, l, acc = carry
        start = pl.multiple_of(j * blk, blk)
        s = lax.dot_general(k_ref[pl.ds(start, blk), :], q, _NT, preferred_element_type=_F32) * scale
        s = s + bias_ref[pl.ds(j, 1), :]
        m_new = jnp.maximum(m, jnp.max(s, axis=0, keepdims=True))
        alpha = jnp.exp(m - m_new)
        p = jnp.exp(s - m_new)
        l = alpha * l + jnp.sum(p, axis=0, keepdims=True)
        acc = alpha * acc + jnp.dot(vt_ref[j], p.astype(_BF16), preferred_element_type=_F32)
        return m_new, l, acc

    _, l, acc = lax.fori_loop(0, qi, past_block, (m0, l0, acc0))
    attn = (acc / l).T
    o_ref[...] = (attn * _silu(z_ref[...].astype(_F32))).astype(o_ref.dtype)


def _moba_attention(proj, batch, seq):
    n_blocks = seq // MOBA_BLOCK
    proj3 = proj.reshape(batch, seq, BRANCH_IN_WIDTH)
    hd = ATT_HEAD_DIM
    cols = ATT_WIDTH // hd
    return pl.pallas_call(
        functools.partial(_moba_kernel, n_blocks=n_blocks),
        grid=(batch, ATT_HEADS, n_blocks),
        in_specs=[pl.BlockSpec((None, MOBA_BLOCK, hd), lambda b, h, i: (b, i, h)),
                  pl.BlockSpec((None, seq, hd), lambda b, h, i: (b, 0, cols + h)),
                  pl.BlockSpec((None, seq, hd), lambda b, h, i: (b, 0, 2 * cols + h)),
                  pl.BlockSpec((None, MOBA_BLOCK, hd), lambda b, h, i: (b, i, 3 * cols + h))],
        out_specs=pl.BlockSpec((None, MOBA_BLOCK, hd), lambda b, h, i: (b, i, h)),
        out_shape=jax.ShapeDtypeStruct((batch, seq, ATT_WIDTH), _BF16),
        scratch_shapes=[pltpu.VMEM((n_blocks, hd, MOBA_BLOCK), _BF16),
                        pltpu.VMEM((n_blocks, hd), _F32),
                        pltpu.VMEM((n_blocks, MOBA_BLOCK), _F32)],
        compiler_params=pltpu.CompilerParams(
            dimension_semantics=("arbitrary", "arbitrary", "arbitrary"), vmem_limit_bytes=VMEM_LIMIT_BYTES),
        name="moba_attention",
    )(proj3, proj3, proj3, proj3)


def _sgu_kernel(u_ref, v_ref, z_ref, w_ref, bt_ref, g_ref, b_ref, o_ref, *, tm):
    u = _gelu_tanh(u_ref[...].astype(_F32))
    v = _gelu_tanh(v_ref[...].astype(_F32))
    mu = jnp.mean(v, axis=-1, keepdims=True)
    d = v - mu
    var = jnp.mean(d * d, axis=-1, keepdims=True)
    vn = (d * lax.rsqrt(var + LN_EPS) * g_ref[...] + b_ref[...]).astype(_BF16)
    gate = u * _silu(z_ref[...].astype(_F32))
    t_pos = lax.broadcasted_iota(jnp.int32, (SGU_CHUNK, SGU_CHUNK), 0)
    s_pos = lax.broadcasted_iota(jnp.int32, (SGU_CHUNK, SGU_CHUNK), 1)
    for g in range(SGU_GROUPS):
        w_causal = jnp.where(s_pos <= t_pos, w_ref[g], 0.0).astype(_BF16)
        bias = bt_ref[:, g:g + 1]
        cols = slice(g * SGU_GROUP_DIM, (g + 1) * SGU_GROUP_DIM)
        for c in range(tm // SGU_CHUNK):
            rows = slice(c * SGU_CHUNK, (c + 1) * SGU_CHUNK)
            mixed = jnp.dot(w_causal, vn[rows, cols], preferred_element_type=_F32) + bias
            o_ref[rows, cols] = (gate[rows, cols] * mixed).astype(o_ref.dtype)


def _spatial_gating(proj, w_s, b_s, ln_v_g, ln_v_b, *, tm=512):
    t = proj.shape[0]
    blk = lambda c: pl.BlockSpec((tm, SGU_WIDTH), lambda i: (i, c))
    whole = lambda shape: pl.BlockSpec(shape, lambda i: (0,) * len(shape))
    return pl.pallas_call(
        functools.partial(_sgu_kernel, tm=tm),
        grid=(t // tm,),
        in_specs=[blk(4), blk(5), blk(6),
                  whole((SGU_GROUPS, SGU_CHUNK, SGU_CHUNK)), whole((SGU_CHUNK, SGU_GROUPS)),
                  whole((1, SGU_WIDTH)), whole((1, SGU_WIDTH))],
        out_specs=pl.BlockSpec((tm, SGU_WIDTH), lambda i: (i, 0)),
        out_shape=jax.ShapeDtypeStruct((t, SGU_WIDTH), _BF16),
        compiler_params=pltpu.CompilerParams(
            dimension_semantics=("arbitrary",), vmem_limit_bytes=VMEM_LIMIT_BYTES),
        name="spatial_gating",
    )(proj, proj, proj, w_s, b_s.T, ln_v_g.reshape(1, -1), ln_v_b.reshape(1, -1))


def _mem_attn_kernel(q_ref, z_ref, k_ref, v_ref, o_ref):
    scale = MEM_HEAD_DIM ** -0.5
    for h in range(MEM_HEADS):
        cols = slice(h * MEM_HEAD_DIM, (h + 1) * MEM_HEAD_DIM)
        s = lax.dot_general(q_ref[:, cols], k_ref[:, cols], _NT, preferred_element_type=_F32) * scale
        p = jnp.exp(s - jnp.max(s, axis=-1, keepdims=True))
        l = jnp.sum(p, axis=-1, keepdims=True)
        o = jnp.dot(p.astype(_BF16), v_ref[:, cols], preferred_element_type=_F32) / l
        o_ref[:, cols] = (o * _silu(z_ref[:, cols].astype(_F32))).astype(o_ref.dtype)


def _memory_attention(proj, mem_kv, batch, seq, *, tm=512):
    proj3 = proj.reshape(batch, seq, BRANCH_IN_WIDTH)
    kv3 = mem_kv.reshape(batch, N_MEM, 2 * MEM_WIDTH)
    return pl.pallas_call(
        _mem_attn_kernel,
        grid=(batch, seq // tm),
        in_specs=[pl.BlockSpec((None, tm, MEM_WIDTH), lambda b, i: (b, i, 7)),
                  pl.BlockSpec((None, tm, MEM_WIDTH), lambda b, i: (b, i, 8)),
                  pl.BlockSpec((None, N_MEM, MEM_WIDTH), lambda b, i: (b, 0, 0)),
                  pl.BlockSpec((None, N_MEM, MEM_WIDTH), lambda b, i: (b, 0, 1))],
        out_specs=pl.BlockSpec((None, tm, MEM_WIDTH), lambda b, i: (b, i, 0)),
        out_shape=jax.ShapeDtypeStruct((batch, seq, MEM_WIDTH), _BF16),
        compiler_params=pltpu.CompilerParams(
            dimension_semantics=("arbitrary", "arbitrary"), vmem_limit_bytes=VMEM_LIMIT_BYTES),
        name="memory_attention",
    )(proj3, proj3, kv3, kv3)


def _merge_kernel(ya_ref, yg_ref, yc_ref, g0_ref, g1_ref, g2_ref, x_ref,
                  wa_ref, wg_ref, wc_ref, wo_ref, lng_ref, lnb_ref, o_ref):
    merged = g0_ref[...].astype(_F32) * jnp.dot(ya_ref[...], wa_ref[...], preferred_element_type=_F32)
    merged += g1_ref[...].astype(_F32) * jnp.dot(yg_ref[...], wg_ref[...], preferred_element_type=_F32)
    merged += g2_ref[...].astype(_F32) * jnp.dot(yc_ref[...], wc_ref[...], preferred_element_type=_F32)
    y = jnp.dot(merged.astype(_BF16), wo_ref[...], preferred_element_type=_F32)
    h = DN_ALPHA * x_ref[...] + y
    mu = jnp.mean(h, axis=-1, keepdims=True)
    d = h - mu
    var = jnp.mean(d * d, axis=-1, keepdims=True)
    o_ref[...] = d * lax.rsqrt(var + LN_EPS) * lng_ref[...] + lnb_ref[...]


def _merge_project_norm(ya, yg, yc, gates, x2, wa, wg, wc, wo, ln_g, ln_b, *, tm=256):
    t = x2.shape[0]
    branch = pl.BlockSpec((tm, ATT_WIDTH), lambda i: (i, 0))
    gate = lambda c: pl.BlockSpec((tm, D_MODEL), lambda i: (i, c))
    resident = lambda shape: pl.BlockSpec(shape, lambda i: (0, 0), pipeline_mode=pl.Buffered(1))
    return pl.pallas_call(
        _merge_kernel,
        grid=(t // tm,),
        in_specs=[branch, branch, branch, gate(0), gate(1), gate(2),
                  pl.BlockSpec((tm, D_MODEL), lambda i: (i, 0)),
                  resident((ATT_WIDTH, D_MODEL)), resident((SGU_WIDTH, D_MODEL)),
                  resident((MEM_WIDTH, D_MODEL)), resident((D_MODEL, D_MODEL)),
                  resident((1, D_MODEL)), resident((1, D_MODEL))],
        out_specs=pl.BlockSpec((tm, D_MODEL), lambda i: (i, 0)),
        out_shape=jax.ShapeDtypeStruct((t, D_MODEL), _F32),
        compiler_params=pltpu.CompilerParams(
            dimension_semantics=("arbitrary",), vmem_limit_bytes=VMEM_LIMIT_BYTES),
        name="merge_project_norm",
    )(ya, yg, yc, gates, gates, gates, x2, wa, wg, wc, wo, ln_g.reshape(1, -1), ln_b.reshape(1, -1))


def kernel(x, mem, w_in, w_mem_k, w_mem_v, w_s, b_s, ln_v_g, ln_v_b,
           w_branch_attn, w_branch_sgu, w_branch_mem, w_out, ln_g, ln_b):
    batch, seq, d = x.shape
    assert d == D_MODEL and seq % MOBA_BLOCK == 0 and mem.shape[1] == N_MEM
    t = batch * seq
    x2 = x.reshape(t, d)
    x_bf = x2.astype(_BF16)

    proj = _matmul(x_bf, w_in[:, :BRANCH_IN_WIDTH].astype(_BF16), tm=1024, tn=1024, name="in_proj_branches")
    gates = _matmul(x_bf, w_in[:, BRANCH_IN_WIDTH:].astype(_BF16), tm=1024, tn=1024,
                    name="in_proj_gates", sigmoid_epilogue=True)
    w_mem = jnp.concatenate([w_mem_k, w_mem_v], axis=1).astype(_BF16)
    mem_kv = _matmul(mem.reshape(batch * N_MEM, d).astype(_BF16), w_mem, tm=batch * N_MEM, tn=512,
                     name="mem_kv_proj")

    ya = _moba_attention(proj, batch, seq).reshape(t, ATT_WIDTH)
    yg = _spatial_gating(proj, w_s, b_s, ln_v_g, ln_v_b)
    yc = _memory_attention(proj, mem_kv, batch, seq).reshape(t, MEM_WIDTH)

    out = _merge_project_norm(ya, yg, yc, gates, x2,
                              w_branch_attn.astype(_BF16), w_branch_sgu.astype(_BF16),
                              w_branch_mem.astype(_BF16), w_out.astype(_BF16), ln_g, ln_b)
    return out.reshape(batch, seq, d)
```

```python
import functools

import jax
import jax.numpy as jnp
from jax import lax
from jax.experimental import pallas as pl
from jax.experimental.pallas import tpu as pltpu

D_MODEL = 2048
DEPTH = 1
N_MEM = 256
ATT_HEAD_DIM = 128
ATT_WIDTH = D_MODEL // 2
ATT_HEADS = ATT_WIDTH // ATT_HEAD_DIM
MOBA_BLOCK = 256
MOBA_TOPK = 3
SGU_WIDTH = D_MODEL // 2
SGU_CHUNK = 128
SGU_GROUP_DIM = 128
SGU_GROUPS = SGU_WIDTH // SGU_GROUP_DIM
MEM_HEADS = 4
MEM_WIDTH = D_MODEL // 2
MEM_HEAD_DIM = MEM_WIDTH // MEM_HEADS
N_BRANCH = 3
DN_ALPHA = (2 * DEPTH) ** 0.25
LN_EPS = 1e-5

BRANCH_IN_WIDTH = 9 * 1024
GATE_WIDTH = N_BRANCH * D_MODEL

VMEM_LIMIT_BYTES = 48 * 1024 * 1024

_NT = (((1,), (1,)), ((), ()))
_F32 = jnp.float32
_BF16 = jnp.bfloat16


def _sigmoid(x):
    return 1.0 / (1.0 + jnp.exp(-x))


def _silu(x):
    return x * _sigmoid(x)


def _gelu_tanh(x):
    c = 0.7978845608028654
    return x * (0.5 * (1.0 + jnp.tanh(c * (x + 0.044715 * (x * x * x)))))


def _matmul_kernel(x_ref, w_ref, o_ref, *, sigmoid_epilogue, first_tile_scale):
    acc = jnp.dot(x_ref[...], w_ref[...], preferred_element_type=_F32)
    if sigmoid_epilogue:
        acc = _sigmoid(acc)
    if first_tile_scale is not None:
        acc = acc * jnp.where(pl.program_id(1) == 0, first_tile_scale, 1.0)
    o_ref[...] = acc.astype(o_ref.dtype)


def _matmul(x, w, *, tm, tn, name, sigmoid_epilogue=False, first_tile_scale=None):
    m, k = x.shape
    _, n = w.shape
    assert m % tm == 0 and n % tn == 0
    return pl.pallas_call(
        functools.partial(_matmul_kernel, sigmoid_epilogue=sigmoid_epilogue, first_tile_scale=first_tile_scale),
        grid=(m // tm, n // tn),
        in_specs=[pl.BlockSpec((tm, k), lambda i, j: (i, 0)),
                  pl.BlockSpec((k, tn), lambda i, j: (0, j))],
        out_specs=pl.BlockSpec((tm, tn), lambda i, j: (i, j)),
        out_shape=jax.ShapeDtypeStruct((m, n), _BF16),
        compiler_params=pltpu.CompilerParams(
            dimension_semantics=("arbitrary", "arbitrary"), vmem_limit_bytes=VMEM_LIMIT_BYTES),
        name=name,
    )(x, w)


MOBA_HEADS_PER_STEP = 4
_ACC_ROWS = ATT_HEAD_DIM + 16
MOBA_Q_PRESCALE = ATT_HEAD_DIM ** -0.5 * 1.4426950408889634


def _moba_kernel(q_ref, k_ref, v_ref, z_ref, o_ref, vt_ref, kmean_ref, neg_ref, pos_ref, *, n_blocks, heads):
    qi = pl.program_id(2)
    blk = MOBA_BLOCK
    hd = ATT_HEAD_DIM

    @pl.when(qi == 0)
    def _per_head_setup():
        for g in range(heads):
            cols = slice(g * hd, (g + 1) * hd)
            for j in range(n_blocks):
                rows = slice(j * blk, (j + 1) * blk)
                vt_ref[g, j, :hd, :] = v_ref[rows, cols].astype(_F32).T.astype(_BF16)
                vt_ref[g, j, hd:, :] = jnp.ones((_ACC_ROWS - hd, blk), _BF16)
                kmean_ref[g, j:j + 1, :] = (
                    jnp.sum(k_ref[rows, cols].astype(_F32), axis=0, keepdims=True) * (1.0 / blk))

    head_cols = [slice(g * hd, (g + 1) * hd) for g in range(heads)]
    qs = [q_ref[:, c] for c in head_cols]

    def block_scores(j):
        start = pl.multiple_of(j * blk, blk)
        return [lax.dot_general(k_ref[pl.ds(start, blk), c], q, _NT, preferred_element_type=_F32)
                for c, q in zip(head_cols, qs)]

    sel_scores = []
    for g in range(heads):
        km = kmean_ref[g]
        km_hi = km.astype(_BF16)
        km_lo = (km - km_hi.astype(_F32)).astype(_BF16)
        sel_scores.append(lax.dot_general(km_hi, qs[g], _NT, preferred_element_type=_F32)
                          + lax.dot_general(km_lo, qs[g], _NT, preferred_element_type=_F32))
    s_first = block_scores(0)
    for g in range(heads):
        sb = sel_scores[g]
        row_id = lax.broadcasted_iota(jnp.int32, sb.shape, 0)
        rank = jnp.zeros(sb.shape, jnp.int32)
        for j in range(n_blocks):
            other = sb[j:j + 1, :]
            beats = (other > sb) | ((other == sb) & (j < row_id))
            rank = rank + jnp.where(beats & (j < qi), 1, 0)
        selected = (row_id < qi) & (rank < MOBA_TOPK)
        neg_ref[g] = jnp.where(selected, 0.0, -jnp.inf)
        pos_ref[g] = jnp.where(selected, -jnp.inf, jnp.inf)

    def accumulate(scores, block_maxes, sub_floor, j, ms, accs):
        probs, m_news = [], []
        for g in range(heads):
            m_new = jnp.maximum(ms[g], block_maxes[g])
            m_sub = m_new if sub_floor is None else jnp.maximum(m_new, sub_floor[g])
            probs.append(jnp.exp2(scores[g] - m_sub).astype(_BF16))
            m_news.append(m_new)
        out = []
        for g in range(heads):
            alpha = jnp.exp2(jnp.where(m_news[g] == -jnp.inf, 0.0, ms[g] - m_news[g]))
            out.append(alpha * accs[g] + jnp.dot(vt_ref[g, j], probs[g], preferred_element_type=_F32))
        return m_news, out

    def past_block(j, carry):
        scores, ms, accs = carry
        next_scores = block_scores(j + 1)
        maxes = [jnp.max(s, axis=0, keepdims=True) + neg_ref[g, pl.ds(j, 1), :] for g, s in enumerate(scores)]
        floors = [pos_ref[g, pl.ds(j, 1), :] for g in range(heads)]
        ms, accs = accumulate(scores, maxes, floors, j, ms, accs)
        return next_scores, ms, accs

    m0 = [jnp.full((1, blk), -jnp.inf, _F32)] * heads
    acc0 = [jnp.zeros((_ACC_ROWS, blk), _F32)] * heads
    scores, ms, accs = lax.fori_loop(0, qi, past_block, (s_first, m0, acc0))

    key_pos = lax.broadcasted_iota(jnp.int32, (blk, blk), 0)
    q_pos = lax.broadcasted_iota(jnp.int32, (blk, blk), 1)
    scores = [jnp.where(key_pos <= q_pos, s, -jnp.inf) for s in scores]
    maxes = [jnp.max(s, axis=0, keepdims=True) for s in scores]
    _, accs = accumulate(scores, maxes, None, qi, ms, accs)

    for g in range(heads):
        attn = (accs[g][:hd, :] / accs[g][hd:hd + 1, :]).T
        o_ref[:, head_cols[g]] = (attn * _silu(z_ref[:, head_cols[g]].astype(_F32))).astype(o_ref.dtype)


def _moba_attention(proj, batch, seq, *, heads=MOBA_HEADS_PER_STEP):
    n_blocks = seq // MOBA_BLOCK
    proj3 = proj.reshape(batch, seq, BRANCH_IN_WIDTH)
    width = heads * ATT_HEAD_DIM
    cols = ATT_WIDTH // width
    return pl.pallas_call(
        functools.partial(_moba_kernel, n_blocks=n_blocks, heads=heads),
        grid=(batch, cols, n_blocks),
        in_specs=[pl.BlockSpec((None, MOBA_BLOCK, width), lambda b, h, i: (b, i, h)),
                  pl.BlockSpec((None, seq, width), lambda b, h, i: (b, 0, cols + h)),
                  pl.BlockSpec((None, seq, width), lambda b, h, i: (b, 0, 2 * cols + h)),
                  pl.BlockSpec((None, MOBA_BLOCK, width), lambda b, h, i: (b, i, 3 * cols + h))],
        out_specs=pl.BlockSpec((None, MOBA_BLOCK, width), lambda b, h, i: (b, i, h)),
        out_shape=jax.ShapeDtypeStruct((batch, seq, ATT_WIDTH), _BF16),
        scratch_shapes=[pltpu.VMEM((heads, n_blocks, _ACC_ROWS, MOBA_BLOCK), _BF16),
                        pltpu.VMEM((heads, n_blocks, ATT_HEAD_DIM), _F32),
                        pltpu.VMEM((heads, n_blocks, MOBA_BLOCK), _F32),
                        pltpu.VMEM((heads, n_blocks, MOBA_BLOCK), _F32)],
        compiler_params=pltpu.CompilerParams(
            dimension_semantics=("arbitrary", "arbitrary", "arbitrary"), vmem_limit_bytes=VMEM_LIMIT_BYTES),
        name="moba_attention",
    )(proj3, proj3, proj3, proj3)


def _sgu_kernel(u_ref, v_ref, z_ref, w_ref, bt_ref, g_ref, b_ref, o_ref, *, tm):
    u = _gelu_tanh(u_ref[...].astype(_F32))
    v = _gelu_tanh(v_ref[...].astype(_F32))
    mu = jnp.mean(v, axis=-1, keepdims=True)
    d = v - mu
    var = jnp.mean(d * d, axis=-1, keepdims=True)
    vn = (d * lax.rsqrt(var + LN_EPS) * g_ref[...] + b_ref[...]).astype(_BF16)
    gate = u * _silu(z_ref[...].astype(_F32))
    t_pos = lax.broadcasted_iota(jnp.int32, (SGU_CHUNK, SGU_CHUNK), 0)
    s_pos = lax.broadcasted_iota(jnp.int32, (SGU_CHUNK, SGU_CHUNK), 1)
    for g in range(SGU_GROUPS):
        w_causal = jnp.where(s_pos <= t_pos, w_ref[g], 0.0).astype(_BF16)
        bias = bt_ref[:, g:g + 1]
        cols = slice(g * SGU_GROUP_DIM, (g + 1) * SGU_GROUP_DIM)
        for c in range(tm // SGU_CHUNK):
            rows = slice(c * SGU_CHUNK, (c + 1) * SGU_CHUNK)
            mixed = jnp.dot(w_causal, vn[rows, cols], preferred_element_type=_F32) + bias
            o_ref[rows, cols] = (gate[rows, cols] * mixed).astype(o_ref.dtype)


def _spatial_gating(proj, w_s, b_s, ln_v_g, ln_v_b, *, tm=512):
    t = proj.shape[0]
    blk = lambda c: pl.BlockSpec((tm, SGU_WIDTH), lambda i: (i, c))
    whole = lambda shape: pl.BlockSpec(shape, lambda i: (0,) * len(shape))
    return pl.pallas_call(
        functools.partial(_sgu_kernel, tm=tm),
        grid=(t // tm,),
        in_specs=[blk(4), blk(5), blk(6),
                  whole((SGU_GROUPS, SGU_CHUNK, SGU_CHUNK)), whole((SGU_CHUNK, SGU_GROUPS)),
                  whole((1, SGU_WIDTH)), whole((1, SGU_WIDTH))],
        out_specs=pl.BlockSpec((tm, SGU_WIDTH), lambda i: (i, 0)),
        out_shape=jax.ShapeDtypeStruct((t, SGU_WIDTH), _BF16),
        compiler_params=pltpu.CompilerParams(
            dimension_semantics=("arbitrary",), vmem_limit_bytes=VMEM_LIMIT_BYTES),
        name="spatial_gating",
    )(proj, proj, proj, w_s, b_s.T, ln_v_g.reshape(1, -1), ln_v_b.reshape(1, -1))


def _mem_attn_kernel(q_ref, z_ref, k_ref, v_ref, o_ref):
    scale = MEM_HEAD_DIM ** -0.5
    for h in range(MEM_HEADS):
        cols = slice(h * MEM_HEAD_DIM, (h + 1) * MEM_HEAD_DIM)
        s = lax.dot_general(q_ref[:, cols], k_ref[:, cols], _NT, preferred_element_type=_F32) * scale
        p = jnp.exp(s - jnp.max(s, axis=-1, keepdims=True))
        l = jnp.sum(p, axis=-1, keepdims=True)
        o = jnp.dot(p.astype(_BF16), v_ref[:, cols], preferred_element_type=_F32) / l
        o_ref[:, cols] = (o * _silu(z_ref[:, cols].astype(_F32))).astype(o_ref.dtype)


def _memory_attention(proj, mem_kv, batch, seq, *, tm=512):
    proj3 = proj.reshape(batch, seq, BRANCH_IN_WIDTH)
    kv3 = mem_kv.reshape(batch, N_MEM, 2 * MEM_WIDTH)
    return pl.pallas_call(
        _mem_attn_kernel,
        grid=(batch, seq // tm),
        in_specs=[pl.BlockSpec((None, tm, MEM_WIDTH), lambda b, i: (b, i, 7)),
                  pl.BlockSpec((None, tm, MEM_WIDTH), lambda b, i: (b, i, 8)),
                  pl.BlockSpec((None, N_MEM, MEM_WIDTH), lambda b, i: (b, 0, 0)),
                  pl.BlockSpec((None, N_MEM, MEM_WIDTH), lambda b, i: (b, 0, 1))],
        out_specs=pl.BlockSpec((None, tm, MEM_WIDTH), lambda b, i: (b, i, 0)),
        out_shape=jax.ShapeDtypeStruct((batch, seq, MEM_WIDTH), _BF16),
        compiler_params=pltpu.CompilerParams(
            dimension_semantics=("arbitrary", "arbitrary"), vmem_limit_bytes=VMEM_LIMIT_BYTES),
        name="memory_attention",
    )(proj3, proj3, kv3, kv3)


def _merge_kernel(ya_ref, yg_ref, yc_ref, g0_ref, g1_ref, g2_ref, x_ref,
                  wa_ref, wg_ref, wc_ref, wo_ref, lng_ref, lnb_ref, o_ref):
    merged = g0_ref[...].astype(_F32) * jnp.dot(ya_ref[...], wa_ref[...], preferred_element_type=_F32)
    merged += g1_ref[...].astype(_F32) * jnp.dot(yg_ref[...], wg_ref[...], preferred_element_type=_F32)
    merged += g2_ref[...].astype(_F32) * jnp.dot(yc_ref[...], wc_ref[...], preferred_element_type=_F32)
    y = jnp.dot(merged.astype(_BF16), wo_ref[...], preferred_element_type=_F32)
    h = DN_ALPHA * x_ref[...] + y
    mu = jnp.mean(h, axis=-1, keepdims=True)
    d = h - mu
    var = jnp.mean(d * d, axis=-1, keepdims=True)
    o_ref[...] = d * lax.rsqrt(var + LN_EPS) * lng_ref[...] + lnb_ref[...]


def _merge_project_norm(ya, yg, yc, gates, x2, wa, wg, wc, wo, ln_g, ln_b, *, tm=256):
    t = x2.shape[0]
    branch = pl.BlockSpec((tm, ATT_WIDTH), lambda i: (i, 0))
    gate = lambda c: pl.BlockSpec((tm, D_MODEL), lambda i: (i, c))
    resident = lambda shape: pl.BlockSpec(shape, lambda i: (0, 0), pipeline_mode=pl.Buffered(1))
    return pl.pallas_call(
        _merge_kernel,
        grid=(t // tm,),
        in_specs=[branch, branch, branch, gate(0), gate(1), gate(2),
                  pl.BlockSpec((tm, D_MODEL), lambda i: (i, 0)),
                  resident((ATT_WIDTH, D_MODEL)), resident((SGU_WIDTH, D_MODEL)),
                  resident((MEM_WIDTH, D_MODEL)), resident((D_MODEL, D_MODEL)),
                  resident((1, D_MODEL)), resident((1, D_MODEL))],
        out_specs=pl.BlockSpec((tm, D_MODEL), lambda i: (i, 0)),
        out_shape=jax.ShapeDtypeStruct((t, D_MODEL), _F32),
        compiler_params=pltpu.CompilerParams(
            dimension_semantics=("arbitrary",), vmem_limit_bytes=VMEM_LIMIT_BYTES),
        name="merge_project_norm",
    )(ya, yg, yc, gates, gates, gates, x2, wa, wg, wc, wo, ln_g.reshape(1, -1), ln_b.reshape(1, -1))


def kernel(x, mem, w_in, w_mem_k, w_mem_v, w_s, b_s, ln_v_g, ln_v_b,
           w_branch_attn, w_branch_sgu, w_branch_mem, w_out, ln_g, ln_b):
    batch, seq, d = x.shape
    assert d == D_MODEL and seq % MOBA_BLOCK == 0 and mem.shape[1] == N_MEM
    t = batch * seq
    x2 = x.reshape(t, d)
    x_bf = x2.astype(_BF16)

    proj = _matmul(x_bf, w_in[:, :BRANCH_IN_WIDTH].astype(_BF16), tm=1024, tn=ATT_WIDTH,
                   name="in_proj_branches", first_tile_scale=MOBA_Q_PRESCALE)
    gates = _matmul(x_bf, w_in[:, BRANCH_IN_WIDTH:].astype(_BF16), tm=1024, tn=1024,
                    name="in_proj_gates", sigmoid_epilogue=True)
    w_mem = jnp.concatenate([w_mem_k, w_mem_v], axis=1).astype(_BF16)
    mem_kv = _matmul(mem.reshape(batch * N_MEM, d).astype(_BF16), w_mem, tm=batch * N_MEM, tn=512,
                     name="mem_kv_proj")

    ya = _moba_attention(proj, batch, seq).reshape(t, ATT_WIDTH)
    yg = _spatial_gating(proj, w_s, b_s, ln_v_g, ln_v_b)
    yc = _memory_attention(proj, mem_kv, batch, seq).reshape(t, MEM_WIDTH)

    out = _merge_project_norm(ya, yg, yc, gates, x2,
                              w_branch_attn.astype(_BF16), w_branch_sgu.astype(_BF16),
                              w_branch_mem.astype(_BF16), w_out.astype(_BF16), ln_g, ln_b)
    return out.reshape(batch, seq, d)
```

```python
import functools

import jax
import jax.numpy as jnp
from jax import lax
from jax.experimental import pallas as pl
from jax.experimental.pallas import tpu as pltpu

D_MODEL = 2048
DEPTH = 1
N_MEM = 256
ATT_HEAD_DIM = 128
ATT_WIDTH = D_MODEL // 2
ATT_HEADS = ATT_WIDTH // ATT_HEAD_DIM
MOBA_BLOCK = 256
MOBA_TOPK = 3
SGU_WIDTH = D_MODEL // 2
SGU_CHUNK = 128
SGU_GROUP_DIM = 128
SGU_GROUPS = SGU_WIDTH // SGU_GROUP_DIM
MEM_HEADS = 4
MEM_WIDTH = D_MODEL // 2
MEM_HEAD_DIM = MEM_WIDTH // MEM_HEADS
N_BRANCH = 3
DN_ALPHA = (2 * DEPTH) ** 0.25
LN_EPS = 1e-5

PROJ_TILE = 1024
GATE_TILE0 = 9
PROJ_WIDTH = (GATE_TILE0 + N_BRANCH * D_MODEL // PROJ_TILE) * PROJ_TILE

VMEM_LIMIT_BYTES = 48 * 1024 * 1024

_NT = (((1,), (1,)), ((), ()))
_F32 = jnp.float32
_BF16 = jnp.bfloat16


def _sigmoid(x):
    return 1.0 / (1.0 + jnp.exp(-x))


def _silu(x):
    return x * _sigmoid(x)


def _gelu_tanh(x):
    c = 0.7978845608028654
    return x * (0.5 * (1.0 + jnp.tanh(c * (x + 0.044715 * (x * x * x)))))


def _project_kernel(x_ref, w_ref, o_ref, w_bf_ref, *, first_tile_scale, sigmoid_from_tile):
    j = pl.program_id(0)

    @pl.when(pl.program_id(1) == 0)
    def _cast_weight_tile():
        w_bf_ref[...] = w_ref[...].astype(_BF16)

    acc = jnp.dot(x_ref[...], w_bf_ref[...], preferred_element_type=_F32)
    if sigmoid_from_tile is None:
        o_ref[...] = acc.astype(o_ref.dtype)
        return

    @pl.when(j >= sigmoid_from_tile)
    def _gate_tile():
        o_ref[...] = _sigmoid(acc).astype(o_ref.dtype)

    @pl.when(j < sigmoid_from_tile)
    def _plain_tile():
        o_ref[...] = (acc * jnp.where(j == 0, first_tile_scale, 1.0)).astype(o_ref.dtype)


def _project(x, w, *, tm, tn, name, first_tile_scale=None, sigmoid_from_tile=None):
    m, k = x.shape
    _, n = w.shape
    assert m % tm == 0 and n % tn == 0
    assert (first_tile_scale is None) == (sigmoid_from_tile is None)
    return pl.pallas_call(
        functools.partial(_project_kernel, first_tile_scale=first_tile_scale, sigmoid_from_tile=sigmoid_from_tile),
        grid=(n // tn, m // tm),
        in_specs=[pl.BlockSpec((tm, k), lambda j, i: (i, 0)),
                  pl.BlockSpec((k, tn), lambda j, i: (0, j))],
        out_specs=pl.BlockSpec((tm, tn), lambda j, i: (i, j)),
        out_shape=jax.ShapeDtypeStruct((m, n), _BF16),
        scratch_shapes=[pltpu.VMEM((k, tn), _BF16)],
        compiler_params=pltpu.CompilerParams(
            dimension_semantics=("arbitrary", "arbitrary"), vmem_limit_bytes=VMEM_LIMIT_BYTES),
        name=name,
    )(x, w)


MOBA_HEADS_PER_STEP = 4
_ACC_ROWS = ATT_HEAD_DIM + 16
MOBA_Q_PRESCALE = ATT_HEAD_DIM ** -0.5 * 1.4426950408889634


def _moba_kernel(q_ref, k_ref, v_ref, z_ref, o_ref, vt_ref, kmean_ref, neg_ref, pos_ref, *, n_blocks, heads):
    qi = pl.program_id(2)
    blk = MOBA_BLOCK
    hd = ATT_HEAD_DIM

    @pl.when(qi == 0)
    def _per_head_setup():
        for g in range(heads):
            cols = slice(g * hd, (g + 1) * hd)
            for j in range(n_blocks):
                rows = slice(j * blk, (j + 1) * blk)
                vt_ref[g, j, :hd, :] = v_ref[rows, cols].astype(_F32).T.astype(_BF16)
                vt_ref[g, j, hd:, :] = jnp.ones((_ACC_ROWS - hd, blk), _BF16)
                kmean_ref[g, j:j + 1, :] = (
                    jnp.sum(k_ref[rows, cols].astype(_F32), axis=0, keepdims=True) * (1.0 / blk))

    head_cols = [slice(g * hd, (g + 1) * hd) for g in range(heads)]
    qs = [q_ref[:, c] for c in head_cols]

    def block_scores(j):
        start = pl.multiple_of(j * blk, blk)
        return [lax.dot_general(k_ref[pl.ds(start, blk), c], q, _NT, preferred_element_type=_F32)
                for c, q in zip(head_cols, qs)]

    sel_scores = []
    for g in range(heads):
        km = kmean_ref[g]
        km_hi = km.astype(_BF16)
        km_lo = (km - km_hi.astype(_F32)).astype(_BF16)
        sel_scores.append(lax.dot_general(km_hi, qs[g], _NT, preferred_element_type=_F32)
                          + lax.dot_general(km_lo, qs[g], _NT, preferred_element_type=_F32))
    s_first = block_scores(0)
    for g in range(heads):
        sb = sel_scores[g]
        row_id = lax.broadcasted_iota(jnp.int32, sb.shape, 0)
        rank = jnp.zeros(sb.shape, jnp.int32)
        for j in range(n_blocks):
            other = sb[j:j + 1, :]
            beats = (other > sb) | ((other == sb) & (j < row_id))
            rank = rank + jnp.where(beats & (j < qi), 1, 0)
        selected = (row_id < qi) & (rank < MOBA_TOPK)
        neg_ref[g] = jnp.where(selected, 0.0, -jnp.inf)
        pos_ref[g] = jnp.where(selected, -jnp.inf, jnp.inf)

    def accumulate(scores, block_maxes, sub_floor, j, ms, accs):
        probs, m_news = [], []
        for g in range(heads):
            m_new = jnp.maximum(ms[g], block_maxes[g])
            m_sub = m_new if sub_floor is None else jnp.maximum(m_new, sub_floor[g])
            probs.append(jnp.exp2(scores[g] - m_sub).astype(_BF16))
            m_news.append(m_new)
        out = []
        for g in range(heads):
            alpha = jnp.exp2(jnp.where(m_news[g] == -jnp.inf, 0.0, ms[g] - m_news[g]))
            out.append(alpha * accs[g] + jnp.dot(vt_ref[g, j], probs[g], preferred_element_type=_F32))
        return m_news, out

    def past_block(j, carry):
        scores, ms, accs = carry
        next_scores = block_scores(j + 1)
        maxes = [jnp.max(s, axis=0, keepdims=True) + neg_ref[g, pl.ds(j, 1), :] for g, s in enumerate(scores)]
        floors = [pos_ref[g, pl.ds(j, 1), :] for g in range(heads)]
        ms, accs = accumulate(scores, maxes, floors, j, ms, accs)
        return next_scores, ms, accs

    m0 = [jnp.full((1, blk), -jnp.inf, _F32)] * heads
    acc0 = [jnp.zeros((_ACC_ROWS, blk), _F32)] * heads
    scores, ms, accs = lax.fori_loop(0, qi, past_block, (s_first, m0, acc0))

    key_pos = lax.broadcasted_iota(jnp.int32, (blk, blk), 0)
    q_pos = lax.broadcasted_iota(jnp.int32, (blk, blk), 1)
    scores = [jnp.where(key_pos <= q_pos, s, -jnp.inf) for s in scores]
    maxes = [jnp.max(s, axis=0, keepdims=True) for s in scores]
    _, accs = accumulate(scores, maxes, None, qi, ms, accs)

    for g in range(heads):
        attn = (accs[g][:hd, :] / accs[g][hd:hd + 1, :]).T
        o_ref[:, head_cols[g]] = (attn * _silu(z_ref[:, head_cols[g]].astype(_F32))).astype(o_ref.dtype)


def _moba_attention(proj, batch, seq, *, heads=MOBA_HEADS_PER_STEP):
    n_blocks = seq // MOBA_BLOCK
    proj3 = proj.reshape(batch, seq, PROJ_WIDTH)
    width = heads * ATT_HEAD_DIM
    cols = ATT_WIDTH // width
    return pl.pallas_call(
        functools.partial(_moba_kernel, n_blocks=n_blocks, heads=heads),
        grid=(batch, cols, n_blocks),
        in_specs=[pl.BlockSpec((None, MOBA_BLOCK, width), lambda b, h, i: (b, i, h)),
                  pl.BlockSpec((None, seq, width), lambda b, h, i: (b, 0, cols + h)),
                  pl.BlockSpec((None, seq, width), lambda b, h, i: (b, 0, 2 * cols + h)),
                  pl.BlockSpec((None, MOBA_BLOCK, width), lambda b, h, i: (b, i, 3 * cols + h))],
        out_specs=pl.BlockSpec((None, MOBA_BLOCK, width), lambda b, h, i: (b, i, h)),
        out_shape=jax.ShapeDtypeStruct((batch, seq, ATT_WIDTH), _BF16),
        scratch_shapes=[pltpu.VMEM((heads, n_blocks, _ACC_ROWS, MOBA_BLOCK), _BF16),
                        pltpu.VMEM((heads, n_blocks, ATT_HEAD_DIM), _F32),
                        pltpu.VMEM((heads, n_blocks, MOBA_BLOCK), _F32),
                        pltpu.VMEM((heads, n_blocks, MOBA_BLOCK), _F32)],
        compiler_params=pltpu.CompilerParams(
            dimension_semantics=("arbitrary", "arbitrary", "arbitrary"), vmem_limit_bytes=VMEM_LIMIT_BYTES),
        name="moba_attention",
    )(proj3, proj3, proj3, proj3)


def _sgu_kernel(u_ref, v_ref, z_ref, w_ref, bt_ref, g_ref, b_ref, o_ref, *, tm):
    u = _gelu_tanh(u_ref[...].astype(_F32))
    v = _gelu_tanh(v_ref[...].astype(_F32))
    mu = jnp.mean(v, axis=-1, keepdims=True)
    d = v - mu
    var = jnp.mean(d * d, axis=-1, keepdims=True)
    vn = (d * lax.rsqrt(var + LN_EPS) * g_ref[...] + b_ref[...]).astype(_BF16)
    gate = u * _silu(z_ref[...].astype(_F32))
    t_pos = lax.broadcasted_iota(jnp.int32, (SGU_CHUNK, SGU_CHUNK), 0)
    s_pos = lax.broadcasted_iota(jnp.int32, (SGU_CHUNK, SGU_CHUNK), 1)
    for g in range(SGU_GROUPS):
        w_causal = jnp.where(s_pos <= t_pos, w_ref[g], 0.0).astype(_BF16)
        bias = bt_ref[:, g:g + 1]
        cols = slice(g * SGU_GROUP_DIM, (g + 1) * SGU_GROUP_DIM)
        for c in range(tm // SGU_CHUNK):
            rows = slice(c * SGU_CHUNK, (c + 1) * SGU_CHUNK)
            mixed = jnp.dot(w_causal, vn[rows, cols], preferred_element_type=_F32) + bias
            o_ref[rows, cols] = (gate[rows, cols] * mixed).astype(o_ref.dtype)


def _spatial_gating(proj, w_s, b_s, ln_v_g, ln_v_b, *, tm=512):
    t = proj.shape[0]
    blk = lambda c: pl.BlockSpec((tm, SGU_WIDTH), lambda i: (i, c))
    whole = lambda shape: pl.BlockSpec(shape, lambda i: (0,) * len(shape))
    return pl.pallas_call(
        functools.partial(_sgu_kernel, tm=tm),
        grid=(t // tm,),
        in_specs=[blk(4), blk(5), blk(6),
                  whole((SGU_GROUPS, SGU_CHUNK, SGU_CHUNK)), whole((SGU_CHUNK, SGU_GROUPS)),
                  whole((1, SGU_WIDTH)), whole((1, SGU_WIDTH))],
        out_specs=pl.BlockSpec((tm, SGU_WIDTH), lambda i: (i, 0)),
        out_shape=jax.ShapeDtypeStruct((t, SGU_WIDTH), _BF16),
        compiler_params=pltpu.CompilerParams(
            dimension_semantics=("arbitrary",), vmem_limit_bytes=VMEM_LIMIT_BYTES),
        name="spatial_gating",
    )(proj, proj, proj, w_s, b_s.T, ln_v_g.reshape(1, -1), ln_v_b.reshape(1, -1))


def _mem_attn_kernel(q_ref, z_ref, k_ref, v_ref, o_ref):
    scale = MEM_HEAD_DIM ** -0.5
    for h in range(MEM_HEADS):
        cols = slice(h * MEM_HEAD_DIM, (h + 1) * MEM_HEAD_DIM)
        s = lax.dot_general(q_ref[:, cols], k_ref[:, cols], _NT, preferred_element_type=_F32) * scale
        p = jnp.exp(s - jnp.max(s, axis=-1, keepdims=True))
        l = jnp.sum(p, axis=-1, keepdims=True)
        o = jnp.dot(p.astype(_BF16), v_ref[:, cols], preferred_element_type=_F32) / l
        o_ref[:, cols] = (o * _silu(z_ref[:, cols].astype(_F32))).astype(o_ref.dtype)


def _memory_attention(proj, mem_k, mem_v, batch, seq, *, tm=512):
    proj3 = proj.reshape(batch, seq, PROJ_WIDTH)
    k3 = mem_k.reshape(batch, N_MEM, MEM_WIDTH)
    v3 = mem_v.reshape(batch, N_MEM, MEM_WIDTH)
    return pl.pallas_call(
        _mem_attn_kernel,
        grid=(batch, seq // tm),
        in_specs=[pl.BlockSpec((None, tm, MEM_WIDTH), lambda b, i: (b, i, 7)),
                  pl.BlockSpec((None, tm, MEM_WIDTH), lambda b, i: (b, i, 8)),
                  pl.BlockSpec((None, N_MEM, MEM_WIDTH), lambda b, i: (b, 0, 0)),
                  pl.BlockSpec((None, N_MEM, MEM_WIDTH), lambda b, i: (b, 0, 0))],
        out_specs=pl.BlockSpec((None, tm, MEM_WIDTH), lambda b, i: (b, i, 0)),
        out_shape=jax.ShapeDtypeStruct((batch, seq, MEM_WIDTH), _BF16),
        compiler_params=pltpu.CompilerParams(
            dimension_semantics=("arbitrary", "arbitrary"), vmem_limit_bytes=VMEM_LIMIT_BYTES),
        name="memory_attention",
    )(proj3, proj3, k3, v3)


def _merge_kernel(ya_ref, yg_ref, yc_ref, *rest):
    n_gate = N_BRANCH * D_MODEL // PROJ_TILE
    gate_refs = rest[:n_gate]
    x_ref, wa_ref, wg_ref, wc_ref, wo_ref, lng_ref, lnb_ref, o_ref, merged_ref = rest[n_gate:]
    per_branch = D_MODEL // PROJ_TILE
    for c in range(per_branch):
        cols = slice(c * PROJ_TILE, (c + 1) * PROJ_TILE)
        merged = None
        for br, (y_ref, w_ref) in enumerate(((ya_ref, wa_ref), (yg_ref, wg_ref), (yc_ref, wc_ref))):
            term = (gate_refs[br * per_branch + c][...].astype(_F32)
                    * jnp.dot(y_ref[...], w_ref[:, cols], preferred_element_type=_F32))
            merged = term if merged is None else merged + term
        merged_ref[:, cols] = merged.astype(_BF16)
    y = jnp.dot(merged_ref[...], wo_ref[...], preferred_element_type=_F32)
    h = DN_ALPHA * x_ref[...] + y
    mu = jnp.mean(h, axis=-1, keepdims=True)
    d = h - mu
    var = jnp.mean(d * d, axis=-1, keepdims=True)
    o_ref[...] = d * lax.rsqrt(var + LN_EPS) * lng_ref[...] + lnb_ref[...]


def _merge_project_norm(ya, yg, yc, proj, x2, wa, wg, wc, wo, ln_g, ln_b, *, tm=256):
    t = x2.shape[0]
    n_gate = N_BRANCH * D_MODEL // PROJ_TILE
    branch = pl.BlockSpec((tm, ATT_WIDTH), lambda i: (i, 0))
    gate = lambda c: pl.BlockSpec((tm, PROJ_TILE), lambda i: (i, GATE_TILE0 + c))
    resident = lambda shape: pl.BlockSpec(shape, lambda i: (0, 0), pipeline_mode=pl.Buffered(1))
    return pl.pallas_call(
        _merge_kernel,
        grid=(t // tm,),
        in_specs=[branch, branch, branch, *[gate(c) for c in range(n_gate)],
                  pl.BlockSpec((tm, D_MODEL), lambda i: (i, 0)),
                  resident((ATT_WIDTH, D_MODEL)), resident((SGU_WIDTH, D_MODEL)),
                  resident((MEM_WIDTH, D_MODEL)), resident((D_MODEL, D_MODEL)),
                  resident((1, D_MODEL)), resident((1, D_MODEL))],
        out_specs=pl.BlockSpec((tm, D_MODEL), lambda i: (i, 0)),
        out_shape=jax.ShapeDtypeStruct((t, D_MODEL), _F32),
        scratch_shapes=[pltpu.VMEM((tm, D_MODEL), _BF16)],
        compiler_params=pltpu.CompilerParams(
            dimension_semantics=("arbitrary",), vmem_limit_bytes=VMEM_LIMIT_BYTES),
        name="merge_project_norm",
    )(ya, yg, yc, *([proj] * n_gate), x2, wa, wg, wc, wo, ln_g.reshape(1, -1), ln_b.reshape(1, -1))


def kernel(x, mem, w_in, w_mem_k, w_mem_v, w_s, b_s, ln_v_g, ln_v_b,
           w_branch_attn, w_branch_sgu, w_branch_mem, w_out, ln_g, ln_b):
    batch, seq, d = x.shape
    assert d == D_MODEL and seq % MOBA_BLOCK == 0 and mem.shape[1] == N_MEM
    t = batch * seq
    x2 = x.reshape(t, d)
    x_bf = x2.astype(_BF16)

    assert w_in.shape[1] == PROJ_WIDTH
    proj = _project(x_bf, w_in, tm=1024, tn=PROJ_TILE, name="in_proj",
                    first_tile_scale=MOBA_Q_PRESCALE, sigmoid_from_tile=GATE_TILE0)
    mem_bf = mem.reshape(batch * N_MEM, d).astype(_BF16)
    mem_k = _project(mem_bf, w_mem_k, tm=batch * N_MEM, tn=512, name="mem_k_proj")
    mem_v = _project(mem_bf, w_mem_v, tm=batch * N_MEM, tn=512, name="mem_v_proj")

    ya = _moba_attention(proj, batch, seq).reshape(t, ATT_WIDTH)
    yg = _spatial_gating(proj, w_s, b_s, ln_v_g, ln_v_b)
    yc = _memory_attention(proj, mem_k, mem_v, batch, seq).reshape(t, MEM_WIDTH)

    out = _merge_project_norm(ya, yg, yc, proj, x2,
                              w_branch_attn.astype(_BF16), w_branch_sgu.astype(_BF16),
                              w_branch_mem.astype(_BF16), w_out.astype(_BF16), ln_g, ln_b)
    return out.reshape(batch, seq, d)
```

```python
import functools

import jax
import jax.numpy as jnp
from jax import lax
from jax.experimental import pallas as pl
from jax.experimental.pallas import tpu as pltpu

D_MODEL = 2048
DEPTH = 1
N_MEM = 256
ATT_HEAD_DIM = 128
ATT_WIDTH = D_MODEL // 2
ATT_HEADS = ATT_WIDTH // ATT_HEAD_DIM
MOBA_BLOCK = 256
MOBA_TOPK = 3
SGU_WIDTH = D_MODEL // 2
SGU_CHUNK = 128
SGU_GROUP_DIM = 128
SGU_GROUPS = SGU_WIDTH // SGU_GROUP_DIM
MEM_HEADS = 4
MEM_WIDTH = D_MODEL // 2
MEM_HEAD_DIM = MEM_WIDTH // MEM_HEADS
N_BRANCH = 3
DN_ALPHA = (2 * DEPTH) ** 0.25
LN_EPS = 1e-5

PROJ_TILE = 1024
GATE_TILE0 = 9
PROJ_WIDTH = (GATE_TILE0 + N_BRANCH * D_MODEL // PROJ_TILE) * PROJ_TILE

VMEM_LIMIT_BYTES = 48 * 1024 * 1024

_NT = (((1,), (1,)), ((), ()))
_F32 = jnp.float32
_BF16 = jnp.bfloat16


def _sigmoid(x):
    return 1.0 / (1.0 + jnp.exp(-x))


def _silu(x):
    return x * _sigmoid(x)


def _gelu_tanh(x):
    c = 0.7978845608028654
    return x * (0.5 * (1.0 + jnp.tanh(c * (x + 0.044715 * (x * x * x)))))


def _project_kernel(x_ref, w_ref, o_ref, w_bf_ref, *, first_tile_scale, sigmoid_from_tile):
    j = pl.program_id(0)

    @pl.when(pl.program_id(1) == 0)
    def _cast_weight_tile():
        w_bf_ref[...] = w_ref[...].astype(_BF16)

    acc = jnp.dot(x_ref[...], w_bf_ref[...], preferred_element_type=_F32)
    if sigmoid_from_tile is None:
        o_ref[...] = acc.astype(o_ref.dtype)
        return

    @pl.when(j >= sigmoid_from_tile)
    def _gate_tile():
        o_ref[...] = _sigmoid(acc).astype(o_ref.dtype)

    @pl.when(j < sigmoid_from_tile)
    def _plain_tile():
        o_ref[...] = (acc * jnp.where(j == 0, first_tile_scale, 1.0)).astype(o_ref.dtype)


def _project(x, w, *, tm, tn, name, first_tile_scale=None, sigmoid_from_tile=None):
    m, k = x.shape
    _, n = w.shape
    assert m % tm == 0 and n % tn == 0
    assert (first_tile_scale is None) == (sigmoid_from_tile is None)
    return pl.pallas_call(
        functools.partial(_project_kernel, first_tile_scale=first_tile_scale, sigmoid_from_tile=sigmoid_from_tile),
        grid=(n // tn, m // tm),
        in_specs=[pl.BlockSpec((tm, k), lambda j, i: (i, 0)),
                  pl.BlockSpec((k, tn), lambda j, i: (0, j))],
        out_specs=pl.BlockSpec((tm, tn), lambda j, i: (i, j)),
        out_shape=jax.ShapeDtypeStruct((m, n), _BF16),
        scratch_shapes=[pltpu.VMEM((k, tn), _BF16)],
        compiler_params=pltpu.CompilerParams(
            dimension_semantics=("arbitrary", "arbitrary"), vmem_limit_bytes=VMEM_LIMIT_BYTES),
        name=name,
    )(x, w)


MOBA_HEADS_PER_STEP = 4
_ACC_ROWS = ATT_HEAD_DIM + 16
MOBA_Q_PRESCALE = ATT_HEAD_DIM ** -0.5 * 1.4426950408889634


def _moba_kernel(q_ref, k_ref, v_ref, z_ref, o_ref,
                 vt_ref, kmean_ref, neg_ref, pos_ref, qt_ref, s_ref, acc_ref, *, n_blocks, heads):
    qi = pl.program_id(2)
    blk = MOBA_BLOCK
    hd = ATT_HEAD_DIM

    @pl.when(qi == 0)
    def _per_head_setup():
        for g in range(heads):
            cols = slice(g * hd, (g + 1) * hd)
            for j in range(n_blocks):
                rows = slice(j * blk, (j + 1) * blk)
                vt_ref[g, j, :hd, :] = v_ref[rows, cols].astype(_F32).T.astype(_BF16)
                vt_ref[g, j, hd:, :] = jnp.ones((_ACC_ROWS - hd, blk), _BF16)
                kmean_ref[g, j:j + 1, :] = (
                    jnp.sum(k_ref[rows, cols].astype(_F32), axis=0, keepdims=True) * (1.0 / blk))

    head_cols = [slice(g * hd, (g + 1) * hd) for g in range(heads)]
    qs = [q_ref[:, c] for c in head_cols]
    for g in range(heads):
        qt_ref[g] = qs[g].astype(_F32).T.astype(_BF16)

    def issue_scores(j, slot):
        start = j * blk if isinstance(j, int) else pl.multiple_of(j * blk, blk)
        for g in range(heads):
            s_ref[slot, g] = jnp.dot(k_ref[pl.ds(start, blk), head_cols[g]], qt_ref[g],
                                     preferred_element_type=_F32)

    sel_scores = []
    for g in range(heads):
        km = kmean_ref[g]
        km_hi = km.astype(_BF16)
        km_lo = (km - km_hi.astype(_F32)).astype(_BF16)
        sel_scores.append(jnp.dot(km_hi, qt_ref[g], preferred_element_type=_F32)
                          + jnp.dot(km_lo, qt_ref[g], preferred_element_type=_F32))
    issue_scores(qi, 2)
    for g in range(heads):
        sb = sel_scores[g]
        row_id = lax.broadcasted_iota(jnp.int32, sb.shape, 0)
        rank = jnp.zeros(sb.shape, jnp.int32)
        for j in range(n_blocks):
            other = sb[j:j + 1, :]
            beats = (other > sb) | ((other == sb) & (j < row_id))
            rank = rank + jnp.where(beats & (j < qi), 1, 0)
        selected = (row_id < qi) & (rank < MOBA_TOPK)
        neg_ref[g] = jnp.where(selected, 0.0, -jnp.inf)
        pos_ref[g] = jnp.where(selected, -jnp.inf, jnp.inf)

    def accumulate(j, slot, ms, causal):
        probs, m_news = [], []
        for g in range(heads):
            s = s_ref[slot, g]
            if causal:
                key_pos = lax.broadcasted_iota(jnp.int32, s.shape, 0)
                q_pos = lax.broadcasted_iota(jnp.int32, s.shape, 1)
                s = jnp.where(key_pos <= q_pos, s, -jnp.inf)
                m_new = jnp.maximum(ms[g], jnp.max(s, axis=0, keepdims=True))
                m_sub = m_new
            else:
                m_new = jnp.maximum(ms[g], jnp.max(s, axis=0, keepdims=True) + neg_ref[g, pl.ds(j, 1), :])
                m_sub = jnp.maximum(m_new, pos_ref[g, pl.ds(j, 1), :])
            probs.append(jnp.exp2(s - m_sub).astype(_BF16))
            m_news.append(m_new)
        for g in range(heads):
            alpha = jnp.exp2(jnp.where(m_news[g] == -jnp.inf, 0.0, ms[g] - m_news[g]))
            acc_ref[g] = alpha * acc_ref[g] + jnp.dot(vt_ref[g, j], probs[g], preferred_element_type=_F32)
        return m_news

    def past_block_pair(t, ms):
        issue_scores(2 * t + 1, 1)
        ms = accumulate(2 * t, 0, ms, causal=False)
        issue_scores(jnp.minimum(2 * t + 2, n_blocks - 1), 0)
        return accumulate(2 * t + 1, 1, ms, causal=False)

    for g in range(heads):
        acc_ref[g] = jnp.zeros((_ACC_ROWS, blk), _F32)
    issue_scores(0, 0)
    ms = accumulate(qi, 2, [jnp.full((1, blk), -jnp.inf, _F32)] * heads, causal=True)
    lax.fori_loop(0, (qi + 1) // 2, past_block_pair, ms)

    for g in range(heads):
        attn = (acc_ref[g, :hd, :] / acc_ref[g, hd:hd + 1, :]).T
        o_ref[:, head_cols[g]] = (attn * _silu(z_ref[:, head_cols[g]].astype(_F32))).astype(o_ref.dtype)


def _moba_attention(proj, batch, seq, *, heads=MOBA_HEADS_PER_STEP):
    n_blocks = seq // MOBA_BLOCK
    proj3 = proj.reshape(batch, seq, PROJ_WIDTH)
    width = heads * ATT_HEAD_DIM
    cols = ATT_WIDTH // width
    return pl.pallas_call(
        functools.partial(_moba_kernel, n_blocks=n_blocks, heads=heads),
        grid=(batch, cols, n_blocks),
        in_specs=[pl.BlockSpec((None, MOBA_BLOCK, width), lambda b, h, i: (b, i, h)),
                  pl.BlockSpec((None, seq, width), lambda b, h, i: (b, 0, cols + h)),
                  pl.BlockSpec((None, seq, width), lambda b, h, i: (b, 0, 2 * cols + h)),
                  pl.BlockSpec((None, MOBA_BLOCK, width), lambda b, h, i: (b, i, 3 * cols + h))],
        out_specs=pl.BlockSpec((None, MOBA_BLOCK, width), lambda b, h, i: (b, i, h)),
        out_shape=jax.ShapeDtypeStruct((batch, seq, ATT_WIDTH), _BF16),
        scratch_shapes=[pltpu.VMEM((heads, n_blocks, _ACC_ROWS, MOBA_BLOCK), _BF16),
                        pltpu.VMEM((heads, n_blocks, ATT_HEAD_DIM), _F32),
                        pltpu.VMEM((heads, n_blocks, MOBA_BLOCK), _F32),
                        pltpu.VMEM((heads, n_blocks, MOBA_BLOCK), _F32),
                        pltpu.VMEM((heads, ATT_HEAD_DIM, MOBA_BLOCK), _BF16),
                        pltpu.VMEM((3, heads, MOBA_BLOCK, MOBA_BLOCK), _F32),
                        pltpu.VMEM((heads, _ACC_ROWS, MOBA_BLOCK), _F32)],
        compiler_params=pltpu.CompilerParams(
            dimension_semantics=("arbitrary", "arbitrary", "arbitrary"), vmem_limit_bytes=VMEM_LIMIT_BYTES),
        name="moba_attention",
    )(proj3, proj3, proj3, proj3)


def _sgu_kernel(u_ref, v_ref, z_ref, w_ref, bt_ref, g_ref, b_ref, o_ref, *, tm):
    u = _gelu_tanh(u_ref[...].astype(_F32))
    v = _gelu_tanh(v_ref[...].astype(_F32))
    mu = jnp.mean(v, axis=-1, keepdims=True)
    d = v - mu
    var = jnp.mean(d * d, axis=-1, keepdims=True)
    vn = (d * lax.rsqrt(var + LN_EPS) * g_ref[...] + b_ref[...]).astype(_BF16)
    gate = u * _silu(z_ref[...].astype(_F32))
    t_pos = lax.broadcasted_iota(jnp.int32, (SGU_CHUNK, SGU_CHUNK), 0)
    s_pos = lax.broadcasted_iota(jnp.int32, (SGU_CHUNK, SGU_CHUNK), 1)
    for g in range(SGU_GROUPS):
        w_causal = jnp.where(s_pos <= t_pos, w_ref[g], 0.0).astype(_BF16)
        bias = bt_ref[:, g:g + 1]
        cols = slice(g * SGU_GROUP_DIM, (g + 1) * SGU_GROUP_DIM)
        for c in range(tm // SGU_CHUNK):
            rows = slice(c * SGU_CHUNK, (c + 1) * SGU_CHUNK)
            mixed = jnp.dot(w_causal, vn[rows, cols], preferred_element_type=_F32) + bias
            o_ref[rows, cols] = (gate[rows, cols] * mixed).astype(o_ref.dtype)


def _spatial_gating(proj, w_s, b_s, ln_v_g, ln_v_b, *, tm=512):
    t = proj.shape[0]
    blk = lambda c: pl.BlockSpec((tm, SGU_WIDTH), lambda i: (i, c))
    whole = lambda shape: pl.BlockSpec(shape, lambda i: (0,) * len(shape))
    return pl.pallas_call(
        functools.partial(_sgu_kernel, tm=tm),
        grid=(t // tm,),
        in_specs=[blk(4), blk(5), blk(6),
                  whole((SGU_GROUPS, SGU_CHUNK, SGU_CHUNK)), whole((SGU_CHUNK, SGU_GROUPS)),
                  whole((1, SGU_WIDTH)), whole((1, SGU_WIDTH))],
        out_specs=pl.BlockSpec((tm, SGU_WIDTH), lambda i: (i, 0)),
        out_shape=jax.ShapeDtypeStruct((t, SGU_WIDTH), _BF16),
        compiler_params=pltpu.CompilerParams(
            dimension_semantics=("arbitrary",), vmem_limit_bytes=VMEM_LIMIT_BYTES),
        name="spatial_gating",
    )(proj, proj, proj, w_s, b_s.T, ln_v_g.reshape(1, -1), ln_v_b.reshape(1, -1))


def _mem_attn_kernel(q_ref, z_ref, k_ref, v_ref, o_ref):
    scale = MEM_HEAD_DIM ** -0.5
    for h in range(MEM_HEADS):
        cols = slice(h * MEM_HEAD_DIM, (h + 1) * MEM_HEAD_DIM)
        s = lax.dot_general(q_ref[:, cols], k_ref[:, cols], _NT, preferred_element_type=_F32) * scale
        p = jnp.exp(s - jnp.max(s, axis=-1, keepdims=True))
        l = jnp.sum(p, axis=-1, keepdims=True)
        o = jnp.dot(p.astype(_BF16), v_ref[:, cols], preferred_element_type=_F32) / l
        o_ref[:, cols] = (o * _silu(z_ref[:, cols].astype(_F32))).astype(o_ref.dtype)


def _memory_attention(proj, mem_k, mem_v, batch, seq, *, tm=512):
    proj3 = proj.reshape(batch, seq, PROJ_WIDTH)
    k3 = mem_k.reshape(batch, N_MEM, MEM_WIDTH)
    v3 = mem_v.reshape(batch, N_MEM, MEM_WIDTH)
    return pl.pallas_call(
        _mem_attn_kernel,
        grid=(batch, seq // tm),
        in_specs=[pl.BlockSpec((None, tm, MEM_WIDTH), lambda b, i: (b, i, 7)),
                  pl.BlockSpec((None, tm, MEM_WIDTH), lambda b, i: (b, i, 8)),
                  pl.BlockSpec((None, N_MEM, MEM_WIDTH), lambda b, i: (b, 0, 0)),
                  pl.BlockSpec((None, N_MEM, MEM_WIDTH), lambda b, i: (b, 0, 0))],
        out_specs=pl.BlockSpec((None, tm, MEM_WIDTH), lambda b, i: (b, i, 0)),
        out_shape=jax.ShapeDtypeStruct((batch, seq, MEM_WIDTH), _BF16),
        compiler_params=pltpu.CompilerParams(
            dimension_semantics=("arbitrary", "arbitrary"), vmem_limit_bytes=VMEM_LIMIT_BYTES),
        name="memory_attention",
    )(proj3, proj3, k3, v3)


def _merge_kernel(ya_ref, yg_ref, yc_ref, *rest):
    n_gate = N_BRANCH * D_MODEL // PROJ_TILE
    gate_refs = rest[:n_gate]
    x_ref, wa_ref, wg_ref, wc_ref, wo_ref, lng_ref, lnb_ref, o_ref, merged_ref = rest[n_gate:]
    per_branch = D_MODEL // PROJ_TILE
    for c in range(per_branch):
        cols = slice(c * PROJ_TILE, (c + 1) * PROJ_TILE)
        merged = None
        for br, (y_ref, w_ref) in enumerate(((ya_ref, wa_ref), (yg_ref, wg_ref), (yc_ref, wc_ref))):
            term = (gate_refs[br * per_branch + c][...].astype(_F32)
                    * jnp.dot(y_ref[...], w_ref[:, cols], preferred_element_type=_F32))
            merged = term if merged is None else merged + term
        merged_ref[:, cols] = merged.astype(_BF16)
    y = jnp.dot(merged_ref[...], wo_ref[...], preferred_element_type=_F32)
    h = DN_ALPHA * x_ref[...] + y
    mu = jnp.mean(h, axis=-1, keepdims=True)
    d = h - mu
    var = jnp.mean(d * d, axis=-1, keepdims=True)
    o_ref[...] = d * lax.rsqrt(var + LN_EPS) * lng_ref[...] + lnb_ref[...]


def _merge_project_norm(ya, yg, yc, proj, x2, wa, wg, wc, wo, ln_g, ln_b, *, tm=256):
    t = x2.shape[0]
    n_gate = N_BRANCH * D_MODEL // PROJ_TILE
    branch = pl.BlockSpec((tm, ATT_WIDTH), lambda i: (i, 0))
    gate = lambda c: pl.BlockSpec((tm, PROJ_TILE), lambda i: (i, GATE_TILE0 + c))
    resident = lambda shape: pl.BlockSpec(shape, lambda i: (0, 0), pipeline_mode=pl.Buffered(1))
    return pl.pallas_call(
        _merge_kernel,
        grid=(t // tm,),
        in_specs=[branch, branch, branch, *[gate(c) for c in range(n_gate)],
                  pl.BlockSpec((tm, D_MODEL), lambda i: (i, 0)),
                  resident((ATT_WIDTH, D_MODEL)), resident((SGU_WIDTH, D_MODEL)),
                  resident((MEM_WIDTH, D_MODEL)), resident((D_MODEL, D_MODEL)),
                  resident((1, D_MODEL)), resident((1, D_MODEL))],
        out_specs=pl.BlockSpec((tm, D_MODEL), lambda i: (i, 0)),
        out_shape=jax.ShapeDtypeStruct((t, D_MODEL), _F32),
        scratch_shapes=[pltpu.VMEM((tm, D_MODEL), _BF16)],
        compiler_params=pltpu.CompilerParams(
            dimension_semantics=("arbitrary",), vmem_limit_bytes=VMEM_LIMIT_BYTES),
        name="merge_project_norm",
    )(ya, yg, yc, *([proj] * n_gate), x2, wa, wg, wc, wo, ln_g.reshape(1, -1), ln_b.reshape(1, -1))


def kernel(x, mem, w_in, w_mem_k, w_mem_v, w_s, b_s, ln_v_g, ln_v_b,
           w_branch_attn, w_branch_sgu, w_branch_mem, w_out, ln_g, ln_b):
    batch, seq, d = x.shape
    assert d == D_MODEL and seq % MOBA_BLOCK == 0 and mem.shape[1] == N_MEM
    t = batch * seq
    x2 = x.reshape(t, d)
    x_bf = x2.astype(_BF16)

    assert w_in.shape[1] == PROJ_WIDTH
    proj = _project(x_bf, w_in, tm=1024, tn=PROJ_TILE, name="in_proj",
                    first_tile_scale=MOBA_Q_PRESCALE, sigmoid_from_tile=GATE_TILE0)
    mem_bf = mem.reshape(batch * N_MEM, d).astype(_BF16)
    mem_k = _project(mem_bf, w_mem_k, tm=batch * N_MEM, tn=512, name="mem_k_proj")
    mem_v = _project(mem_bf, w_mem_v, tm=batch * N_MEM, tn=512, name="mem_v_proj")

    ya = _moba_attention(proj, batch, seq).reshape(t, ATT_WIDTH)
    yg = _spatial_gating(proj, w_s, b_s, ln_v_g, ln_v_b)
    yc = _memory_attention(proj, mem_k, mem_v, batch, seq).reshape(t, MEM_WIDTH)

    out = _merge_project_norm(ya, yg, yc, proj, x2,
                              w_branch_attn.astype(_BF16), w_branch_sgu.astype(_BF16),
                              w_branch_mem.astype(_BF16), w_out.astype(_BF16), ln_g, ln_b)
    return out.reshape(batch, seq, d)
```

```python
import functools

import jax
import jax.numpy as jnp
from jax import lax
from jax.experimental import pallas as pl
from jax.experimental.pallas import tpu as pltpu

D_MODEL = 2048
DEPTH = 1
N_MEM = 256
ATT_HEAD_DIM = 128
ATT_WIDTH = D_MODEL // 2
ATT_HEADS = ATT_WIDTH // ATT_HEAD_DIM
MOBA_BLOCK = 256
MOBA_TOPK = 3
SGU_WIDTH = D_MODEL // 2
SGU_CHUNK = 128
SGU_GROUP_DIM = 128
SGU_GROUPS = SGU_WIDTH // SGU_GROUP_DIM
MEM_HEADS = 4
MEM_WIDTH = D_MODEL // 2
MEM_HEAD_DIM = MEM_WIDTH // MEM_HEADS
N_BRANCH = 3
DN_ALPHA = (2 * DEPTH) ** 0.25
LN_EPS = 1e-5

PROJ_TILE = 1024
GATE_TILE0 = 9
PROJ_WIDTH = GATE_TILE0 * PROJ_TILE

VMEM_LIMIT_BYTES = 48 * 1024 * 1024
IN_PROJ_ROW_CHUNKS = 4
MERGE_ROW_CHUNKS = 1

_NT = (((1,), (1,)), ((), ()))
_F32 = jnp.float32
_BF16 = jnp.bfloat16


def _sigmoid(x):
    return 1.0 / (1.0 + jnp.exp(-x))


def _silu(x):
    return x * _sigmoid(x)


def _gelu_tanh(x):
    c = 0.7978845608028654
    return x * (0.5 * (1.0 + jnp.tanh(c * (x + 0.044715 * (x * x * x)))))


def _project_kernel(x_ref, w_ref, o_ref, w_bf_ref, *, first_tile_scale, sigmoid, row_chunks):
    @pl.when(pl.program_id(1) == 0)
    def _cast_weight_tile():
        w_bf_ref[...] = w_ref[...].astype(_BF16)

    tc = x_ref.shape[0] // row_chunks
    for c in range(row_chunks):
        rows = slice(c * tc, (c + 1) * tc)
        acc = jnp.dot(x_ref[rows, :], w_bf_ref[...], preferred_element_type=_F32)
        if sigmoid:
            acc = _sigmoid(acc)
        if first_tile_scale is not None:
            acc = acc * jnp.where(pl.program_id(0) == 0, first_tile_scale, 1.0)
        o_ref[rows, :] = acc.astype(o_ref.dtype)


def _project(x, w, *, tm, tn, name, first_col_tile=0, n_col_tiles=None,
             first_tile_scale=None, sigmoid=False, row_chunks=1):
    m, k = x.shape
    if n_col_tiles is None:
        n_col_tiles = w.shape[1] // tn - first_col_tile
    assert m % (tm * row_chunks) == 0 and (first_col_tile + n_col_tiles) * tn <= w.shape[1]
    return pl.pallas_call(
        functools.partial(_project_kernel, first_tile_scale=first_tile_scale, sigmoid=sigmoid,
                          row_chunks=row_chunks),
        grid=(n_col_tiles, m // tm),
        in_specs=[pl.BlockSpec((tm, k), lambda j, i: (i, 0)),
                  pl.BlockSpec((k, tn), lambda j, i: (0, first_col_tile + j))],
        out_specs=pl.BlockSpec((tm, tn), lambda j, i: (i, j)),
        out_shape=jax.ShapeDtypeStruct((m, n_col_tiles * tn), _BF16),
        scratch_shapes=[pltpu.VMEM((k, tn), _BF16)],
        compiler_params=pltpu.CompilerParams(
            dimension_semantics=("arbitrary", "arbitrary"), vmem_limit_bytes=VMEM_LIMIT_BYTES),
        name=name,
    )(x, w)


MOBA_HEADS_PER_STEP = 4
_ACC_ROWS = ATT_HEAD_DIM + 16
MOBA_Q_PRESCALE = ATT_HEAD_DIM ** -0.5 * 1.4426950408889634


def _moba_kernel(q_ref, k_ref, v_ref, z_ref, o_ref,
                 vt_ref, kmean_ref, neg_ref, pos_ref, qt_ref, s_ref, acc_ref, *, n_blocks, heads):
    qi = pl.program_id(2)
    blk = MOBA_BLOCK
    hd = ATT_HEAD_DIM

    @pl.when(qi == 0)
    def _per_head_setup():
        for g in range(heads):
            cols = slice(g * hd, (g + 1) * hd)
            for j in range(n_blocks):
                rows = slice(j * blk, (j + 1) * blk)
                vt_ref[g, j, :hd, :] = v_ref[rows, cols].astype(_F32).T.astype(_BF16)
                vt_ref[g, j, hd:, :] = jnp.ones((_ACC_ROWS - hd, blk), _BF16)
                kmean_ref[g, j:j + 1, :] = (
                    jnp.sum(k_ref[rows, cols].astype(_F32), axis=0, keepdims=True) * (1.0 / blk))

    head_cols = [slice(g * hd, (g + 1) * hd) for g in range(heads)]
    qs = [q_ref[:, c] for c in head_cols]
    for g in range(heads):
        qt_ref[g] = qs[g].astype(_F32).T.astype(_BF16)

    def issue_scores(j, slot):
        start = j * blk if isinstance(j, int) else pl.multiple_of(j * blk, blk)
        for g in range(heads):
            s_ref[slot, g] = jnp.dot(k_ref[pl.ds(start, blk), head_cols[g]], qt_ref[g],
                                     preferred_element_type=_F32)

    sel_scores = []
    for g in range(heads):
        km = kmean_ref[g]
        km_hi = km.astype(_BF16)
        km_lo = (km - km_hi.astype(_F32)).astype(_BF16)
        sel_scores.append(jnp.dot(km_hi, qt_ref[g], preferred_element_type=_F32)
                          + jnp.dot(km_lo, qt_ref[g], preferred_element_type=_F32))
    issue_scores(qi, 2)
    for g in range(heads):
        sb = sel_scores[g]
        row_id = lax.broadcasted_iota(jnp.int32, sb.shape, 0)
        rank = jnp.zeros(sb.shape, jnp.int32)
        for j in range(n_blocks):
            other = sb[j:j + 1, :]
            beats = (other > sb) | ((other == sb) & (j < row_id))
            rank = rank + jnp.where(beats & (j < qi), 1, 0)
        selected = (row_id < qi) & (rank < MOBA_TOPK)
        neg_ref[g] = jnp.where(selected, 0.0, -jnp.inf)
        pos_ref[g] = jnp.where(selected, -jnp.inf, jnp.inf)

    def accumulate(j, slot, ms, causal):
        probs, m_news = [], []
        for g in range(heads):
            s = s_ref[slot, g]
            if causal:
                key_pos = lax.broadcasted_iota(jnp.int32, s.shape, 0)
                q_pos = lax.broadcasted_iota(jnp.int32, s.shape, 1)
                s = jnp.where(key_pos <= q_pos, s, -jnp.inf)
                m_new = jnp.maximum(ms[g], jnp.max(s, axis=0, keepdims=True))
                m_sub = m_new
            else:
                m_new = jnp.maximum(ms[g], jnp.max(s, axis=0, keepdims=True) + neg_ref[g, pl.ds(j, 1), :])
                m_sub = jnp.maximum(m_new, pos_ref[g, pl.ds(j, 1), :])
            probs.append(jnp.exp2(s - m_sub).astype(_BF16))
            m_news.append(m_new)
        for g in range(heads):
            alpha = jnp.exp2(jnp.where(m_news[g] == -jnp.inf, 0.0, ms[g] - m_news[g]))
            acc_ref[g] = alpha * acc_ref[g] + jnp.dot(vt_ref[g, j], probs[g], preferred_element_type=_F32)
        return m_news

    def past_block_pair(t, ms):
        issue_scores(2 * t + 1, 1)
        ms = accumulate(2 * t, 0, ms, causal=False)
        issue_scores(jnp.minimum(2 * t + 2, n_blocks - 1), 0)
        return accumulate(2 * t + 1, 1, ms, causal=False)

    for g in range(heads):
        acc_ref[g] = jnp.zeros((_ACC_ROWS, blk), _F32)
    issue_scores(0, 0)
    ms = accumulate(qi, 2, [jnp.full((1, blk), -jnp.inf, _F32)] * heads, causal=True)
    lax.fori_loop(0, (qi + 1) // 2, past_block_pair, ms)

    for g in range(heads):
        attn = (acc_ref[g, :hd, :] / acc_ref[g, hd:hd + 1, :]).T
        o_ref[:, head_cols[g]] = (attn * _silu(z_ref[:, head_cols[g]].astype(_F32))).astype(o_ref.dtype)


def _moba_attention(proj, batch, seq, *, heads=MOBA_HEADS_PER_STEP):
    n_blocks = seq // MOBA_BLOCK
    proj3 = proj.reshape(batch, seq, PROJ_WIDTH)
    width = heads * ATT_HEAD_DIM
    cols = ATT_WIDTH // width
    return pl.pallas_call(
        functools.partial(_moba_kernel, n_blocks=n_blocks, heads=heads),
        grid=(batch, cols, n_blocks),
        in_specs=[pl.BlockSpec((None, MOBA_BLOCK, width), lambda b, h, i: (b, i, h)),
                  pl.BlockSpec((None, seq, width), lambda b, h, i: (b, 0, cols + h)),
                  pl.BlockSpec((None, seq, width), lambda b, h, i: (b, 0, 2 * cols + h)),
                  pl.BlockSpec((None, MOBA_BLOCK, width), lambda b, h, i: (b, i, 3 * cols + h))],
        out_specs=pl.BlockSpec((None, MOBA_BLOCK, width), lambda b, h, i: (b, i, h)),
        out_shape=jax.ShapeDtypeStruct((batch, seq, ATT_WIDTH), _BF16),
        scratch_shapes=[pltpu.VMEM((heads, n_blocks, _ACC_ROWS, MOBA_BLOCK), _BF16),
                        pltpu.VMEM((heads, n_blocks, ATT_HEAD_DIM), _F32),
                        pltpu.VMEM((heads, n_blocks, MOBA_BLOCK), _F32),
                        pltpu.VMEM((heads, n_blocks, MOBA_BLOCK), _F32),
                        pltpu.VMEM((heads, ATT_HEAD_DIM, MOBA_BLOCK), _BF16),
                        pltpu.VMEM((3, heads, MOBA_BLOCK, MOBA_BLOCK), _F32),
                        pltpu.VMEM((heads, _ACC_ROWS, MOBA_BLOCK), _F32)],
        compiler_params=pltpu.CompilerParams(
            dimension_semantics=("arbitrary", "arbitrary", "arbitrary"), vmem_limit_bytes=VMEM_LIMIT_BYTES),
        name="moba_attention",
    )(proj3, proj3, proj3, proj3)


def _sgu_kernel(u_ref, v_ref, z_ref, w_ref, bt_ref, g_ref, b_ref, o_ref, *, tm):
    u = _gelu_tanh(u_ref[...].astype(_F32))
    v = _gelu_tanh(v_ref[...].astype(_F32))
    mu = jnp.mean(v, axis=-1, keepdims=True)
    d = v - mu
    var = jnp.mean(d * d, axis=-1, keepdims=True)
    vn = (d * lax.rsqrt(var + LN_EPS) * g_ref[...] + b_ref[...]).astype(_BF16)
    gate = u * _silu(z_ref[...].astype(_F32))
    t_pos = lax.broadcasted_iota(jnp.int32, (SGU_CHUNK, SGU_CHUNK), 0)
    s_pos = lax.broadcasted_iota(jnp.int32, (SGU_CHUNK, SGU_CHUNK), 1)
    for g in range(SGU_GROUPS):
        w_causal = jnp.where(s_pos <= t_pos, w_ref[g], 0.0).astype(_BF16)
        bias = bt_ref[:, g:g + 1]
        cols = slice(g * SGU_GROUP_DIM, (g + 1) * SGU_GROUP_DIM)
        for c in range(tm // SGU_CHUNK):
            rows = slice(c * SGU_CHUNK, (c + 1) * SGU_CHUNK)
            mixed = jnp.dot(w_causal, vn[rows, cols], preferred_element_type=_F32) + bias
            o_ref[rows, cols] = (gate[rows, cols] * mixed).astype(o_ref.dtype)


def _spatial_gating(proj, w_s, b_s, ln_v_g, ln_v_b, *, tm=512):
    t = proj.shape[0]
    blk = lambda c: pl.BlockSpec((tm, SGU_WIDTH), lambda i: (i, c))
    whole = lambda shape: pl.BlockSpec(shape, lambda i: (0,) * len(shape))
    return pl.pallas_call(
        functools.partial(_sgu_kernel, tm=tm),
        grid=(t // tm,),
        in_specs=[blk(4), blk(5), blk(6),
                  whole((SGU_GROUPS, SGU_CHUNK, SGU_CHUNK)), whole((SGU_CHUNK, SGU_GROUPS)),
                  whole((1, SGU_WIDTH)), whole((1, SGU_WIDTH))],
        out_specs=pl.BlockSpec((tm, SGU_WIDTH), lambda i: (i, 0)),
        out_shape=jax.ShapeDtypeStruct((t, SGU_WIDTH), _BF16),
        compiler_params=pltpu.CompilerParams(
            dimension_semantics=("arbitrary",), vmem_limit_bytes=VMEM_LIMIT_BYTES),
        name="spatial_gating",
    )(proj, proj, proj, w_s, b_s.T, ln_v_g.reshape(1, -1), ln_v_b.reshape(1, -1))


def _mem_attn_kernel(q_ref, z_ref, k_ref, v_ref, o_ref):
    scale = MEM_HEAD_DIM ** -0.5
    for h in range(MEM_HEADS):
        cols = slice(h * MEM_HEAD_DIM, (h + 1) * MEM_HEAD_DIM)
        s = lax.dot_general(q_ref[:, cols], k_ref[:, cols], _NT, preferred_element_type=_F32) * scale
        p = jnp.exp(s - jnp.max(s, axis=-1, keepdims=True))
        l = jnp.sum(p, axis=-1, keepdims=True)
        o = jnp.dot(p.astype(_BF16), v_ref[:, cols], preferred_element_type=_F32) / l
        o_ref[:, cols] = (o * _silu(z_ref[:, cols].astype(_F32))).astype(o_ref.dtype)


def _memory_attention(proj, mem_k, mem_v, batch, seq, *, tm=512):
    proj3 = proj.reshape(batch, seq, PROJ_WIDTH)
    k3 = mem_k.reshape(batch, N_MEM, MEM_WIDTH)
    v3 = mem_v.reshape(batch, N_MEM, MEM_WIDTH)
    return pl.pallas_call(
        _mem_attn_kernel,
        grid=(batch, seq // tm),
        in_specs=[pl.BlockSpec((None, tm, MEM_WIDTH), lambda b, i: (b, i, 7)),
                  pl.BlockSpec((None, tm, MEM_WIDTH), lambda b, i: (b, i, 8)),
                  pl.BlockSpec((None, N_MEM, MEM_WIDTH), lambda b, i: (b, 0, 0)),
                  pl.BlockSpec((None, N_MEM, MEM_WIDTH), lambda b, i: (b, 0, 0))],
        out_specs=pl.BlockSpec((None, tm, MEM_WIDTH), lambda b, i: (b, i, 0)),
        out_shape=jax.ShapeDtypeStruct((batch, seq, MEM_WIDTH), _BF16),
        compiler_params=pltpu.CompilerParams(
            dimension_semantics=("arbitrary", "arbitrary"), vmem_limit_bytes=VMEM_LIMIT_BYTES),
        name="memory_attention",
    )(proj3, proj3, k3, v3)


def _merge_kernel(ya_ref, yg_ref, yc_ref, *rest):
    n_gate = N_BRANCH * D_MODEL // PROJ_TILE
    gate_refs = rest[:n_gate]
    x_ref, wa_ref, wg_ref, wc_ref, wo_ref, lng_ref, lnb_ref, o_ref, merged_ref = rest[n_gate:]
    per_branch = D_MODEL // PROJ_TILE
    tc = x_ref.shape[0] // MERGE_ROW_CHUNKS
    for r in range(MERGE_ROW_CHUNKS):
        rows = slice(r * tc, (r + 1) * tc)
        for c in range(per_branch):
            cols = slice(c * PROJ_TILE, (c + 1) * PROJ_TILE)
            merged = None
            for br, (y_ref, w_ref) in enumerate(((ya_ref, wa_ref), (yg_ref, wg_ref), (yc_ref, wc_ref))):
                term = (gate_refs[br * per_branch + c][rows, :].astype(_F32)
                        * jnp.dot(y_ref[rows, :], w_ref[:, cols], preferred_element_type=_F32))
                merged = term if merged is None else merged + term
            merged_ref[rows, cols] = merged.astype(_BF16)
        y = jnp.dot(merged_ref[rows, :], wo_ref[...], preferred_element_type=_F32)
        h = DN_ALPHA * x_ref[rows, :] + y
        mu = jnp.mean(h, axis=-1, keepdims=True)
        d = h - mu
        var = jnp.mean(d * d, axis=-1, keepdims=True)
        o_ref[rows, :] = d * lax.rsqrt(var + LN_EPS) * lng_ref[...] + lnb_ref[...]


def _merge_project_norm(ya, yg, yc, gates, x2, wa, wg, wc, wo, ln_g, ln_b, *, tm=256):
    t = x2.shape[0]
    n_gate = N_BRANCH * D_MODEL // PROJ_TILE
    branch = pl.BlockSpec((tm, ATT_WIDTH), lambda i: (i, 0))
    gate = lambda c: pl.BlockSpec((tm, PROJ_TILE), lambda i: (i, c))
    resident = lambda shape: pl.BlockSpec(shape, lambda i: (0, 0), pipeline_mode=pl.Buffered(1))
    return pl.pallas_call(
        _merge_kernel,
        grid=(t // tm,),
        in_specs=[branch, branch, branch, *[gate(c) for c in range(n_gate)],
                  pl.BlockSpec((tm, D_MODEL), lambda i: (i, 0)),
                  resident((ATT_WIDTH, D_MODEL)), resident((SGU_WIDTH, D_MODEL)),
                  resident((MEM_WIDTH, D_MODEL)), resident((D_MODEL, D_MODEL)),
                  resident((1, D_MODEL)), resident((1, D_MODEL))],
        out_specs=pl.BlockSpec((tm, D_MODEL), lambda i: (i, 0)),
        out_shape=jax.ShapeDtypeStruct((t, D_MODEL), _F32),
        scratch_shapes=[pltpu.VMEM((tm, D_MODEL), _BF16)],
        compiler_params=pltpu.CompilerParams(
            dimension_semantics=("arbitrary",), vmem_limit_bytes=VMEM_LIMIT_BYTES),
        name="merge_project_norm",
    )(ya, yg, yc, *([gates] * n_gate), x2, wa, wg, wc, wo, ln_g.reshape(1, -1), ln_b.reshape(1, -1))


def kernel(x, mem, w_in, w_mem_k, w_mem_v, w_s, b_s, ln_v_g, ln_v_b,
           w_branch_attn, w_branch_sgu, w_branch_mem, w_out, ln_g, ln_b):
    batch, seq, d = x.shape
    assert d == D_MODEL and seq % MOBA_BLOCK == 0 and mem.shape[1] == N_MEM
    t = batch * seq
    x2 = x.reshape(t, d)
    x_bf = x2.astype(_BF16)

    assert w_in.shape[1] == PROJ_WIDTH + N_BRANCH * D_MODEL
    proj = _project(x_bf, w_in, tm=1024, tn=PROJ_TILE, name="in_proj_branches", n_col_tiles=GATE_TILE0,
                    first_tile_scale=MOBA_Q_PRESCALE, row_chunks=IN_PROJ_ROW_CHUNKS)
    gates = _project(x_bf, w_in, tm=1024, tn=PROJ_TILE, name="in_proj_gates", first_col_tile=GATE_TILE0,
                     sigmoid=True, row_chunks=IN_PROJ_ROW_CHUNKS)
    mem_bf = mem.reshape(batch * N_MEM, d).astype(_BF16)
    mem_k = _project(mem_bf, w_mem_k, tm=batch * N_MEM, tn=512, name="mem_k_proj")
    mem_v = _project(mem_bf, w_mem_v, tm=batch * N_MEM, tn=512, name="mem_v_proj")

    ya = _moba_attention(proj, batch, seq).reshape(t, ATT_WIDTH)
    yg = _spatial_gating(proj, w_s, b_s, ln_v_g, ln_v_b)
    yc = _memory_attention(proj, mem_k, mem_v, batch, seq).reshape(t, MEM_WIDTH)

    out = _merge_project_norm(ya, yg, yc, gates, x2,
                              w_branch_attn.astype(_BF16), w_branch_sgu.astype(_BF16),
                              w_branch_mem.astype(_BF16), w_out.astype(_BF16), ln_g, ln_b)
    return out.reshape(batch, seq, d)
```

```python
import functools

import jax
import jax.numpy as jnp
from jax import lax
from jax.experimental import pallas as pl
from jax.experimental.pallas import tpu as pltpu

D_MODEL = 2048
DEPTH = 1
N_MEM = 256
ATT_HEAD_DIM = 128
ATT_WIDTH = D_MODEL // 2
ATT_HEADS = ATT_WIDTH // ATT_HEAD_DIM
MOBA_BLOCK = 256
MOBA_TOPK = 3
SGU_WIDTH = D_MODEL // 2
SGU_CHUNK = 128
SGU_GROUP_DIM = 128
SGU_GROUPS = SGU_WIDTH // SGU_GROUP_DIM
MEM_HEADS = 4
MEM_WIDTH = D_MODEL // 2
MEM_HEAD_DIM = MEM_WIDTH // MEM_HEADS
N_BRANCH = 3
DN_ALPHA = (2 * DEPTH) ** 0.25
LN_EPS = 1e-5

PROJ_TILE = 1024
GATE_TILE0 = 9
PROJ_WIDTH = GATE_TILE0 * PROJ_TILE

VMEM_LIMIT_BYTES = 56 * 1024 * 1024
IN_PROJ_TM = 2048
IN_PROJ_ROW_CHUNKS = 8

_NT = (((1,), (1,)), ((), ()))
_F32 = jnp.float32
_BF16 = jnp.bfloat16


def _sigmoid(x):
    return 1.0 / (1.0 + jnp.exp(-x))


def _silu(x):
    return x * _sigmoid(x)


def _gelu_tanh(x):
    c = 0.7978845608028654
    return x * (0.5 * (1.0 + jnp.tanh(c * (x + 0.044715 * (x * x * x)))))


def _project_kernel(x_ref, w_ref, o_ref, w_bf_ref, *, first_tile_scale, sigmoid, row_chunks):
    @pl.when(pl.program_id(1) == 0)
    def _cast_weight_tile():
        w_bf_ref[...] = w_ref[...].astype(_BF16)

    tc = x_ref.shape[0] // row_chunks
    for c in range(row_chunks):
        rows = slice(c * tc, (c + 1) * tc)
        acc = jnp.dot(x_ref[rows, :], w_bf_ref[...], preferred_element_type=_F32)
        if sigmoid:
            acc = _sigmoid(acc)
        if first_tile_scale is not None:
            acc = acc * jnp.where(pl.program_id(0) == 0, first_tile_scale, 1.0)
        o_ref[rows, :] = acc.astype(o_ref.dtype)


def _project(x, w, *, tm, tn, name, first_col_tile=0, n_col_tiles=None,
             first_tile_scale=None, sigmoid=False, row_chunks=1):
    m, k = x.shape
    if n_col_tiles is None:
        n_col_tiles = w.shape[1] // tn - first_col_tile
    assert m % tm == 0 and tm % row_chunks == 0 and (first_col_tile + n_col_tiles) * tn <= w.shape[1]
    return pl.pallas_call(
        functools.partial(_project_kernel, first_tile_scale=first_tile_scale, sigmoid=sigmoid,
                          row_chunks=row_chunks),
        grid=(n_col_tiles, m // tm),
        in_specs=[pl.BlockSpec((tm, k), lambda j, i: (i, 0)),
                  pl.BlockSpec((k, tn), lambda j, i: (0, first_col_tile + j))],
        out_specs=pl.BlockSpec((tm, tn), lambda j, i: (i, j)),
        out_shape=jax.ShapeDtypeStruct((m, n_col_tiles * tn), _BF16),
        scratch_shapes=[pltpu.VMEM((k, tn), _BF16)],
        compiler_params=pltpu.CompilerParams(
            dimension_semantics=("arbitrary", "arbitrary"), vmem_limit_bytes=VMEM_LIMIT_BYTES),
        name=name,
    )(x, w)


MOBA_HEADS_PER_STEP = 4
_ACC_ROWS = ATT_HEAD_DIM + 16
MOBA_Q_PRESCALE = ATT_HEAD_DIM ** -0.5 * 1.4426950408889634


def _moba_kernel(q_ref, k_ref, v_ref, z_ref, o_ref,
                 vt_ref, kmean_ref, neg_ref, pos_ref, qt_ref, s_ref, acc_ref, *, n_blocks, heads):
    qi = pl.program_id(2)
    blk = MOBA_BLOCK
    hd = ATT_HEAD_DIM

    @pl.when(qi == 0)
    def _per_head_setup():
        for g in range(heads):
            cols = slice(g * hd, (g + 1) * hd)
            for j in range(n_blocks):
                rows = slice(j * blk, (j + 1) * blk)
                vt_ref[g, j, :hd, :] = v_ref[rows, cols].astype(_F32).T.astype(_BF16)
                vt_ref[g, j, hd:, :] = jnp.ones((_ACC_ROWS - hd, blk), _BF16)
                kmean_ref[g, j:j + 1, :] = (
                    jnp.sum(k_ref[rows, cols].astype(_F32), axis=0, keepdims=True) * (1.0 / blk))

    head_cols = [slice(g * hd, (g + 1) * hd) for g in range(heads)]
    qs = [q_ref[:, c] for c in head_cols]
    for g in range(heads):
        qt_ref[g] = qs[g].astype(_F32).T.astype(_BF16)

    def issue_scores(j, slot):
        start = j * blk if isinstance(j, int) else pl.multiple_of(j * blk, blk)
        for g in range(heads):
            s_ref[slot, g] = jnp.dot(k_ref[pl.ds(start, blk), head_cols[g]], qt_ref[g],
                                     preferred_element_type=_F32)

    sel_scores = []
    for g in range(heads):
        km = kmean_ref[g]
        km_hi = km.astype(_BF16)
        km_lo = (km - km_hi.astype(_F32)).astype(_BF16)
        sel_scores.append(jnp.dot(km_hi, qt_ref[g], preferred_element_type=_F32)
                          + jnp.dot(km_lo, qt_ref[g], preferred_element_type=_F32))
    issue_scores(qi, 2)
    for g in range(heads):
        sb = sel_scores[g]
        row_id = lax.broadcasted_iota(jnp.int32, sb.shape, 0)
        rank = jnp.zeros(sb.shape, jnp.int32)
        for j in range(n_blocks):
            other = sb[j:j + 1, :]
            beats = (other > sb) | ((other == sb) & (j < row_id))
            rank = rank + jnp.where(beats & (j < qi), 1, 0)
        selected = (row_id < qi) & (rank < MOBA_TOPK)
        neg_ref[g] = jnp.where(selected, 0.0, -jnp.inf)
        pos_ref[g] = jnp.where(selected, -jnp.inf, jnp.inf)

    def accumulate(j, slot, ms, causal):
        probs, m_news = [], []
        for g in range(heads):
            s = s_ref[slot, g]
            if causal:
                key_pos = lax.broadcasted_iota(jnp.int32, s.shape, 0)
                q_pos = lax.broadcasted_iota(jnp.int32, s.shape, 1)
                s = jnp.where(key_pos <= q_pos, s, -jnp.inf)
                m_new = jnp.maximum(ms[g], jnp.max(s, axis=0, keepdims=True))
                m_sub = m_new
            else:
                m_new = jnp.maximum(ms[g], jnp.max(s, axis=0, keepdims=True) + neg_ref[g, pl.ds(j, 1), :])
                m_sub = jnp.maximum(m_new, pos_ref[g, pl.ds(j, 1), :])
            probs.append(jnp.exp2(s - m_sub).astype(_BF16))
            m_news.append(m_new)
        for g in range(heads):
            alpha = jnp.exp2(jnp.where(m_news[g] == -jnp.inf, 0.0, ms[g] - m_news[g]))
            acc_ref[g] = alpha * acc_ref[g] + jnp.dot(vt_ref[g, j], probs[g], preferred_element_type=_F32)
        return m_news

    def past_block_pair(t, ms):
        issue_scores(2 * t + 1, 1)
        ms = accumulate(2 * t, 0, ms, causal=False)
        issue_scores(jnp.minimum(2 * t + 2, n_blocks - 1), 0)
        return accumulate(2 * t + 1, 1, ms, causal=False)

    for g in range(heads):
        acc_ref[g] = jnp.zeros((_ACC_ROWS, blk), _F32)
    issue_scores(0, 0)
    ms = accumulate(qi, 2, [jnp.full((1, blk), -jnp.inf, _F32)] * heads, causal=True)
    lax.fori_loop(0, (qi + 1) // 2, past_block_pair, ms)

    for g in range(heads):
        attn = (acc_ref[g, :hd, :] / acc_ref[g, hd:hd + 1, :]).T
        o_ref[:, head_cols[g]] = (attn * _silu(z_ref[:, head_cols[g]].astype(_F32))).astype(o_ref.dtype)


def _moba_attention(proj, batch, seq, *, heads=MOBA_HEADS_PER_STEP):
    n_blocks = seq // MOBA_BLOCK
    proj3 = proj.reshape(batch, seq, PROJ_WIDTH)
    width = heads * ATT_HEAD_DIM
    cols = ATT_WIDTH // width
    return pl.pallas_call(
        functools.partial(_moba_kernel, n_blocks=n_blocks, heads=heads),
        grid=(batch, cols, n_blocks),
        in_specs=[pl.BlockSpec((None, MOBA_BLOCK, width), lambda b, h, i: (b, i, h)),
                  pl.BlockSpec((None, seq, width), lambda b, h, i: (b, 0, cols + h)),
                  pl.BlockSpec((None, seq, width), lambda b, h, i: (b, 0, 2 * cols + h)),
                  pl.BlockSpec((None, MOBA_BLOCK, width), lambda b, h, i: (b, i, 3 * cols + h))],
        out_specs=pl.BlockSpec((None, MOBA_BLOCK, width), lambda b, h, i: (b, i, h)),
        out_shape=jax.ShapeDtypeStruct((batch, seq, ATT_WIDTH), _BF16),
        scratch_shapes=[pltpu.VMEM((heads, n_blocks, _ACC_ROWS, MOBA_BLOCK), _BF16),
                        pltpu.VMEM((heads, n_blocks, ATT_HEAD_DIM), _F32),
                        pltpu.VMEM((heads, n_blocks, MOBA_BLOCK), _F32),
                        pltpu.VMEM((heads, n_blocks, MOBA_BLOCK), _F32),
                        pltpu.VMEM((heads, ATT_HEAD_DIM, MOBA_BLOCK), _BF16),
                        pltpu.VMEM((3, heads, MOBA_BLOCK, MOBA_BLOCK), _F32),
                        pltpu.VMEM((heads, _ACC_ROWS, MOBA_BLOCK), _F32)],
        compiler_params=pltpu.CompilerParams(
            dimension_semantics=("arbitrary", "arbitrary", "arbitrary"), vmem_limit_bytes=VMEM_LIMIT_BYTES),
        name="moba_attention",
    )(proj3, proj3, proj3, proj3)


def _sgu_kernel(u_ref, v_ref, z_ref, w_ref, bt_ref, g_ref, b_ref, o_ref, *, tm):
    u = _gelu_tanh(u_ref[...].astype(_F32))
    v = _gelu_tanh(v_ref[...].astype(_F32))
    mu = jnp.mean(v, axis=-1, keepdims=True)
    d = v - mu
    var = jnp.mean(d * d, axis=-1, keepdims=True)
    vn = (d * lax.rsqrt(var + LN_EPS) * g_ref[...] + b_ref[...]).astype(_BF16)
    gate = u * _silu(z_ref[...].astype(_F32))
    t_pos = lax.broadcasted_iota(jnp.int32, (SGU_CHUNK, SGU_CHUNK), 0)
    s_pos = lax.broadcasted_iota(jnp.int32, (SGU_CHUNK, SGU_CHUNK), 1)
    for g in range(SGU_GROUPS):
        w_causal = jnp.where(s_pos <= t_pos, w_ref[g], 0.0).astype(_BF16)
        bias = bt_ref[:, g:g + 1]
        cols = slice(g * SGU_GROUP_DIM, (g + 1) * SGU_GROUP_DIM)
        for c in range(tm // SGU_CHUNK):
            rows = slice(c * SGU_CHUNK, (c + 1) * SGU_CHUNK)
            mixed = jnp.dot(w_causal, vn[rows, cols], preferred_element_type=_F32) + bias
            o_ref[rows, cols] = (gate[rows, cols] * mixed).astype(o_ref.dtype)


def _spatial_gating(proj, w_s, b_s, ln_v_g, ln_v_b, *, tm=512):
    t = proj.shape[0]
    blk = lambda c: pl.BlockSpec((tm, SGU_WIDTH), lambda i: (i, c))
    whole = lambda shape: pl.BlockSpec(shape, lambda i: (0,) * len(shape))
    return pl.pallas_call(
        functools.partial(_sgu_kernel, tm=tm),
        grid=(t // tm,),
        in_specs=[blk(4), blk(5), blk(6),
                  whole((SGU_GROUPS, SGU_CHUNK, SGU_CHUNK)), whole((SGU_CHUNK, SGU_GROUPS)),
                  whole((1, SGU_WIDTH)), whole((1, SGU_WIDTH))],
        out_specs=pl.BlockSpec((tm, SGU_WIDTH), lambda i: (i, 0)),
        out_shape=jax.ShapeDtypeStruct((t, SGU_WIDTH), _BF16),
        compiler_params=pltpu.CompilerParams(
            dimension_semantics=("arbitrary",), vmem_limit_bytes=VMEM_LIMIT_BYTES),
        name="spatial_gating",
    )(proj, proj, proj, w_s, b_s.T, ln_v_g.reshape(1, -1), ln_v_b.reshape(1, -1))


def _mem_attn_kernel(q_ref, z_ref, k_ref, v_ref, o_ref):
    scale = MEM_HEAD_DIM ** -0.5
    for h in range(MEM_HEADS):
        cols = slice(h * MEM_HEAD_DIM, (h + 1) * MEM_HEAD_DIM)
        s = lax.dot_general(q_ref[:, cols], k_ref[:, cols], _NT, preferred_element_type=_F32) * scale
        p = jnp.exp(s - jnp.max(s, axis=-1, keepdims=True))
        l = jnp.sum(p, axis=-1, keepdims=True)
        o = jnp.dot(p.astype(_BF16), v_ref[:, cols], preferred_element_type=_F32) / l
        o_ref[:, cols] = (o * _silu(z_ref[:, cols].astype(_F32))).astype(o_ref.dtype)


def _memory_attention(proj, mem_k, mem_v, batch, seq, *, tm=512):
    proj3 = proj.reshape(batch, seq, PROJ_WIDTH)
    k3 = mem_k.reshape(batch, N_MEM, MEM_WIDTH)
    v3 = mem_v.reshape(batch, N_MEM, MEM_WIDTH)
    return pl.pallas_call(
        _mem_attn_kernel,
        grid=(batch, seq // tm),
        in_specs=[pl.BlockSpec((None, tm, MEM_WIDTH), lambda b, i: (b, i, 7)),
                  pl.BlockSpec((None, tm, MEM_WIDTH), lambda b, i: (b, i, 8)),
                  pl.BlockSpec((None, N_MEM, MEM_WIDTH), lambda b, i: (b, 0, 0)),
                  pl.BlockSpec((None, N_MEM, MEM_WIDTH), lambda b, i: (b, 0, 0))],
        out_specs=pl.BlockSpec((None, tm, MEM_WIDTH), lambda b, i: (b, i, 0)),
        out_shape=jax.ShapeDtypeStruct((batch, seq, MEM_WIDTH), _BF16),
        compiler_params=pltpu.CompilerParams(
            dimension_semantics=("arbitrary", "arbitrary"), vmem_limit_bytes=VMEM_LIMIT_BYTES),
        name="memory_attention",
    )(proj3, proj3, k3, v3)


MERGE_TM = 512
OUT_NORM_TM = 512
MERGE_ROW_CHUNK = 256


def _branch_merge_kernel(ya_ref, yg_ref, yc_ref, *rest):
    n_gate = N_BRANCH * D_MODEL // PROJ_TILE
    gate_refs = rest[:n_gate]
    wa_ref, wg_ref, wc_ref, o_ref = rest[n_gate:]
    per_branch = D_MODEL // PROJ_TILE
    pairs = ((ya_ref, wa_ref), (yg_ref, wg_ref), (yc_ref, wc_ref))
    tiles = [(slice(r * MERGE_ROW_CHUNK, (r + 1) * MERGE_ROW_CHUNK), c)
             for r in range(o_ref.shape[0] // MERGE_ROW_CHUNK) for c in range(per_branch)]

    def branch_dots(rows, c):
        cols = slice(c * PROJ_TILE, (c + 1) * PROJ_TILE)
        return [jnp.dot(y_ref[rows, :], w_ref[:, cols], preferred_element_type=_F32) for y_ref, w_ref in pairs]

    dots = branch_dots(*tiles[0])
    for k, (rows, c) in enumerate(tiles):
        next_dots = branch_dots(*tiles[k + 1]) if k + 1 < len(tiles) else None
        merged = None
        for br in range(N_BRANCH):
            term = gate_refs[br * per_branch + c][rows, :].astype(_F32) * dots[br]
            merged = term if merged is None else merged + term
        o_ref[rows, c * PROJ_TILE:(c + 1) * PROJ_TILE] = merged.astype(o_ref.dtype)
        dots = next_dots


def _out_norm_kernel(m_ref, x_ref, wo_ref, lng_ref, lnb_ref, o_ref):
    chunks = [slice(r * MERGE_ROW_CHUNK, (r + 1) * MERGE_ROW_CHUNK)
              for r in range(o_ref.shape[0] // MERGE_ROW_CHUNK)]

    def out_dot(rows):
        return jnp.dot(m_ref[rows, :], wo_ref[...], preferred_element_type=_F32)

    y = out_dot(chunks[0])
    for k, rows in enumerate(chunks):
        y_next = out_dot(chunks[k + 1]) if k + 1 < len(chunks) else None
        h = DN_ALPHA * x_ref[rows, :] + y
        mu = jnp.mean(h, axis=-1, keepdims=True)
        d = h - mu
        var = jnp.mean(d * d, axis=-1, keepdims=True)
        o_ref[rows, :] = d * lax.rsqrt(var + LN_EPS) * lng_ref[...] + lnb_ref[...]
        y = y_next


def _merge_project_norm(ya, yg, yc, gates, x2, wa, wg, wc, wo, ln_g, ln_b, *, tm=MERGE_TM):
    t = x2.shape[0]
    assert t % tm == 0 and tm % MERGE_ROW_CHUNK == 0 and t % OUT_NORM_TM == 0 and OUT_NORM_TM % MERGE_ROW_CHUNK == 0
    n_gate = N_BRANCH * D_MODEL // PROJ_TILE
    branch = pl.BlockSpec((tm, ATT_WIDTH), lambda i: (i, 0))
    gate = lambda c: pl.BlockSpec((tm, PROJ_TILE), lambda i: (i, c))
    rows = pl.BlockSpec((tm, D_MODEL), lambda i: (i, 0))
    resident = lambda shape: pl.BlockSpec(shape, lambda i: (0, 0), pipeline_mode=pl.Buffered(1))
    params = pltpu.CompilerParams(dimension_semantics=("arbitrary",), vmem_limit_bytes=VMEM_LIMIT_BYTES)
    merged = pl.pallas_call(
        _branch_merge_kernel,
        grid=(t // tm,),
        in_specs=[branch, branch, branch, *[gate(c) for c in range(n_gate)],
                  resident((ATT_WIDTH, D_MODEL)), resident((SGU_WIDTH, D_MODEL)), resident((MEM_WIDTH, D_MODEL))],
        out_specs=rows,
        out_shape=jax.ShapeDtypeStruct((t, D_MODEL), _BF16),
        compiler_params=params,
        name="branch_merge",
    )(ya, yg, yc, *([gates] * n_gate), wa, wg, wc)
    out_rows = pl.BlockSpec((OUT_NORM_TM, D_MODEL), lambda i: (i, 0))
    return pl.pallas_call(
        _out_norm_kernel,
        grid=(t // OUT_NORM_TM,),
        in_specs=[out_rows, out_rows,
                  resident((D_MODEL, D_MODEL)), resident((1, D_MODEL)), resident((1, D_MODEL))],
        out_specs=out_rows,
        out_shape=jax.ShapeDtypeStruct((t, D_MODEL), _F32),
        compiler_params=params,
        name="out_proj_norm",
    )(merged, x2, wo, ln_g.reshape(1, -1), ln_b.reshape(1, -1))


def kernel(x, mem, w_in, w_mem_k, w_mem_v, w_s, b_s, ln_v_g, ln_v_b,
           w_branch_attn, w_branch_sgu, w_branch_mem, w_out, ln_g, ln_b):
    batch, seq, d = x.shape
    assert d == D_MODEL and seq % MOBA_BLOCK == 0 and mem.shape[1] == N_MEM
    t = batch * seq
    x2 = x.reshape(t, d)
    x_bf = x2.astype(_BF16)

    assert w_in.shape[1] == PROJ_WIDTH + N_BRANCH * D_MODEL
    proj = _project(x_bf, w_in, tm=IN_PROJ_TM, tn=PROJ_TILE, name="in_proj_branches", n_col_tiles=GATE_TILE0,
                    first_tile_scale=MOBA_Q_PRESCALE, row_chunks=IN_PROJ_ROW_CHUNKS)
    gates = _project(x_bf, w_in, tm=IN_PROJ_TM, tn=PROJ_TILE, name="in_proj_gates", first_col_tile=GATE_TILE0,
                     sigmoid=True, row_chunks=IN_PROJ_ROW_CHUNKS)
    mem_bf = mem.reshape(batch * N_MEM, d).astype(_BF16)
    mem_k = _project(mem_bf, w_mem_k, tm=batch * N_MEM, tn=512, name="mem_k_proj")
    mem_v = _project(mem_bf, w_mem_v, tm=batch * N_MEM, tn=512, name="mem_v_proj")

    ya = _moba_attention(proj, batch, seq).reshape(t, ATT_WIDTH)
    yg = _spatial_gating(proj, w_s, b_s, ln_v_g, ln_v_b)
    yc = _memory_attention(proj, mem_k, mem_v, batch, seq).reshape(t, MEM_WIDTH)

    out = _merge_project_norm(ya, yg, yc, gates, x2,
                              w_branch_attn.astype(_BF16), w_branch_sgu.astype(_BF16),
                              w_branch_mem.astype(_BF16), w_out.astype(_BF16), ln_g, ln_b)
    return out.reshape(batch, seq, d)
```

```python
import functools

import jax
import jax.numpy as jnp
from jax import lax
from jax.experimental import pallas as pl
from jax.experimental.pallas import tpu as pltpu

D_MODEL = 2048
DEPTH = 1
N_MEM = 256
ATT_HEAD_DIM = 128
ATT_WIDTH = D_MODEL // 2
ATT_HEADS = ATT_WIDTH // ATT_HEAD_DIM
MOBA_BLOCK = 256
MOBA_TOPK = 3
SGU_WIDTH = D_MODEL // 2
SGU_CHUNK = 128
SGU_GROUP_DIM = 128
SGU_GROUPS = SGU_WIDTH // SGU_GROUP_DIM
MEM_HEADS = 4
MEM_WIDTH = D_MODEL // 2
MEM_HEAD_DIM = MEM_WIDTH // MEM_HEADS
N_BRANCH = 3
DN_ALPHA = (2 * DEPTH) ** 0.25
LN_EPS = 1e-5

PROJ_TILE = 1024
PROJ_TILE0 = 1
GATE_TILE0 = 9
PROJ_WIDTH = (GATE_TILE0 - PROJ_TILE0) * PROJ_TILE
K_A, V_A, Z_A, U_G, V_G, Z_G, Q_C, Z_C = range(8)

VMEM_LIMIT_BYTES = 56 * 1024 * 1024
IN_PROJ_TM = 2048
IN_PROJ_ROW_CHUNKS = 8

_NT = (((1,), (1,)), ((), ()))
_F32 = jnp.float32
_BF16 = jnp.bfloat16


def _sigmoid(x):
    return 1.0 / (1.0 + jnp.exp(-x))


def _silu(x):
    return x * _sigmoid(x)


def _gelu_tanh(x):
    c = 0.7978845608028654
    return x * (0.5 * (1.0 + jnp.tanh(c * (x + 0.044715 * (x * x * x)))))


def _project_kernel(x_ref, w_ref, o_ref, *rest, scale, sigmoid, row_chunks, emit_x_bf16):
    x_bf_ref = rest[0] if emit_x_bf16 else None
    w_bf_ref = rest[-1]

    @pl.when(pl.program_id(1) == 0)
    def _cast_weight_tile():
        w_bf_ref[...] = w_ref[...].astype(_BF16)

    tc = x_ref.shape[0] // row_chunks
    for c in range(row_chunks):
        rows = slice(c * tc, (c + 1) * tc)
        xc = x_ref[rows, :]
        if emit_x_bf16:
            xc = xc.astype(_BF16)
            x_bf_ref[rows, :] = xc
        acc = jnp.dot(xc, w_bf_ref[...], preferred_element_type=_F32)
        if sigmoid:
            acc = _sigmoid(acc)
        if scale is not None:
            acc = acc * scale
        o_ref[rows, :] = acc.astype(o_ref.dtype)


def _project(x, w, *, tm, tn, name, first_col_tile=0, n_col_tiles=None,
             scale=None, sigmoid=False, row_chunks=1, emit_x_bf16=False):
    m, k = x.shape
    if n_col_tiles is None:
        n_col_tiles = w.shape[1] // tn - first_col_tile
    assert m % tm == 0 and tm % row_chunks == 0 and (first_col_tile + n_col_tiles) * tn <= w.shape[1]
    assert not emit_x_bf16 or n_col_tiles == 1
    out_specs = [pl.BlockSpec((tm, tn), lambda j, i: (i, j))]
    out_shape = [jax.ShapeDtypeStruct((m, n_col_tiles * tn), _BF16)]
    if emit_x_bf16:
        out_specs.append(pl.BlockSpec((tm, k), lambda j, i: (i, 0)))
        out_shape.append(jax.ShapeDtypeStruct((m, k), _BF16))
    outs = pl.pallas_call(
        functools.partial(_project_kernel, scale=scale, sigmoid=sigmoid, row_chunks=row_chunks,
                          emit_x_bf16=emit_x_bf16),
        grid=(n_col_tiles, m // tm),
        in_specs=[pl.BlockSpec((tm, k), lambda j, i: (i, 0)),
                  pl.BlockSpec((k, tn), lambda j, i: (0, first_col_tile + j))],
        out_specs=out_specs,
        out_shape=out_shape,
        scratch_shapes=[pltpu.VMEM((k, tn), _BF16)],
        compiler_params=pltpu.CompilerParams(
            dimension_semantics=("arbitrary", "arbitrary"), vmem_limit_bytes=VMEM_LIMIT_BYTES),
        name=name,
    )(x, w)
    return outs if emit_x_bf16 else outs[0]


MOBA_HEADS_PER_STEP = 4
_ACC_ROWS = ATT_HEAD_DIM + 16
MOBA_Q_PRESCALE = ATT_HEAD_DIM ** -0.5 * 1.4426950408889634


def _moba_kernel(q_ref, k_ref, v_ref, z_ref, o_ref,
                 vt_ref, kmean_ref, neg_ref, pos_ref, qt_ref, s_ref, acc_ref, *, n_blocks, heads):
    qi = pl.program_id(2)
    blk = MOBA_BLOCK
    hd = ATT_HEAD_DIM

    @pl.when(qi == 0)
    def _per_head_setup():
        for g in range(heads):
            cols = slice(g * hd, (g + 1) * hd)
            for j in range(n_blocks):
                rows = slice(j * blk, (j + 1) * blk)
                vt_ref[g, j, :hd, :] = v_ref[rows, cols].astype(_F32).T.astype(_BF16)
                vt_ref[g, j, hd:, :] = jnp.ones((_ACC_ROWS - hd, blk), _BF16)
                kmean_ref[g, j:j + 1, :] = (
                    jnp.sum(k_ref[rows, cols].astype(_F32), axis=0, keepdims=True) * (1.0 / blk))

    head_cols = [slice(g * hd, (g + 1) * hd) for g in range(heads)]
    qs = [q_ref[:, c] for c in head_cols]
    for g in range(heads):
        qt_ref[g] = qs[g].astype(_F32).T.astype(_BF16)

    def issue_scores(j, slot):
        start = j * blk if isinstance(j, int) else pl.multiple_of(j * blk, blk)
        for g in range(heads):
            s_ref[slot, g] = jnp.dot(k_ref[pl.ds(start, blk), head_cols[g]], qt_ref[g],
                                     preferred_element_type=_F32)

    sel_scores = []
    for g in range(heads):
        km = kmean_ref[g]
        km_hi = km.astype(_BF16)
        km_lo = (km - km_hi.astype(_F32)).astype(_BF16)
        sel_scores.append(jnp.dot(km_hi, qt_ref[g], preferred_element_type=_F32)
                          + jnp.dot(km_lo, qt_ref[g], preferred_element_type=_F32))
    issue_scores(qi, 2)
    row_id = lax.broadcasted_iota(jnp.int32, (n_blocks, blk), 0)
    past = row_id < qi
    later_rows = [jnp.where(row_id > j, 1.0, 0.0) for j in range(n_blocks)]
    for g in range(heads):
        sb = jnp.where(past, sel_scores[g], -jnp.inf)
        rank = jnp.zeros(sb.shape, _F32)
        for j in range(n_blocks):
            other = sb[j:j + 1, :]
            rank = rank + jnp.where(other > sb, 1.0, 0.0) + jnp.where(other == sb, later_rows[j], 0.0)
        selected = past & (rank < MOBA_TOPK)
        neg_ref[g] = jnp.where(selected, 0.0, -jnp.inf)
        pos_ref[g] = jnp.where(selected, -jnp.inf, jnp.inf)

    def accumulate(j, slot, ms, causal):
        probs, m_news = [], []
        for g in range(heads):
            s = s_ref[slot, g]
            if causal:
                key_pos = lax.broadcasted_iota(jnp.int32, s.shape, 0)
                q_pos = lax.broadcasted_iota(jnp.int32, s.shape, 1)
                s = jnp.where(key_pos <= q_pos, s, -jnp.inf)
                m_new = jnp.maximum(ms[g], jnp.max(s, axis=0, keepdims=True))
                m_sub = m_new
            else:
                m_new = jnp.maximum(ms[g], jnp.max(s, axis=0, keepdims=True) + neg_ref[g, pl.ds(j, 1), :])
                m_sub = jnp.maximum(m_new, pos_ref[g, pl.ds(j, 1), :])
            probs.append(jnp.exp2(s - m_sub).astype(_BF16))
            m_news.append(m_new)
        for g in range(heads):
            alpha = jnp.exp2(jnp.where(m_news[g] == -jnp.inf, 0.0, ms[g] - m_news[g]))
            acc_ref[g] = alpha * acc_ref[g] + jnp.dot(vt_ref[g, j], probs[g], preferred_element_type=_F32)
        return m_news

    def past_block_pair(t, ms):
        issue_scores(2 * t + 1, 1)
        ms = accumulate(2 * t, 0, ms, causal=False)
        issue_scores(jnp.minimum(2 * t + 2, n_blocks - 1), 0)
        return accumulate(2 * t + 1, 1, ms, causal=False)

    for g in range(heads):
        acc_ref[g] = jnp.zeros((_ACC_ROWS, blk), _F32)
    issue_scores(0, 0)
    ms = accumulate(qi, 2, [jnp.full((1, blk), -jnp.inf, _F32)] * heads, causal=True)
    lax.fori_loop(0, (qi + 1) // 2, past_block_pair, ms)

    for g in range(heads):
        attn = (acc_ref[g, :hd, :] / acc_ref[g, hd:hd + 1, :]).T
        o_ref[:, head_cols[g]] = (attn * _silu(z_ref[:, head_cols[g]].astype(_F32))).astype(o_ref.dtype)


def _moba_attention(q, proj, batch, seq, *, heads=MOBA_HEADS_PER_STEP):
    n_blocks = seq // MOBA_BLOCK
    q3 = q.reshape(batch, seq, ATT_WIDTH)
    proj3 = proj.reshape(batch, seq, PROJ_WIDTH)
    width = heads * ATT_HEAD_DIM
    cols = PROJ_TILE // width
    return pl.pallas_call(
        functools.partial(_moba_kernel, n_blocks=n_blocks, heads=heads),
        grid=(batch, ATT_WIDTH // width, n_blocks),
        in_specs=[pl.BlockSpec((None, MOBA_BLOCK, width), lambda b, h, i: (b, i, h)),
                  pl.BlockSpec((None, seq, width), lambda b, h, i: (b, 0, K_A * cols + h)),
                  pl.BlockSpec((None, seq, width), lambda b, h, i: (b, 0, V_A * cols + h)),
                  pl.BlockSpec((None, MOBA_BLOCK, width), lambda b, h, i: (b, i, Z_A * cols + h))],
        out_specs=pl.BlockSpec((None, MOBA_BLOCK, width), lambda b, h, i: (b, i, h)),
        out_shape=jax.ShapeDtypeStruct((batch, seq, ATT_WIDTH), _BF16),
        scratch_shapes=[pltpu.VMEM((heads, n_blocks, _ACC_ROWS, MOBA_BLOCK), _BF16),
                        pltpu.VMEM((heads, n_blocks, ATT_HEAD_DIM), _F32),
                        pltpu.VMEM((heads, n_blocks, MOBA_BLOCK), _F32),
                        pltpu.VMEM((heads, n_blocks, MOBA_BLOCK), _F32),
                        pltpu.VMEM((heads, ATT_HEAD_DIM, MOBA_BLOCK), _BF16),
                        pltpu.VMEM((3, heads, MOBA_BLOCK, MOBA_BLOCK), _F32),
                        pltpu.VMEM((heads, _ACC_ROWS, MOBA_BLOCK), _F32)],
        compiler_params=pltpu.CompilerParams(
            dimension_semantics=("arbitrary", "arbitrary", "arbitrary"), vmem_limit_bytes=VMEM_LIMIT_BYTES),
        name="moba_attention",
    )(q3, proj3, proj3, proj3)


def _sgu_kernel(u_ref, v_ref, z_ref, w_ref, bt_ref, g_ref, b_ref, o_ref, *, tm):
    u = _gelu_tanh(u_ref[...].astype(_F32))
    v = _gelu_tanh(v_ref[...].astype(_F32))
    mu = jnp.mean(v, axis=-1, keepdims=True)
    d = v - mu
    var = jnp.mean(d * d, axis=-1, keepdims=True)
    vn = (d * lax.rsqrt(var + LN_EPS) * g_ref[...] + b_ref[...]).astype(_BF16)
    gate = u * _silu(z_ref[...].astype(_F32))
    t_pos = lax.broadcasted_iota(jnp.int32, (SGU_CHUNK, SGU_CHUNK), 0)
    s_pos = lax.broadcasted_iota(jnp.int32, (SGU_CHUNK, SGU_CHUNK), 1)
    for g in range(SGU_GROUPS):
        w_causal = jnp.where(s_pos <= t_pos, w_ref[g], 0.0).astype(_BF16)
        bias = bt_ref[:, g:g + 1]
        cols = slice(g * SGU_GROUP_DIM, (g + 1) * SGU_GROUP_DIM)
        for c in range(tm // SGU_CHUNK):
            rows = slice(c * SGU_CHUNK, (c + 1) * SGU_CHUNK)
            mixed = jnp.dot(w_causal, vn[rows, cols], preferred_element_type=_F32) + bias
            o_ref[rows, cols] = (gate[rows, cols] * mixed).astype(o_ref.dtype)


def _spatial_gating(proj, w_s, b_s, ln_v_g, ln_v_b, *, tm=512):
    t = proj.shape[0]
    blk = lambda c: pl.BlockSpec((tm, SGU_WIDTH), lambda i: (i, c))
    whole = lambda shape: pl.BlockSpec(shape, lambda i: (0,) * len(shape))
    return pl.pallas_call(
        functools.partial(_sgu_kernel, tm=tm),
        grid=(t // tm,),
        in_specs=[blk(U_G), blk(V_G), blk(Z_G),
                  whole((SGU_GROUPS, SGU_CHUNK, SGU_CHUNK)), whole((SGU_CHUNK, SGU_GROUPS)),
                  whole((1, SGU_WIDTH)), whole((1, SGU_WIDTH))],
        out_specs=pl.BlockSpec((tm, SGU_WIDTH), lambda i: (i, 0)),
        out_shape=jax.ShapeDtypeStruct((t, SGU_WIDTH), _BF16),
        compiler_params=pltpu.CompilerParams(
            dimension_semantics=("arbitrary",), vmem_limit_bytes=VMEM_LIMIT_BYTES),
        name="spatial_gating",
    )(proj, proj, proj, w_s, b_s.T, ln_v_g.reshape(1, -1), ln_v_b.reshape(1, -1))


def _mem_attn_kernel(q_ref, z_ref, k_ref, v_ref, o_ref):
    scale = MEM_HEAD_DIM ** -0.5
    for h in range(MEM_HEADS):
        cols = slice(h * MEM_HEAD_DIM, (h + 1) * MEM_HEAD_DIM)
        s = lax.dot_general(q_ref[:, cols], k_ref[:, cols], _NT, preferred_element_type=_F32) * scale
        p = jnp.exp(s - jnp.max(s, axis=-1, keepdims=True))
        l = jnp.sum(p, axis=-1, keepdims=True)
        o = jnp.dot(p.astype(_BF16), v_ref[:, cols], preferred_element_type=_F32) / l
        o_ref[:, cols] = (o * _silu(z_ref[:, cols].astype(_F32))).astype(o_ref.dtype)


def _memory_attention(proj, mem_k, mem_v, batch, seq, *, tm=512):
    proj3 = proj.reshape(batch, seq, PROJ_WIDTH)
    k3 = mem_k.reshape(batch, N_MEM, MEM_WIDTH)
    v3 = mem_v.reshape(batch, N_MEM, MEM_WIDTH)
    return pl.pallas_call(
        _mem_attn_kernel,
        grid=(batch, seq // tm),
        in_specs=[pl.BlockSpec((None, tm, MEM_WIDTH), lambda b, i: (b, i, Q_C)),
                  pl.BlockSpec((None, tm, MEM_WIDTH), lambda b, i: (b, i, Z_C)),
                  pl.BlockSpec((None, N_MEM, MEM_WIDTH), lambda b, i: (b, 0, 0)),
                  pl.BlockSpec((None, N_MEM, MEM_WIDTH), lambda b, i: (b, 0, 0))],
        out_specs=pl.BlockSpec((None, tm, MEM_WIDTH), lambda b, i: (b, i, 0)),
        out_shape=jax.ShapeDtypeStruct((batch, seq, MEM_WIDTH), _BF16),
        compiler_params=pltpu.CompilerParams(
            dimension_semantics=("arbitrary", "arbitrary"), vmem_limit_bytes=VMEM_LIMIT_BYTES),
        name="memory_attention",
    )(proj3, proj3, k3, v3)


MERGE_TM = 512
OUT_NORM_TM = 512
MERGE_ROW_CHUNK = 256


def _branch_merge_kernel(ya_ref, yg_ref, yc_ref, *rest):
    n_gate = N_BRANCH * D_MODEL // PROJ_TILE
    gate_refs = rest[:n_gate]
    wa_ref, wg_ref, wc_ref, o_ref = rest[n_gate:]
    per_branch = D_MODEL // PROJ_TILE
    pairs = ((ya_ref, wa_ref), (yg_ref, wg_ref), (yc_ref, wc_ref))
    tiles = [(slice(r * MERGE_ROW_CHUNK, (r + 1) * MERGE_ROW_CHUNK), c)
             for r in range(o_ref.shape[0] // MERGE_ROW_CHUNK) for c in range(per_branch)]

    def branch_dots(rows, c):
        cols = slice(c * PROJ_TILE, (c + 1) * PROJ_TILE)
        return [jnp.dot(y_ref[rows, :], w_ref[:, cols], preferred_element_type=_F32) for y_ref, w_ref in pairs]

    dots = branch_dots(*tiles[0])
    for k, (rows, c) in enumerate(tiles):
        next_dots = branch_dots(*tiles[k + 1]) if k + 1 < len(tiles) else None
        merged = None
        for br in range(N_BRANCH):
            term = gate_refs[br * per_branch + c][rows, :].astype(_F32) * dots[br]
            merged = term if merged is None else merged + term
        o_ref[rows, c * PROJ_TILE:(c + 1) * PROJ_TILE] = merged.astype(o_ref.dtype)
        dots = next_dots


def _out_norm_kernel(m_ref, x_ref, wo_ref, lng_ref, lnb_ref, o_ref):
    chunks = [slice(r * MERGE_ROW_CHUNK, (r + 1) * MERGE_ROW_CHUNK)
              for r in range(o_ref.shape[0] // MERGE_ROW_CHUNK)]

    def out_dot(rows):
        return jnp.dot(m_ref[rows, :], wo_ref[...], preferred_element_type=_F32)

    y = out_dot(chunks[0])
    for k, rows in enumerate(chunks):
        y_next = out_dot(chunks[k + 1]) if k + 1 < len(chunks) else None
        h = DN_ALPHA * x_ref[rows, :] + y
        mu = jnp.mean(h, axis=-1, keepdims=True)
        d = h - mu
        var = jnp.mean(d * d, axis=-1, keepdims=True)
        o_ref[rows, :] = d * lax.rsqrt(var + LN_EPS) * lng_ref[...] + lnb_ref[...]
        y = y_next


def _merge_project_norm(ya, yg, yc, gates, x2, wa, wg, wc, wo, ln_g, ln_b, *, tm=MERGE_TM):
    t = x2.shape[0]
    assert t % tm == 0 and tm % MERGE_ROW_CHUNK == 0 and t % OUT_NORM_TM == 0 and OUT_NORM_TM % MERGE_ROW_CHUNK == 0
    n_gate = N_BRANCH * D_MODEL // PROJ_TILE
    branch = pl.BlockSpec((tm, ATT_WIDTH), lambda i: (i, 0))
    gate = lambda c: pl.BlockSpec((tm, PROJ_TILE), lambda i: (i, c))
    rows = pl.BlockSpec((tm, D_MODEL), lambda i: (i, 0))
    resident = lambda shape: pl.BlockSpec(shape, lambda i: (0, 0), pipeline_mode=pl.Buffered(1))
    params = pltpu.CompilerParams(dimension_semantics=("arbitrary",), vmem_limit_bytes=VMEM_LIMIT_BYTES)
    merged = pl.pallas_call(
        _branch_merge_kernel,
        grid=(t // tm,),
        in_specs=[branch, branch, branch, *[gate(c) for c in range(n_gate)],
                  resident((ATT_WIDTH, D_MODEL)), resident((SGU_WIDTH, D_MODEL)), resident((MEM_WIDTH, D_MODEL))],
        out_specs=rows,
        out_shape=jax.ShapeDtypeStruct((t, D_MODEL), _BF16),
        compiler_params=params,
        name="branch_merge",
    )(ya, yg, yc, *([gates] * n_gate), wa, wg, wc)
    out_rows = pl.BlockSpec((OUT_NORM_TM, D_MODEL), lambda i: (i, 0))
    return pl.pallas_call(
        _out_norm_kernel,
        grid=(t // OUT_NORM_TM,),
        in_specs=[out_rows, out_rows,
                  resident((D_MODEL, D_MODEL)), resident((1, D_MODEL)), resident((1, D_MODEL))],
        out_specs=out_rows,
        out_shape=jax.ShapeDtypeStruct((t, D_MODEL), _F32),
        compiler_params=params,
        name="out_proj_norm",
    )(merged, x2, wo, ln_g.reshape(1, -1), ln_b.reshape(1, -1))


def kernel(x, mem, w_in, w_mem_k, w_mem_v, w_s, b_s, ln_v_g, ln_v_b,
           w_branch_attn, w_branch_sgu, w_branch_mem, w_out, ln_g, ln_b):
    batch, seq, d = x.shape
    assert d == D_MODEL and seq % MOBA_BLOCK == 0 and mem.shape[1] == N_MEM
    t = batch * seq
    x2 = x.reshape(t, d)

    assert w_in.shape[1] == GATE_TILE0 * PROJ_TILE + N_BRANCH * D_MODEL and ATT_WIDTH == PROJ_TILE
    q_a, x_bf = _project(x2, w_in, tm=1024, tn=PROJ_TILE, name="in_proj_q", n_col_tiles=1,
                         scale=MOBA_Q_PRESCALE, row_chunks=4, emit_x_bf16=True)
    proj = _project(x_bf, w_in, tm=IN_PROJ_TM, tn=PROJ_TILE, name="in_proj_branches", first_col_tile=PROJ_TILE0,
                    n_col_tiles=GATE_TILE0 - PROJ_TILE0, row_chunks=IN_PROJ_ROW_CHUNKS)
    gates = _project(x_bf, w_in, tm=IN_PROJ_TM, tn=PROJ_TILE, name="in_proj_gates", first_col_tile=GATE_TILE0,
                     sigmoid=True, row_chunks=IN_PROJ_ROW_CHUNKS)
    mem_bf = mem.reshape(batch * N_MEM, d).astype(_BF16)
    mem_k = _project(mem_bf, w_mem_k, tm=batch * N_MEM, tn=512, name="mem_k_proj")
    mem_v = _project(mem_bf, w_mem_v, tm=batch * N_MEM, tn=512, name="mem_v_proj")

    ya = _moba_attention(q_a, proj, batch, seq).reshape(t, ATT_WIDTH)
    yg = _spatial_gating(proj, w_s, b_s, ln_v_g, ln_v_b)
    yc = _memory_attention(proj, mem_k, mem_v, batch, seq).reshape(t, MEM_WIDTH)

    out = _merge_project_norm(ya, yg, yc, gates, x2,
                              w_branch_attn.astype(_BF16), w_branch_sgu.astype(_BF16),
                              w_branch_mem.astype(_BF16), w_out.astype(_BF16), ln_g, ln_b)
    return out.reshape(batch, seq, d)
```

```python
import functools

import jax
import jax.numpy as jnp
from jax import lax
from jax.experimental import pallas as pl
from jax.experimental.pallas import tpu as pltpu

D_MODEL = 2048
DEPTH = 1
N_MEM = 256
ATT_HEAD_DIM = 128
ATT_WIDTH = D_MODEL // 2
ATT_HEADS = ATT_WIDTH // ATT_HEAD_DIM
MOBA_BLOCK = 256
MOBA_TOPK = 3
SGU_WIDTH = D_MODEL // 2
SGU_CHUNK = 128
SGU_GROUP_DIM = 128
SGU_GROUPS = SGU_WIDTH // SGU_GROUP_DIM
MEM_HEADS = 4
MEM_WIDTH = D_MODEL // 2
MEM_HEAD_DIM = MEM_WIDTH // MEM_HEADS
N_BRANCH = 3
DN_ALPHA = (2 * DEPTH) ** 0.25
LN_EPS = 1e-5

PROJ_TILE = 1024
PROJ_TILE0 = 1
GATE_TILE0 = 9
PROJ_WIDTH = (GATE_TILE0 - PROJ_TILE0) * PROJ_TILE
K_A, V_A, Z_A, U_G, V_G, Z_G, Q_C, Z_C = range(8)

VMEM_LIMIT_BYTES = 56 * 1024 * 1024
IN_PROJ_TM = 2048
IN_PROJ_ROW_CHUNKS = 8

_NT = (((1,), (1,)), ((), ()))
_F32 = jnp.float32
_BF16 = jnp.bfloat16


def _sigmoid(x):
    return 1.0 / (1.0 + jnp.exp(-x))


def _silu(x):
    return x * _sigmoid(x)


def _gelu_tanh(x):
    c = 0.7978845608028654
    return x * (0.5 * (1.0 + jnp.tanh(c * (x + 0.044715 * (x * x * x)))))


def _project_kernel(x_ref, w_ref, o_ref, *rest, scale, sigmoid, row_chunks, emit_x_bf16):
    x_bf_ref = rest[0] if emit_x_bf16 else None
    w_bf_ref = rest[-1]

    @pl.when(pl.program_id(1) == 0)
    def _cast_weight_tile():
        w_bf_ref[...] = w_ref[...].astype(_BF16)

    tc = x_ref.shape[0] // row_chunks
    for c in range(row_chunks):
        rows = slice(c * tc, (c + 1) * tc)
        xc = x_ref[rows, :]
        if emit_x_bf16:
            xc = xc.astype(_BF16)
            x_bf_ref[rows, :] = xc
        acc = jnp.dot(xc, w_bf_ref[...], preferred_element_type=_F32)
        if sigmoid:
            acc = _sigmoid(acc)
        if scale is not None:
            acc = acc * scale
        o_ref[rows, :] = acc.astype(o_ref.dtype)


def _project(x, w, *, tm, tn, name, first_col_tile=0, n_col_tiles=None,
             scale=None, sigmoid=False, row_chunks=1, emit_x_bf16=False):
    m, k = x.shape
    if n_col_tiles is None:
        n_col_tiles = w.shape[1] // tn - first_col_tile
    assert m % tm == 0 and tm % row_chunks == 0 and (first_col_tile + n_col_tiles) * tn <= w.shape[1]
    assert not emit_x_bf16 or n_col_tiles == 1
    out_specs = [pl.BlockSpec((tm, tn), lambda j, i: (i, j))]
    out_shape = [jax.ShapeDtypeStruct((m, n_col_tiles * tn), _BF16)]
    if emit_x_bf16:
        out_specs.append(pl.BlockSpec((tm, k), lambda j, i: (i, 0)))
        out_shape.append(jax.ShapeDtypeStruct((m, k), _BF16))
    outs = pl.pallas_call(
        functools.partial(_project_kernel, scale=scale, sigmoid=sigmoid, row_chunks=row_chunks,
                          emit_x_bf16=emit_x_bf16),
        grid=(n_col_tiles, m // tm),
        in_specs=[pl.BlockSpec((tm, k), lambda j, i: (i, 0)),
                  pl.BlockSpec((k, tn), lambda j, i: (0, first_col_tile + j))],
        out_specs=out_specs,
        out_shape=out_shape,
        scratch_shapes=[pltpu.VMEM((k, tn), _BF16)],
        compiler_params=pltpu.CompilerParams(
            dimension_semantics=("arbitrary", "arbitrary"), vmem_limit_bytes=VMEM_LIMIT_BYTES),
        name=name,
    )(x, w)
    return outs if emit_x_bf16 else outs[0]


MOBA_HEADS_PER_STEP = 4
_ACC_ROWS = ATT_HEAD_DIM + 16
MOBA_Q_PRESCALE = ATT_HEAD_DIM ** -0.5 * 1.4426950408889634


def _moba_kernel(q_ref, k_ref, v_ref, z_ref, o_ref,
                 vt_ref, kmean_ref, neg_ref, pos_ref, qt_ref, s_ref, acc_ref, *, n_blocks, heads):
    qi = pl.program_id(2)
    blk = MOBA_BLOCK
    hd = ATT_HEAD_DIM

    @pl.when(qi == 0)
    def _per_head_setup():
        for g in range(heads):
            cols = slice(g * hd, (g + 1) * hd)
            for j in range(n_blocks):
                rows = slice(j * blk, (j + 1) * blk)
                vt_ref[g, j, :hd, :] = v_ref[rows, cols].astype(_F32).T.astype(_BF16)
                vt_ref[g, j, hd:, :] = jnp.ones((_ACC_ROWS - hd, blk), _BF16)
                kmean_ref[g, j:j + 1, :] = (
                    jnp.sum(k_ref[rows, cols].astype(_F32), axis=0, keepdims=True) * (1.0 / blk))

    head_cols = [slice(g * hd, (g + 1) * hd) for g in range(heads)]
    qs = [q_ref[:, c] for c in head_cols]
    for g in range(heads):
        qt_ref[g] = qs[g].astype(_F32).T.astype(_BF16)

    def issue_scores(j, slot):
        start = j * blk if isinstance(j, int) else pl.multiple_of(j * blk, blk)
        for g in range(heads):
            s_ref[slot, g] = jnp.dot(k_ref[pl.ds(start, blk), head_cols[g]], qt_ref[g],
                                     preferred_element_type=_F32)

    sel_scores = []
    for g in range(heads):
        km = kmean_ref[g]
        km_hi = km.astype(_BF16)
        km_lo = (km - km_hi.astype(_F32)).astype(_BF16)
        sel_scores.append(jnp.dot(km_hi, qt_ref[g], preferred_element_type=_F32)
                          + jnp.dot(km_lo, qt_ref[g], preferred_element_type=_F32))
    issue_scores(qi, 2)
    row_id = lax.broadcasted_iota(jnp.int32, (n_blocks, blk), 0)
    past = row_id < qi
    later_rows = [jnp.where(row_id > j, 1.0, 0.0) for j in range(n_blocks)]
    for g in range(heads):
        sb = jnp.where(past, sel_scores[g], -jnp.inf)
        rank = jnp.zeros(sb.shape, _F32)
        for j in range(n_blocks):
            other = sb[j:j + 1, :]
            rank = rank + jnp.where(other > sb, 1.0, 0.0) + jnp.where(other == sb, later_rows[j], 0.0)
        selected = past & (rank < MOBA_TOPK)
        neg_ref[g] = jnp.where(selected, 0.0, -jnp.inf)
        pos_ref[g] = jnp.where(selected, -jnp.inf, jnp.inf)

    def accumulate(j, slot, ms, causal):
        probs, m_news = [], []
        for g in range(heads):
            s = s_ref[slot, g]
            if causal:
                key_pos = lax.broadcasted_iota(jnp.int32, s.shape, 0)
                q_pos = lax.broadcasted_iota(jnp.int32, s.shape, 1)
                s = jnp.where(key_pos <= q_pos, s, -jnp.inf)
                m_new = jnp.maximum(ms[g], jnp.max(s, axis=0, keepdims=True))
                m_sub = m_new
            else:
                m_new = jnp.maximum(ms[g], jnp.max(s, axis=0, keepdims=True) + neg_ref[g, pl.ds(j, 1), :])
                m_sub = jnp.maximum(m_new, pos_ref[g, pl.ds(j, 1), :])
            probs.append(jnp.exp2(s - m_sub).astype(_BF16))
            m_news.append(m_new)
        for g in range(heads):
            alpha = jnp.exp2(jnp.where(m_news[g] == -jnp.inf, 0.0, ms[g] - m_news[g]))
            acc_ref[g] = alpha * acc_ref[g] + jnp.dot(vt_ref[g, j], probs[g], preferred_element_type=_F32)
        return m_news

    def past_block_pair(t, ms):
        issue_scores(2 * t + 1, 1)
        ms = accumulate(2 * t, 0, ms, causal=False)
        issue_scores(jnp.minimum(2 * t + 2, n_blocks - 1), 0)
        return accumulate(2 * t + 1, 1, ms, causal=False)

    for g in range(heads):
        acc_ref[g] = jnp.zeros((_ACC_ROWS, blk), _F32)
    issue_scores(0, 0)
    ms = accumulate(qi, 2, [jnp.full((1, blk), -jnp.inf, _F32)] * heads, causal=True)
    lax.fori_loop(0, (qi + 1) // 2, past_block_pair, ms)

    for g in range(heads):
        attn = (acc_ref[g, :hd, :] / acc_ref[g, hd:hd + 1, :]).T
        o_ref[:, head_cols[g]] = (attn * _silu(z_ref[:, head_cols[g]].astype(_F32))).astype(o_ref.dtype)


def _moba_attention(q, proj, batch, seq, *, heads=MOBA_HEADS_PER_STEP):
    n_blocks = seq // MOBA_BLOCK
    q3 = q.reshape(batch, seq, ATT_WIDTH)
    proj3 = proj.reshape(batch, seq, PROJ_WIDTH)
    width = heads * ATT_HEAD_DIM
    cols = PROJ_TILE // width
    return pl.pallas_call(
        functools.partial(_moba_kernel, n_blocks=n_blocks, heads=heads),
        grid=(batch, ATT_WIDTH // width, n_blocks),
        in_specs=[pl.BlockSpec((None, MOBA_BLOCK, width), lambda b, h, i: (b, i, h)),
                  pl.BlockSpec((None, seq, width), lambda b, h, i: (b, 0, K_A * cols + h)),
                  pl.BlockSpec((None, seq, width), lambda b, h, i: (b, 0, V_A * cols + h)),
                  pl.BlockSpec((None, MOBA_BLOCK, width), lambda b, h, i: (b, i, Z_A * cols + h))],
        out_specs=pl.BlockSpec((None, MOBA_BLOCK, width), lambda b, h, i: (b, i, h)),
        out_shape=jax.ShapeDtypeStruct((batch, seq, ATT_WIDTH), _BF16),
        scratch_shapes=[pltpu.VMEM((heads, n_blocks, _ACC_ROWS, MOBA_BLOCK), _BF16),
                        pltpu.VMEM((heads, n_blocks, ATT_HEAD_DIM), _F32),
                        pltpu.VMEM((heads, n_blocks, MOBA_BLOCK), _F32),
                        pltpu.VMEM((heads, n_blocks, MOBA_BLOCK), _F32),
                        pltpu.VMEM((heads, ATT_HEAD_DIM, MOBA_BLOCK), _BF16),
                        pltpu.VMEM((3, heads, MOBA_BLOCK, MOBA_BLOCK), _F32),
                        pltpu.VMEM((heads, _ACC_ROWS, MOBA_BLOCK), _F32)],
        compiler_params=pltpu.CompilerParams(
            dimension_semantics=("arbitrary", "arbitrary", "arbitrary"), vmem_limit_bytes=VMEM_LIMIT_BYTES),
        name="moba_attention",
    )(q3, proj3, proj3, proj3)


def _sgu_causal_weights(w_ref):
    t_pos = lax.broadcasted_iota(jnp.int32, (SGU_CHUNK, SGU_CHUNK), 0)
    s_pos = lax.broadcasted_iota(jnp.int32, (SGU_CHUNK, SGU_CHUNK), 1)
    return [jnp.where(s_pos <= t_pos, w_ref[g], 0.0).astype(_BF16) for g in range(SGU_GROUPS)]


def _sgu_rows(u_ref, v_ref, z_ref, w_causal, bt_ref, g_ref, b_ref, rows, yg_ref):
    u = _gelu_tanh(u_ref[rows, :].astype(_F32))
    v = _gelu_tanh(v_ref[rows, :].astype(_F32))
    mu = jnp.mean(v, axis=-1, keepdims=True)
    d = v - mu
    var = jnp.mean(d * d, axis=-1, keepdims=True)
    vn = (d * lax.rsqrt(var + LN_EPS) * g_ref[...] + b_ref[...]).astype(_BF16)
    gate = u * _silu(z_ref[rows, :].astype(_F32))
    for g in range(SGU_GROUPS):
        bias = bt_ref[:, g:g + 1]
        cols = slice(g * SGU_GROUP_DIM, (g + 1) * SGU_GROUP_DIM)
        for c in range((rows.stop - rows.start) // SGU_CHUNK):
            sub = slice(c * SGU_CHUNK, (c + 1) * SGU_CHUNK)
            mixed = jnp.dot(w_causal[g], vn[sub, cols], preferred_element_type=_F32) + bias
            yg_ref[rows.start + c * SGU_CHUNK:rows.start + (c + 1) * SGU_CHUNK, cols] = (
                gate[sub, cols] * mixed).astype(yg_ref.dtype)


def _mem_attn_kernel(q_ref, z_ref, k_ref, v_ref, o_ref):
    scale = MEM_HEAD_DIM ** -0.5
    for h in range(MEM_HEADS):
        cols = slice(h * MEM_HEAD_DIM, (h + 1) * MEM_HEAD_DIM)
        s = lax.dot_general(q_ref[:, cols], k_ref[:, cols], _NT, preferred_element_type=_F32) * scale
        p = jnp.exp(s - jnp.max(s, axis=-1, keepdims=True))
        l = jnp.sum(p, axis=-1, keepdims=True)
        o = jnp.dot(p.astype(_BF16), v_ref[:, cols], preferred_element_type=_F32) / l
        o_ref[:, cols] = (o * _silu(z_ref[:, cols].astype(_F32))).astype(o_ref.dtype)


def _memory_attention(proj, mem_k, mem_v, batch, seq, *, tm=512):
    proj3 = proj.reshape(batch, seq, PROJ_WIDTH)
    k3 = mem_k.reshape(batch, N_MEM, MEM_WIDTH)
    v3 = mem_v.reshape(batch, N_MEM, MEM_WIDTH)
    return pl.pallas_call(
        _mem_attn_kernel,
        grid=(batch, seq // tm),
        in_specs=[pl.BlockSpec((None, tm, MEM_WIDTH), lambda b, i: (b, i, Q_C)),
                  pl.BlockSpec((None, tm, MEM_WIDTH), lambda b, i: (b, i, Z_C)),
                  pl.BlockSpec((None, N_MEM, MEM_WIDTH), lambda b, i: (b, 0, 0)),
                  pl.BlockSpec((None, N_MEM, MEM_WIDTH), lambda b, i: (b, 0, 0))],
        out_specs=pl.BlockSpec((None, tm, MEM_WIDTH), lambda b, i: (b, i, 0)),
        out_shape=jax.ShapeDtypeStruct((batch, seq, MEM_WIDTH), _BF16),
        compiler_params=pltpu.CompilerParams(
            dimension_semantics=("arbitrary", "arbitrary"), vmem_limit_bytes=VMEM_LIMIT_BYTES),
        name="memory_attention",
    )(proj3, proj3, k3, v3)


MERGE_TM = 512
OUT_NORM_TM = 512
MERGE_ROW_CHUNK = 256


def _branch_merge_kernel(ya_ref, yc_ref, u_ref, v_ref, z_ref, ws_ref, bt_ref, lvg_ref, lvb_ref, *rest):
    n_gate = N_BRANCH * D_MODEL // PROJ_TILE
    gate_refs = rest[:n_gate]
    wa_ref, wg_ref, wc_ref, o_ref, yg_ref = rest[n_gate:]
    per_branch = D_MODEL // PROJ_TILE
    col_tiles = [slice(c * PROJ_TILE, (c + 1) * PROJ_TILE) for c in range(per_branch)]
    chunks = [slice(r * MERGE_ROW_CHUNK, (r + 1) * MERGE_ROW_CHUNK)
              for r in range(o_ref.shape[0] // MERGE_ROW_CHUNK)]
    w_causal = _sgu_causal_weights(ws_ref)

    def branch_dots(rows):
        n_sub = len(col_tiles)
        sub_rows = (rows.stop - rows.start) // n_sub
        outer = []
        for c, cols in enumerate(col_tiles):
            outer.append((jnp.dot(ya_ref[rows, :], wa_ref[:, cols], preferred_element_type=_F32),
                          jnp.dot(yc_ref[rows, :], wc_ref[:, cols], preferred_element_type=_F32)))
            sub = slice(rows.start + c * sub_rows, rows.start + (c + 1) * sub_rows)
            _sgu_rows(u_ref, v_ref, z_ref, w_causal, bt_ref, lvg_ref, lvb_ref, sub, yg_ref)
        inner = [jnp.dot(yg_ref[rows, :], wg_ref[:, cols], preferred_element_type=_F32) for cols in col_tiles]
        return outer, inner

    dots = branch_dots(chunks[0])
    for k, rows in enumerate(chunks):
        next_dots = branch_dots(chunks[k + 1]) if k + 1 < len(chunks) else None
        outer, inner = dots
        for c, cols in enumerate(col_tiles):
            merged = (gate_refs[c][rows, :].astype(_F32) * outer[c][0]
                      + gate_refs[per_branch + c][rows, :].astype(_F32) * inner[c]
                      + gate_refs[2 * per_branch + c][rows, :].astype(_F32) * outer[c][1])
            o_ref[rows, cols] = merged.astype(o_ref.dtype)
        dots = next_dots


def _out_norm_kernel(m_ref, x_ref, wo_ref, lng_ref, lnb_ref, o_ref):
    chunks = [slice(r * MERGE_ROW_CHUNK, (r + 1) * MERGE_ROW_CHUNK)
              for r in range(o_ref.shape[0] // MERGE_ROW_CHUNK)]

    def out_dot(rows):
        return jnp.dot(m_ref[rows, :], wo_ref[...], preferred_element_type=_F32)

    y = out_dot(chunks[0])
    for k, rows in enumerate(chunks):
        y_next = out_dot(chunks[k + 1]) if k + 1 < len(chunks) else None
        h = DN_ALPHA * x_ref[rows, :] + y
        mu = jnp.mean(h, axis=-1, keepdims=True)
        d = h - mu
        var = jnp.mean(d * d, axis=-1, keepdims=True)
        o_ref[rows, :] = d * lax.rsqrt(var + LN_EPS) * lng_ref[...] + lnb_ref[...]
        y = y_next


def _merge_project_norm(ya, yc, proj, sgu_params, gates, x2, wa, wg, wc, wo, ln_g, ln_b, *, tm=MERGE_TM):
    t = x2.shape[0]
    w_s, b_s, ln_v_g, ln_v_b = sgu_params
    assert t % tm == 0 and tm % MERGE_ROW_CHUNK == 0 and t % OUT_NORM_TM == 0 and OUT_NORM_TM % MERGE_ROW_CHUNK == 0
    n_gate = N_BRANCH * D_MODEL // PROJ_TILE
    branch = pl.BlockSpec((tm, ATT_WIDTH), lambda i: (i, 0))
    gate = lambda c: pl.BlockSpec((tm, PROJ_TILE), lambda i: (i, c))
    rows = pl.BlockSpec((tm, D_MODEL), lambda i: (i, 0))
    resident = lambda shape: pl.BlockSpec(shape, lambda i: (0, 0), pipeline_mode=pl.Buffered(1))
    params = pltpu.CompilerParams(dimension_semantics=("arbitrary",), vmem_limit_bytes=VMEM_LIMIT_BYTES)
    section = lambda c: pl.BlockSpec((tm, PROJ_TILE), lambda i: (i, c))
    whole = lambda shape: pl.BlockSpec(shape, lambda i: (0,) * len(shape))
    merged = pl.pallas_call(
        _branch_merge_kernel,
        grid=(t // tm,),
        in_specs=[branch, branch, section(U_G), section(V_G), section(Z_G),
                  whole((SGU_GROUPS, SGU_CHUNK, SGU_CHUNK)), whole((SGU_CHUNK, SGU_GROUPS)),
                  whole((1, SGU_WIDTH)), whole((1, SGU_WIDTH)),
                  *[gate(c) for c in range(n_gate)],
                  resident((ATT_WIDTH, D_MODEL)), resident((SGU_WIDTH, D_MODEL)), resident((MEM_WIDTH, D_MODEL))],
        out_specs=rows,
        out_shape=jax.ShapeDtypeStruct((t, D_MODEL), _BF16),
        scratch_shapes=[pltpu.VMEM((tm, SGU_WIDTH), _BF16)],
        compiler_params=params,
        name="branch_merge",
    )(ya, yc, proj, proj, proj, w_s, b_s.T, ln_v_g.reshape(1, -1), ln_v_b.reshape(1, -1),
      *([gates] * n_gate), wa, wg, wc)
    out_rows = pl.BlockSpec((OUT_NORM_TM, D_MODEL), lambda i: (i, 0))
    return pl.pallas_call(
        _out_norm_kernel,
        grid=(t // OUT_NORM_TM,),
        in_specs=[out_rows, out_rows,
                  resident((D_MODEL, D_MODEL)), resident((1, D_MODEL)), resident((1, D_MODEL))],
        out_specs=out_rows,
        out_shape=jax.ShapeDtypeStruct((t, D_MODEL), _F32),
        compiler_params=params,
        name="out_proj_norm",
    )(merged, x2, wo, ln_g.reshape(1, -1), ln_b.reshape(1, -1))


def kernel(x, mem, w_in, w_mem_k, w_mem_v, w_s, b_s, ln_v_g, ln_v_b,
           w_branch_attn, w_branch_sgu, w_branch_mem, w_out, ln_g, ln_b):
    batch, seq, d = x.shape
    assert d == D_MODEL and seq % MOBA_BLOCK == 0 and mem.shape[1] == N_MEM
    t = batch * seq
    x2 = x.reshape(t, d)

    assert w_in.shape[1] == GATE_TILE0 * PROJ_TILE + N_BRANCH * D_MODEL and ATT_WIDTH == PROJ_TILE
    q_a, x_bf = _project(x2, w_in, tm=1024, tn=PROJ_TILE, name="in_proj_q", n_col_tiles=1,
                         scale=MOBA_Q_PRESCALE, row_chunks=4, emit_x_bf16=True)
    proj = _project(x_bf, w_in, tm=IN_PROJ_TM, tn=PROJ_TILE, name="in_proj_branches", first_col_tile=PROJ_TILE0,
                    n_col_tiles=GATE_TILE0 - PROJ_TILE0, row_chunks=IN_PROJ_ROW_CHUNKS)
    gates = _project(x_bf, w_in, tm=IN_PROJ_TM, tn=PROJ_TILE, name="in_proj_gates", first_col_tile=GATE_TILE0,
                     sigmoid=True, row_chunks=IN_PROJ_ROW_CHUNKS)
    mem_bf = mem.reshape(batch * N_MEM, d).astype(_BF16)
    mem_k = _project(mem_bf, w_mem_k, tm=batch * N_MEM, tn=512, name="mem_k_proj")
    mem_v = _project(mem_bf, w_mem_v, tm=batch * N_MEM, tn=512, name="mem_v_proj")

    ya = _moba_attention(q_a, proj, batch, seq).reshape(t, ATT_WIDTH)
    yc = _memory_attention(proj, mem_k, mem_v, batch, seq).reshape(t, MEM_WIDTH)

    out = _merge_project_norm(ya, yc, proj, (w_s, b_s, ln_v_g, ln_v_b), gates, x2,
                              w_branch_attn.astype(_BF16), w_branch_sgu.astype(_BF16),
                              w_branch_mem.astype(_BF16), w_out.astype(_BF16), ln_g, ln_b)
    return out.reshape(batch, seq, d)
```

```python
import functools

import jax
import jax.numpy as jnp
from jax import lax
from jax.experimental import pallas as pl
from jax.experimental.pallas import tpu as pltpu

D_MODEL = 2048
DEPTH = 1
N_MEM = 256
ATT_HEAD_DIM = 128
ATT_WIDTH = D_MODEL // 2
ATT_HEADS = ATT_WIDTH // ATT_HEAD_DIM
MOBA_BLOCK = 256
MOBA_TOPK = 3
SGU_WIDTH = D_MODEL // 2
SGU_CHUNK = 128
SGU_GROUP_DIM = 128
SGU_GROUPS = SGU_WIDTH // SGU_GROUP_DIM
MEM_HEADS = 4
MEM_WIDTH = D_MODEL // 2
MEM_HEAD_DIM = MEM_WIDTH // MEM_HEADS
N_BRANCH = 3
DN_ALPHA = (2 * DEPTH) ** 0.25
LN_EPS = 1e-5

PROJ_TILE = 1024
PROJ_TILE0 = 1
GATE_TILE0 = 9
PROJ_WIDTH = (GATE_TILE0 - PROJ_TILE0) * PROJ_TILE
K_A, V_A, Z_A, U_G, V_G, Z_G, Q_C, Z_C = range(8)

VMEM_LIMIT_BYTES = 56 * 1024 * 1024
IN_PROJ_TM = 2048
IN_PROJ_ROW_CHUNKS = 8
CAST_JOB_ROWS = 128

_NT = (((1,), (1,)), ((), ()))
_F32 = jnp.float32
_BF16 = jnp.bfloat16


_LOG2E = 1.4426950408889634


def _sigmoid(x):
    return 1.0 / (1.0 + jnp.exp2(x * -_LOG2E))


def _silu(x):
    return x * _sigmoid(x)


def _gelu_tanh(x):
    a = -2.0 * 0.7978845608028654 * _LOG2E
    return x / (1.0 + jnp.exp2(x * (a + (a * 0.044715) * (x * x))))


def _project_kernel(x_ref, w_ref, o_ref, *rest, scale, sigmoid, row_chunks, emit_x_bf16):
    x_bf_ref = rest[0] if emit_x_bf16 else None
    w_bf_ref = rest[-1]

    @pl.when(pl.program_id(1) == 0)
    def _cast_weight_tile():
        w_bf_ref[...] = w_ref[...].astype(_BF16)

    tc = x_ref.shape[0] // row_chunks
    for c in range(row_chunks):
        rows = slice(c * tc, (c + 1) * tc)
        xc = x_ref[rows, :]
        if emit_x_bf16:
            xc = xc.astype(_BF16)
            x_bf_ref[rows, :] = xc
        acc = jnp.dot(xc, w_bf_ref[...], preferred_element_type=_F32)
        if sigmoid:
            acc = _sigmoid(acc)
        if scale is not None:
            acc = acc * scale
        o_ref[rows, :] = acc.astype(o_ref.dtype)


def _project(x, w, *, tm, tn, name, first_col_tile=0, n_col_tiles=None,
             scale=None, sigmoid=False, row_chunks=1, emit_x_bf16=False):
    m, k = x.shape
    if n_col_tiles is None:
        n_col_tiles = w.shape[1] // tn - first_col_tile
    assert m % tm == 0 and tm % row_chunks == 0 and (first_col_tile + n_col_tiles) * tn <= w.shape[1]
    assert not emit_x_bf16 or n_col_tiles == 1
    out_specs = [pl.BlockSpec((tm, tn), lambda j, i: (i, j))]
    out_shape = [jax.ShapeDtypeStruct((m, n_col_tiles * tn), _BF16)]
    if emit_x_bf16:
        out_specs.append(pl.BlockSpec((tm, k), lambda j, i: (i, 0)))
        out_shape.append(jax.ShapeDtypeStruct((m, k), _BF16))
    outs = pl.pallas_call(
        functools.partial(_project_kernel, scale=scale, sigmoid=sigmoid, row_chunks=row_chunks,
                          emit_x_bf16=emit_x_bf16),
        grid=(n_col_tiles, m // tm),
        in_specs=[pl.BlockSpec((tm, k), lambda j, i: (i, 0)),
                  pl.BlockSpec((k, tn), lambda j, i: (0, first_col_tile + j))],
        out_specs=out_specs,
        out_shape=out_shape,
        scratch_shapes=[pltpu.VMEM((k, tn), _BF16)],
        compiler_params=pltpu.CompilerParams(
            dimension_semantics=("arbitrary", "arbitrary"), vmem_limit_bytes=VMEM_LIMIT_BYTES),
        name=name,
    )(x, w)
    return outs if emit_x_bf16 else outs[0]


def _cast_job_plan(arrays, n_steps, linear_step):
    specs, shapes, blocks, first = [], [], [], 0
    for a in arrays:
        assert a.shape[0] % CAST_JOB_ROWS == 0
        n_blocks = a.shape[0] // CAST_JOB_ROWS
        specs.append(pl.BlockSpec(
            (CAST_JOB_ROWS, a.shape[1]),
            lambda *idx, first=first, n_blocks=n_blocks: (jnp.clip(linear_step(*idx) - first, 0, n_blocks - 1), 0)))
        shapes.append(jax.ShapeDtypeStruct(a.shape, _BF16))
        blocks.append(n_blocks)
        first += n_blocks
    assert first <= n_steps
    return specs, shapes, tuple(blocks)


def _run_cast_jobs(step, job_in, job_out, blocks):
    first = 0
    for src, dst, n_blocks in zip(job_in, job_out, blocks):
        @pl.when((step >= first) & (step < first + n_blocks))
        def _cast_job_block(src=src, dst=dst):
            dst[...] = src[...].astype(_BF16)
        first += n_blocks


MOBA_HEADS_PER_STEP = 4
_ACC_ROWS = ATT_HEAD_DIM + 16
MOBA_Q_PRESCALE = ATT_HEAD_DIM ** -0.5 * 1.4426950408889634


def _moba_kernel(q_ref, k_ref, v_ref, z_ref, *rest, n_blocks, heads, cast_job_blocks):
    n_jobs = len(cast_job_blocks)
    job_in, o_ref, job_out = rest[:n_jobs], rest[n_jobs], rest[n_jobs + 1:2 * n_jobs + 1]
    vt_ref, kmean_ref, neg_ref, pos_ref, qt_ref, s_ref, acc_ref = rest[2 * n_jobs + 1:]
    qi = pl.program_id(2)
    blk = MOBA_BLOCK
    hd = ATT_HEAD_DIM
    step = (pl.program_id(0) * pl.num_programs(1) + pl.program_id(1)) * pl.num_programs(2) + qi
    _run_cast_jobs(step, job_in, job_out, cast_job_blocks)

    @pl.when(qi == 0)
    def _per_head_setup():
        for g in range(heads):
            cols = slice(g * hd, (g + 1) * hd)
            for j in range(n_blocks):
                rows = slice(j * blk, (j + 1) * blk)
                vt_ref[g, j, :hd, :] = v_ref[rows, cols].astype(_F32).T.astype(_BF16)
                vt_ref[g, j, hd:, :] = jnp.ones((_ACC_ROWS - hd, blk), _BF16)
                kmean_ref[g, j:j + 1, :] = (
                    jnp.sum(k_ref[rows, cols].astype(_F32), axis=0, keepdims=True) * (1.0 / blk))

    head_cols = [slice(g * hd, (g + 1) * hd) for g in range(heads)]
    qs = [q_ref[:, c] for c in head_cols]
    for g in range(heads):
        qt_ref[g] = qs[g].astype(_F32).T.astype(_BF16)

    def issue_scores(j, slot):
        start = j * blk if isinstance(j, int) else pl.multiple_of(j * blk, blk)
        for g in range(heads):
            s_ref[slot, g] = jnp.dot(k_ref[pl.ds(start, blk), head_cols[g]], qt_ref[g],
                                     preferred_element_type=_F32)

    sel_scores = []
    for g in range(heads):
        km = kmean_ref[g]
        km_hi = km.astype(_BF16)
        km_lo = (km - km_hi.astype(_F32)).astype(_BF16)
        sel_scores.append(jnp.dot(km_hi, qt_ref[g], preferred_element_type=_F32)
                          + jnp.dot(km_lo, qt_ref[g], preferred_element_type=_F32))
    issue_scores(qi, 2)
    row_id = lax.broadcasted_iota(jnp.int32, (n_blocks, blk), 0)
    past = row_id < qi
    later_rows = [jnp.where(row_id > j, 1.0, 0.0) for j in range(n_blocks)]
    for g in range(heads):
        sb = jnp.where(past, sel_scores[g], -jnp.inf)
        rank = jnp.zeros(sb.shape, _F32)
        for j in range(n_blocks):
            other = sb[j:j + 1, :]
            rank = rank + jnp.where(other > sb, 1.0, 0.0) + jnp.where(other == sb, later_rows[j], 0.0)
        selected = past & (rank < MOBA_TOPK)
        neg_ref[g] = jnp.where(selected, 0.0, -jnp.inf)
        pos_ref[g] = jnp.where(selected, -jnp.inf, jnp.inf)

    def accumulate(j, slot, ms, causal):
        probs, m_news = [], []
        for g in range(heads):
            s = s_ref[slot, g]
            if causal:
                key_pos = lax.broadcasted_iota(jnp.int32, s.shape, 0)
                q_pos = lax.broadcasted_iota(jnp.int32, s.shape, 1)
                s = jnp.where(key_pos <= q_pos, s, -jnp.inf)
                m_new = jnp.maximum(ms[g], jnp.max(s, axis=0, keepdims=True))
                m_sub = m_new
            else:
                m_new = jnp.maximum(ms[g], jnp.max(s, axis=0, keepdims=True) + neg_ref[g, pl.ds(j, 1), :])
                m_sub = jnp.maximum(m_new, pos_ref[g, pl.ds(j, 1), :])
            probs.append(jnp.exp2(s - m_sub).astype(_BF16))
            m_news.append(m_new)
        for g in range(heads):
            alpha = jnp.exp2(jnp.where(m_news[g] == -jnp.inf, 0.0, ms[g] - m_news[g]))
            acc_ref[g] = alpha * acc_ref[g] + jnp.dot(vt_ref[g, j], probs[g], preferred_element_type=_F32)
        return m_news

    def past_block_pair(t, ms):
        issue_scores(2 * t + 1, 1)
        ms = accumulate(2 * t, 0, ms, causal=False)
        issue_scores(jnp.minimum(2 * t + 2, n_blocks - 1), 0)
        return accumulate(2 * t + 1, 1, ms, causal=False)

    for g in range(heads):
        acc_ref[g] = jnp.zeros((_ACC_ROWS, blk), _F32)
    issue_scores(0, 0)
    ms = accumulate(qi, 2, [jnp.full((1, blk), -jnp.inf, _F32)] * heads, causal=True)
    lax.fori_loop(0, (qi + 1) // 2, past_block_pair, ms)

    for g in range(heads):
        attn = (acc_ref[g, :hd, :] / acc_ref[g, hd:hd + 1, :]).T
        o_ref[:, head_cols[g]] = (attn * _silu(z_ref[:, head_cols[g]].astype(_F32))).astype(o_ref.dtype)


def _moba_attention(q, proj, batch, seq, *, heads=MOBA_HEADS_PER_STEP, cast_jobs=()):
    n_blocks = seq // MOBA_BLOCK
    q3 = q.reshape(batch, seq, ATT_WIDTH)
    proj3 = proj.reshape(batch, seq, PROJ_WIDTH)
    width = heads * ATT_HEAD_DIM
    cols = PROJ_TILE // width
    grid = (batch, ATT_WIDTH // width, n_blocks)
    job_specs, job_shapes, job_blocks = _cast_job_plan(
        cast_jobs, grid[0] * grid[1] * grid[2], lambda b, h, i: (b * grid[1] + h) * grid[2] + i)
    return pl.pallas_call(
        functools.partial(_moba_kernel, n_blocks=n_blocks, heads=heads, cast_job_blocks=job_blocks),
        grid=grid,
        in_specs=[pl.BlockSpec((None, MOBA_BLOCK, width), lambda b, h, i: (b, i, h)),
                  pl.BlockSpec((None, seq, width), lambda b, h, i: (b, 0, K_A * cols + h)),
                  pl.BlockSpec((None, seq, width), lambda b, h, i: (b, 0, V_A * cols + h)),
                  pl.BlockSpec((None, MOBA_BLOCK, width), lambda b, h, i: (b, i, Z_A * cols + h)),
                  *job_specs],
        out_specs=[pl.BlockSpec((None, MOBA_BLOCK, width), lambda b, h, i: (b, i, h)), *job_specs],
        out_shape=[jax.ShapeDtypeStruct((batch, seq, ATT_WIDTH), _BF16), *job_shapes],
        scratch_shapes=[pltpu.VMEM((heads, n_blocks, _ACC_ROWS, MOBA_BLOCK), _BF16),
                        pltpu.VMEM((heads, n_blocks, ATT_HEAD_DIM), _F32),
                        pltpu.VMEM((heads, n_blocks, MOBA_BLOCK), _F32),
                        pltpu.VMEM((heads, n_blocks, MOBA_BLOCK), _F32),
                        pltpu.VMEM((heads, ATT_HEAD_DIM, MOBA_BLOCK), _BF16),
                        pltpu.VMEM((3, heads, MOBA_BLOCK, MOBA_BLOCK), _F32),
                        pltpu.VMEM((heads, _ACC_ROWS, MOBA_BLOCK), _F32)],
        compiler_params=pltpu.CompilerParams(
            dimension_semantics=("arbitrary", "arbitrary", "arbitrary"), vmem_limit_bytes=VMEM_LIMIT_BYTES),
        name="moba_attention",
    )(q3, proj3, proj3, proj3, *cast_jobs)


def _sgu_causal_weights(w_ref):
    t_pos = lax.broadcasted_iota(jnp.int32, (SGU_CHUNK, SGU_CHUNK), 0)
    s_pos = lax.broadcasted_iota(jnp.int32, (SGU_CHUNK, SGU_CHUNK), 1)
    return [jnp.where(s_pos <= t_pos, w_ref[g], 0.0).astype(_BF16) for g in range(SGU_GROUPS)]


def _sgu_rows(u_ref, v_ref, z_ref, w_causal, bt_ref, g_ref, b_ref, rows, yg_ref):
    u = _gelu_tanh(u_ref[rows, :].astype(_F32))
    v = _gelu_tanh(v_ref[rows, :].astype(_F32))
    mu = jnp.mean(v, axis=-1, keepdims=True)
    d = v - mu
    var = jnp.mean(d * d, axis=-1, keepdims=True)
    vn = (d * lax.rsqrt(var + LN_EPS) * g_ref[...] + b_ref[...]).astype(_BF16)
    gate = u * _silu(z_ref[rows, :].astype(_F32))
    for g in range(SGU_GROUPS):
        bias = bt_ref[:, g:g + 1]
        cols = slice(g * SGU_GROUP_DIM, (g + 1) * SGU_GROUP_DIM)
        for c in range((rows.stop - rows.start) // SGU_CHUNK):
            sub = slice(c * SGU_CHUNK, (c + 1) * SGU_CHUNK)
            mixed = jnp.dot(w_causal[g], vn[sub, cols], preferred_element_type=_F32) + bias
            yg_ref[rows.start + c * SGU_CHUNK:rows.start + (c + 1) * SGU_CHUNK, cols] = (
                gate[sub, cols] * mixed).astype(yg_ref.dtype)


def _mem_attn_kernel(q_ref, z_ref, mem_ref, wk_ref, wv_ref, o_ref, k_ref, v_ref):
    @pl.when(pl.program_id(1) == 0)
    def _project_memory():
        mem_bf = mem_ref[...].astype(_BF16)
        for w_ref, dst in ((wk_ref, k_ref), (wv_ref, v_ref)):
            for c in range(0, MEM_WIDTH, MEM_HEAD_DIM):
                dst[:, c:c + MEM_HEAD_DIM] = jnp.dot(
                    mem_bf, w_ref[:, c:c + MEM_HEAD_DIM].astype(_BF16), preferred_element_type=_F32
                ).astype(_BF16)

    scale = MEM_HEAD_DIM ** -0.5
    for h in range(MEM_HEADS):
        cols = slice(h * MEM_HEAD_DIM, (h + 1) * MEM_HEAD_DIM)
        s = lax.dot_general(q_ref[:, cols], k_ref[:, cols], _NT, preferred_element_type=_F32) * scale
        p = jnp.exp(s - jnp.max(s, axis=-1, keepdims=True))
        l = jnp.sum(p, axis=-1, keepdims=True)
        o = jnp.dot(p.astype(_BF16), v_ref[:, cols], preferred_element_type=_F32) / l
        o_ref[:, cols] = (o * _silu(z_ref[:, cols].astype(_F32))).astype(o_ref.dtype)


def _memory_attention(proj, mem, w_mem_k, w_mem_v, seq, *, tm=512):
    batch, n_mem, d = mem.shape
    proj3 = proj.reshape(batch, seq, PROJ_WIDTH)
    weight = pl.BlockSpec((d, MEM_WIDTH), lambda b, i: (0, 0), pipeline_mode=pl.Buffered(1))
    return pl.pallas_call(
        _mem_attn_kernel,
        grid=(batch, seq // tm),
        in_specs=[pl.BlockSpec((None, tm, MEM_WIDTH), lambda b, i: (b, i, Q_C)),
                  pl.BlockSpec((None, tm, MEM_WIDTH), lambda b, i: (b, i, Z_C)),
                  pl.BlockSpec((None, n_mem, d), lambda b, i: (b, 0, 0)),
                  weight, weight],
        out_specs=pl.BlockSpec((None, tm, MEM_WIDTH), lambda b, i: (b, i, 0)),
        out_shape=jax.ShapeDtypeStruct((batch, seq, MEM_WIDTH), _BF16),
        scratch_shapes=[pltpu.VMEM((n_mem, MEM_WIDTH), _BF16), pltpu.VMEM((n_mem, MEM_WIDTH), _BF16)],
        compiler_params=pltpu.CompilerParams(
            dimension_semantics=("arbitrary", "arbitrary"), vmem_limit_bytes=VMEM_LIMIT_BYTES),
        name="memory_attention",
    )(proj3, proj3, mem, w_mem_k, w_mem_v)


MERGE_TM = 512
OUT_NORM_TM = 512
OUT_NORM_CHUNKS = (256, 128, 128)
MERGE_ROW_CHUNK = 256


def _branch_merge_kernel(ya_ref, yc_ref, u_ref, v_ref, z_ref, ws_ref, bt_ref, lvg_ref, lvb_ref, *rest):
    n_gate = N_BRANCH * D_MODEL // PROJ_TILE
    gate_refs = rest[:n_gate]
    wa_ref, wg_ref, wc_ref, o_ref, yg_ref = rest[n_gate:]
    per_branch = D_MODEL // PROJ_TILE
    col_tiles = [slice(c * PROJ_TILE, (c + 1) * PROJ_TILE) for c in range(per_branch)]
    chunks = [slice(r * MERGE_ROW_CHUNK, (r + 1) * MERGE_ROW_CHUNK)
              for r in range(o_ref.shape[0] // MERGE_ROW_CHUNK)]
    w_causal = _sgu_causal_weights(ws_ref)

    def branch_dots(rows):
        n_sub = len(col_tiles)
        sub_rows = (rows.stop - rows.start) // n_sub
        outer = []
        for c, cols in enumerate(col_tiles):
            outer.append((jnp.dot(ya_ref[rows, :], wa_ref[:, cols], preferred_element_type=_F32),
                          jnp.dot(yc_ref[rows, :], wc_ref[:, cols], preferred_element_type=_F32)))
            sub = slice(rows.start + c * sub_rows, rows.start + (c + 1) * sub_rows)
            _sgu_rows(u_ref, v_ref, z_ref, w_causal, bt_ref, lvg_ref, lvb_ref, sub, yg_ref)
        inner = [jnp.dot(yg_ref[rows, :], wg_ref[:, cols], preferred_element_type=_F32) for cols in col_tiles]
        return outer, inner

    dots = branch_dots(chunks[0])
    for k, rows in enumerate(chunks):
        next_dots = branch_dots(chunks[k + 1]) if k + 1 < len(chunks) else None
        outer, inner = dots
        for c, cols in enumerate(col_tiles):
            merged = (gate_refs[c][rows, :].astype(_F32) * outer[c][0]
                      + gate_refs[per_branch + c][rows, :].astype(_F32) * inner[c]
                      + gate_refs[2 * per_branch + c][rows, :].astype(_F32) * outer[c][1])
            o_ref[rows, cols] = merged.astype(o_ref.dtype)
        dots = next_dots


def _out_norm_kernel(m_ref, x_ref, wo_ref, lng_ref, lnb_ref, o_ref):
    bounds = [0]
    for size in OUT_NORM_CHUNKS:
        bounds.append(bounds[-1] + size)
    assert bounds[-1] == o_ref.shape[0]
    chunks = [slice(a, b) for a, b in zip(bounds[:-1], bounds[1:])]

    def out_dot(rows):
        return jnp.dot(m_ref[rows, :], wo_ref[...], preferred_element_type=_F32)

    y = out_dot(chunks[0])
    for k, rows in enumerate(chunks):
        y_next = out_dot(chunks[k + 1]) if k + 1 < len(chunks) else None
        h = DN_ALPHA * x_ref[rows, :] + y
        mu = jnp.mean(h, axis=-1, keepdims=True)
        d = h - mu
        var = jnp.mean(d * d, axis=-1, keepdims=True)
        o_ref[rows, :] = d * lax.rsqrt(var + LN_EPS) * lng_ref[...] + lnb_ref[...]
        y = y_next


def _merge_project_norm(ya, yc, proj, sgu_params, gates, x2, wa, wg, wc, wo, ln_g, ln_b, *, tm=MERGE_TM):
    t = x2.shape[0]
    w_s, b_s, ln_v_g, ln_v_b = sgu_params
    assert t % tm == 0 and tm % MERGE_ROW_CHUNK == 0 and t % OUT_NORM_TM == 0 and OUT_NORM_TM % MERGE_ROW_CHUNK == 0
    n_gate = N_BRANCH * D_MODEL // PROJ_TILE
    branch = pl.BlockSpec((tm, ATT_WIDTH), lambda i: (i, 0))
    gate = lambda c: pl.BlockSpec((tm, PROJ_TILE), lambda i: (i, c))
    rows = pl.BlockSpec((tm, D_MODEL), lambda i: (i, 0))
    resident = lambda shape: pl.BlockSpec(shape, lambda i: (0, 0), pipeline_mode=pl.Buffered(1))
    params = pltpu.CompilerParams(dimension_semantics=("arbitrary",), vmem_limit_bytes=VMEM_LIMIT_BYTES)
    section = lambda c: pl.BlockSpec((tm, PROJ_TILE), lambda i: (i, c))
    whole = lambda shape: pl.BlockSpec(shape, lambda i: (0,) * len(shape))
    merged = pl.pallas_call(
        _branch_merge_kernel,
        grid=(t // tm,),
        in_specs=[branch, branch, section(U_G), section(V_G), section(Z_G),
                  whole((SGU_GROUPS, SGU_CHUNK, SGU_CHUNK)), whole((SGU_CHUNK, SGU_GROUPS)),
                  whole((1, SGU_WIDTH)), whole((1, SGU_WIDTH)),
                  *[gate(c) for c in range(n_gate)],
                  resident((ATT_WIDTH, D_MODEL)), resident((SGU_WIDTH, D_MODEL)), resident((MEM_WIDTH, D_MODEL))],
        out_specs=rows,
        out_shape=jax.ShapeDtypeStruct((t, D_MODEL), _BF16),
        scratch_shapes=[pltpu.VMEM((tm, SGU_WIDTH), _BF16)],
        compiler_params=params,
        name="branch_merge",
    )(ya, yc, proj, proj, proj, w_s, b_s.T, ln_v_g.reshape(1, -1), ln_v_b.reshape(1, -1),
      *([gates] * n_gate), wa, wg, wc)
    out_rows = pl.BlockSpec((OUT_NORM_TM, D_MODEL), lambda i: (i, 0))
    return pl.pallas_call(
        _out_norm_kernel,
        grid=(t // OUT_NORM_TM,),
        in_specs=[out_rows, out_rows,
                  resident((D_MODEL, D_MODEL)), resident((1, D_MODEL)), resident((1, D_MODEL))],
        out_specs=out_rows,
        out_shape=jax.ShapeDtypeStruct((t, D_MODEL), _F32),
        compiler_params=params,
        name="out_proj_norm",
    )(merged, x2, wo, ln_g.reshape(1, -1), ln_b.reshape(1, -1))


def kernel(x, mem, w_in, w_mem_k, w_mem_v, w_s, b_s, ln_v_g, ln_v_b,
           w_branch_attn, w_branch_sgu, w_branch_mem, w_out, ln_g, ln_b):
    batch, seq, d = x.shape
    assert d == D_MODEL and seq % MOBA_BLOCK == 0 and mem.shape[1] == N_MEM
    t = batch * seq
    x2 = x.reshape(t, d)

    assert w_in.shape[1] == GATE_TILE0 * PROJ_TILE + N_BRANCH * D_MODEL and ATT_WIDTH == PROJ_TILE
    q_a, x_bf = _project(x2, w_in, tm=1024, tn=PROJ_TILE, name="in_proj_q", n_col_tiles=1,
                         scale=MOBA_Q_PRESCALE, row_chunks=4, emit_x_bf16=True)
    proj = _project(x_bf, w_in, tm=IN_PROJ_TM, tn=PROJ_TILE, name="in_proj_branches", first_col_tile=PROJ_TILE0,
                    n_col_tiles=GATE_TILE0 - PROJ_TILE0, row_chunks=IN_PROJ_ROW_CHUNKS)
    gates = _project(x_bf, w_in, tm=IN_PROJ_TM, tn=PROJ_TILE, name="in_proj_gates", first_col_tile=GATE_TILE0,
                     sigmoid=True, row_chunks=IN_PROJ_ROW_CHUNKS)

    ya, wa, wg, wc, wo = _moba_attention(q_a, proj, batch, seq,
                                         cast_jobs=(w_branch_attn, w_branch_sgu, w_branch_mem, w_out))
    ya = ya.reshape(t, ATT_WIDTH)
    yc = _memory_attention(proj, mem, w_mem_k, w_mem_v, seq).reshape(t, MEM_WIDTH)

    out = _merge_project_norm(ya, yc, proj, (w_s, b_s, ln_v_g, ln_v_b), gates, x2, wa, wg, wc, wo, ln_g, ln_b)
    return out.reshape(batch, seq, d)
```

```python
import functools

import jax
import jax.numpy as jnp
from jax import lax
from jax.experimental import pallas as pl
from jax.experimental.pallas import tpu as pltpu

D_MODEL = 2048
DEPTH = 1
N_MEM = 256
ATT_HEAD_DIM = 128
ATT_WIDTH = D_MODEL // 2
ATT_HEADS = ATT_WIDTH // ATT_HEAD_DIM
MOBA_BLOCK = 256
MOBA_TOPK = 3
SGU_WIDTH = D_MODEL // 2
SGU_CHUNK = 128
SGU_GROUP_DIM = 128
SGU_GROUPS = SGU_WIDTH // SGU_GROUP_DIM
MEM_HEADS = 4
MEM_WIDTH = D_MODEL // 2
MEM_HEAD_DIM = MEM_WIDTH // MEM_HEADS
N_BRANCH = 3
DN_ALPHA = (2 * DEPTH) ** 0.25
LN_EPS = 1e-5

PROJ_TILE = 1024
PROJ_TILE0 = 1
GATE_TILE0 = 9
PROJ_WIDTH = (GATE_TILE0 - PROJ_TILE0) * PROJ_TILE
K_A, V_A, Z_A, U_G, V_G, Z_G, Q_C, Z_C = range(8)

VMEM_LIMIT_BYTES = 56 * 1024 * 1024
IN_PROJ_TM = 2048
IN_PROJ_ROW_CHUNKS = 8
CAST_JOB_ROWS = 256

_NT = (((1,), (1,)), ((), ()))
_F32 = jnp.float32
_BF16 = jnp.bfloat16


_LOG2E = 1.4426950408889634


def _sigmoid(x):
    return 1.0 / (1.0 + jnp.exp2(x * -_LOG2E))


def _silu(x):
    return x * _sigmoid(x)


def _gelu_tanh(x):
    a = -2.0 * 0.7978845608028654 * _LOG2E
    return x / (1.0 + jnp.exp2(x * (a + (a * 0.044715) * (x * x))))


def _project_kernel(x_ref, w_ref, o_ref, *rest, scale, sigmoid, row_chunks, emit_x_bf16):
    x_bf_ref = rest[0] if emit_x_bf16 else None
    w_bf_ref = rest[-1]

    @pl.when(pl.program_id(1) == 0)
    def _cast_weight_tile():
        w_bf_ref[...] = w_ref[...].astype(_BF16)

    tc = x_ref.shape[0] // row_chunks
    for c in range(row_chunks):
        rows = slice(c * tc, (c + 1) * tc)
        xc = x_ref[rows, :]
        if emit_x_bf16:
            xc = xc.astype(_BF16)
            x_bf_ref[rows, :] = xc
        acc = jnp.dot(xc, w_bf_ref[...], preferred_element_type=_F32)
        if sigmoid:
            acc = _sigmoid(acc)
        if scale is not None:
            acc = acc * scale
        o_ref[rows, :] = acc.astype(o_ref.dtype)


def _project(x, w, *, tm, tn, name, first_col_tile=0, n_col_tiles=None,
             scale=None, sigmoid=False, row_chunks=1, emit_x_bf16=False):
    m, k = x.shape
    if n_col_tiles is None:
        n_col_tiles = w.shape[1] // tn - first_col_tile
    assert m % tm == 0 and tm % row_chunks == 0 and (first_col_tile + n_col_tiles) * tn <= w.shape[1]
    assert not emit_x_bf16 or n_col_tiles == 1
    out_specs = [pl.BlockSpec((tm, tn), lambda j, i: (i, j))]
    out_shape = [jax.ShapeDtypeStruct((m, n_col_tiles * tn), _BF16)]
    if emit_x_bf16:
        out_specs.append(pl.BlockSpec((tm, k), lambda j, i: (i, 0)))
        out_shape.append(jax.ShapeDtypeStruct((m, k), _BF16))
    outs = pl.pallas_call(
        functools.partial(_project_kernel, scale=scale, sigmoid=sigmoid, row_chunks=row_chunks,
                          emit_x_bf16=emit_x_bf16),
        grid=(n_col_tiles, m // tm),
        in_specs=[pl.BlockSpec((tm, k), lambda j, i: (i, 0)),
                  pl.BlockSpec((k, tn), lambda j, i: (0, first_col_tile + j))],
        out_specs=out_specs,
        out_shape=out_shape,
        scratch_shapes=[pltpu.VMEM((k, tn), _BF16)],
        compiler_params=pltpu.CompilerParams(
            dimension_semantics=("arbitrary", "arbitrary"), vmem_limit_bytes=VMEM_LIMIT_BYTES),
        name=name,
    )(x, w)
    return outs if emit_x_bf16 else outs[0]


def _cast_job_plan(arrays, n_steps, linear_step):
    specs, shapes, blocks, first = [], [], [], 0
    for a in arrays:
        assert a.shape[0] % CAST_JOB_ROWS == 0
        n_blocks = a.shape[0] // CAST_JOB_ROWS
        specs.append(pl.BlockSpec(
            (CAST_JOB_ROWS, a.shape[1]),
            lambda *idx, first=first, n_blocks=n_blocks: (jnp.clip(linear_step(*idx) - first, 0, n_blocks - 1), 0)))
        shapes.append(jax.ShapeDtypeStruct(a.shape, _BF16))
        blocks.append(n_blocks)
        first += n_blocks
    assert first <= n_steps
    return specs, shapes, tuple(blocks)


def _run_cast_jobs(step, job_in, job_out, blocks):
    first = 0
    for src, dst, n_blocks in zip(job_in, job_out, blocks):
        @pl.when((step >= first) & (step < first + n_blocks))
        def _cast_job_block(src=src, dst=dst):
            dst[...] = src[...].astype(_BF16)
        first += n_blocks


MOBA_HEADS_PER_STEP = 4
MOBA_Q_BLOCKS_PER_STEP = 2
_ACC_ROWS = ATT_HEAD_DIM + 16
MOBA_Q_PRESCALE = ATT_HEAD_DIM ** -0.5 * 1.4426950408889634


def _moba_kernel(q_ref, k_ref, v_ref, z_ref, *rest, n_blocks, heads, cast_job_blocks):
    n_jobs = len(cast_job_blocks)
    job_in, o_ref, job_out = rest[:n_jobs], rest[n_jobs], rest[n_jobs + 1:2 * n_jobs + 1]
    vt_ref, kmean_ref, neg_ref, pos_ref, qt_ref, s_ref, acc_ref = rest[2 * n_jobs + 1:]
    blk = MOBA_BLOCK
    hd = ATT_HEAD_DIM
    step = (pl.program_id(0) * pl.num_programs(1) + pl.program_id(1)) * pl.num_programs(2) + pl.program_id(2)
    _run_cast_jobs(step, job_in, job_out, cast_job_blocks)

    @pl.when(pl.program_id(2) == 0)
    def _per_head_setup():
        for g in range(heads):
            cols = slice(g * hd, (g + 1) * hd)
            for j in range(n_blocks):
                rows = slice(j * blk, (j + 1) * blk)
                vt_ref[g, j, :hd, :] = v_ref[rows, cols].astype(_F32).T.astype(_BF16)
                vt_ref[g, j, hd:, :] = jnp.ones((_ACC_ROWS - hd, blk), _BF16)
                kmean_ref[g, j:j + 1, :] = (
                    jnp.sum(k_ref[rows, cols].astype(_F32), axis=0, keepdims=True) * (1.0 / blk))

    for sub in range(MOBA_Q_BLOCKS_PER_STEP):
        rows = slice(sub * blk, (sub + 1) * blk)
        _moba_query_block(pl.program_id(2) * MOBA_Q_BLOCKS_PER_STEP + sub, q_ref.at[rows], z_ref.at[rows],
                          o_ref.at[rows], k_ref, vt_ref, kmean_ref, neg_ref, pos_ref, qt_ref, s_ref, acc_ref,
                          n_blocks=n_blocks, heads=heads)


def _moba_query_block(qi, q_ref, z_ref, o_ref, k_ref, vt_ref, kmean_ref, neg_ref, pos_ref, qt_ref, s_ref, acc_ref,
                      *, n_blocks, heads):
    blk = MOBA_BLOCK
    hd = ATT_HEAD_DIM
    head_cols = [slice(g * hd, (g + 1) * hd) for g in range(heads)]
    qs = [q_ref[:, c] for c in head_cols]
    for g in range(heads):
        qt_ref[g] = qs[g].astype(_F32).T.astype(_BF16)

    def issue_scores(j, slot):
        start = j * blk if isinstance(j, int) else pl.multiple_of(j * blk, blk)
        for g in range(heads):
            s_ref[slot, g] = jnp.dot(k_ref[pl.ds(start, blk), head_cols[g]], qt_ref[g],
                                     preferred_element_type=_F32)

    sel_scores = []
    for g in range(heads):
        km = kmean_ref[g]
        km_hi = km.astype(_BF16)
        km_lo = (km - km_hi.astype(_F32)).astype(_BF16)
        sel_scores.append(jnp.dot(km_hi, qt_ref[g], preferred_element_type=_F32)
                          + jnp.dot(km_lo, qt_ref[g], preferred_element_type=_F32))
    issue_scores(qi, 2)
    row_id = lax.broadcasted_iota(jnp.int32, (n_blocks, blk), 0)
    past = row_id < qi
    later_rows = [jnp.where(row_id > j, 1.0, 0.0) for j in range(n_blocks)]
    for g in range(heads):
        sb = jnp.where(past, sel_scores[g], -jnp.inf)
        rank = jnp.zeros(sb.shape, _F32)
        for j in range(n_blocks):
            other = sb[j:j + 1, :]
            rank = rank + jnp.where(other > sb, 1.0, 0.0) + jnp.where(other == sb, later_rows[j], 0.0)
        selected = past & (rank < MOBA_TOPK)
        neg_ref[g] = jnp.where(selected, 0.0, -jnp.inf)
        pos_ref[g] = jnp.where(selected, -jnp.inf, jnp.inf)

    def accumulate(j, slot, ms, causal):
        probs, m_news = [], []
        for g in range(heads):
            s = s_ref[slot, g]
            if causal:
                key_pos = lax.broadcasted_iota(jnp.int32, s.shape, 0)
                q_pos = lax.broadcasted_iota(jnp.int32, s.shape, 1)
                s = jnp.where(key_pos <= q_pos, s, -jnp.inf)
                m_new = jnp.maximum(ms[g], jnp.max(s, axis=0, keepdims=True))
                m_sub = m_new
            else:
                m_new = jnp.maximum(ms[g], jnp.max(s, axis=0, keepdims=True) + neg_ref[g, pl.ds(j, 1), :])
                m_sub = jnp.maximum(m_new, pos_ref[g, pl.ds(j, 1), :])
            probs.append(jnp.exp2(s - m_sub).astype(_BF16))
            m_news.append(m_new)
        for g in range(heads):
            alpha = jnp.exp2(jnp.where(m_news[g] == -jnp.inf, 0.0, ms[g] - m_news[g]))
            acc_ref[g] = alpha * acc_ref[g] + jnp.dot(vt_ref[g, j], probs[g], preferred_element_type=_F32)
        return m_news

    def past_block_pair(t, ms):
        issue_scores(2 * t + 1, 1)
        ms = accumulate(2 * t, 0, ms, causal=False)
        issue_scores(jnp.minimum(2 * t + 2, n_blocks - 1), 0)
        return accumulate(2 * t + 1, 1, ms, causal=False)

    for g in range(heads):
        acc_ref[g] = jnp.zeros((_ACC_ROWS, blk), _F32)
    issue_scores(0, 0)
    ms = accumulate(qi, 2, [jnp.full((1, blk), -jnp.inf, _F32)] * heads, causal=True)
    lax.fori_loop(0, (qi + 1) // 2, past_block_pair, ms)

    for g in range(heads):
        attn = (acc_ref[g, :hd, :] / acc_ref[g, hd:hd + 1, :]).T
        o_ref[:, head_cols[g]] = (attn * _silu(z_ref[:, head_cols[g]].astype(_F32))).astype(o_ref.dtype)


def _moba_attention(q, proj, batch, seq, *, heads=MOBA_HEADS_PER_STEP, cast_jobs=()):
    n_blocks = seq // MOBA_BLOCK
    q3 = q.reshape(batch, seq, ATT_WIDTH)
    proj3 = proj.reshape(batch, seq, PROJ_WIDTH)
    width = heads * ATT_HEAD_DIM
    cols = PROJ_TILE // width
    q_rows = MOBA_Q_BLOCKS_PER_STEP * MOBA_BLOCK
    assert n_blocks % MOBA_Q_BLOCKS_PER_STEP == 0
    grid = (batch, ATT_WIDTH // width, n_blocks // MOBA_Q_BLOCKS_PER_STEP)
    job_specs, job_shapes, job_blocks = _cast_job_plan(
        cast_jobs, grid[0] * grid[1] * grid[2], lambda b, h, i: (b * grid[1] + h) * grid[2] + i)
    return pl.pallas_call(
        functools.partial(_moba_kernel, n_blocks=n_blocks, heads=heads, cast_job_blocks=job_blocks),
        grid=grid,
        in_specs=[pl.BlockSpec((None, q_rows, width), lambda b, h, i: (b, i, h)),
                  pl.BlockSpec((None, seq, width), lambda b, h, i: (b, 0, K_A * cols + h)),
                  pl.BlockSpec((None, seq, width), lambda b, h, i: (b, 0, V_A * cols + h)),
                  pl.BlockSpec((None, q_rows, width), lambda b, h, i: (b, i, Z_A * cols + h)),
                  *job_specs],
        out_specs=[pl.BlockSpec((None, q_rows, width), lambda b, h, i: (b, i, h)), *job_specs],
        out_shape=[jax.ShapeDtypeStruct((batch, seq, ATT_WIDTH), _BF16), *job_shapes],
        scratch_shapes=[pltpu.VMEM((heads, n_blocks, _ACC_ROWS, MOBA_BLOCK), _BF16),
                        pltpu.VMEM((heads, n_blocks, ATT_HEAD_DIM), _F32),
                        pltpu.VMEM((heads, n_blocks, MOBA_BLOCK), _F32),
                        pltpu.VMEM((heads, n_blocks, MOBA_BLOCK), _F32),
                        pltpu.VMEM((heads, ATT_HEAD_DIM, MOBA_BLOCK), _BF16),
                        pltpu.VMEM((3, heads, MOBA_BLOCK, MOBA_BLOCK), _F32),
                        pltpu.VMEM((heads, _ACC_ROWS, MOBA_BLOCK), _F32)],
        compiler_params=pltpu.CompilerParams(
            dimension_semantics=("arbitrary", "arbitrary", "arbitrary"), vmem_limit_bytes=VMEM_LIMIT_BYTES),
        name="moba_attention",
    )(q3, proj3, proj3, proj3, *cast_jobs)


def _sgu_causal_weights(w_ref):
    t_pos = lax.broadcasted_iota(jnp.int32, (SGU_CHUNK, SGU_CHUNK), 0)
    s_pos = lax.broadcasted_iota(jnp.int32, (SGU_CHUNK, SGU_CHUNK), 1)
    return [jnp.where(s_pos <= t_pos, w_ref[g], 0.0).astype(_BF16) for g in range(SGU_GROUPS)]


def _sgu_rows(u_ref, v_ref, z_ref, w_causal, bt_ref, g_ref, b_ref, rows, yg_ref):
    u = _gelu_tanh(u_ref[rows, :].astype(_F32))
    v = _gelu_tanh(v_ref[rows, :].astype(_F32))
    mu = jnp.mean(v, axis=-1, keepdims=True)
    d = v - mu
    var = jnp.mean(d * d, axis=-1, keepdims=True)
    vn = (d * lax.rsqrt(var + LN_EPS) * g_ref[...] + b_ref[...]).astype(_BF16)
    gate = u * _silu(z_ref[rows, :].astype(_F32))
    for g in range(SGU_GROUPS):
        bias = bt_ref[:, g:g + 1]
        cols = slice(g * SGU_GROUP_DIM, (g + 1) * SGU_GROUP_DIM)
        for c in range((rows.stop - rows.start) // SGU_CHUNK):
            sub = slice(c * SGU_CHUNK, (c + 1) * SGU_CHUNK)
            mixed = jnp.dot(w_causal[g], vn[sub, cols], preferred_element_type=_F32) + bias
            yg_ref[rows.start + c * SGU_CHUNK:rows.start + (c + 1) * SGU_CHUNK, cols] = (
                gate[sub, cols] * mixed).astype(yg_ref.dtype)


def _mem_attn_kernel(q_ref, z_ref, mem_ref, wk_ref, wv_ref, o_ref, k_ref, v_ref):
    @pl.when(pl.program_id(1) == 0)
    def _project_memory():
        mem_bf = mem_ref[...].astype(_BF16)
        for w_ref, dst in ((wk_ref, k_ref), (wv_ref, v_ref)):
            for c in range(0, MEM_WIDTH, MEM_HEAD_DIM):
                dst[:, c:c + MEM_HEAD_DIM] = jnp.dot(
                    mem_bf, w_ref[:, c:c + MEM_HEAD_DIM].astype(_BF16), preferred_element_type=_F32
                ).astype(_BF16)

    scale = MEM_HEAD_DIM ** -0.5
    for h in range(MEM_HEADS):
        cols = slice(h * MEM_HEAD_DIM, (h + 1) * MEM_HEAD_DIM)
        s = lax.dot_general(q_ref[:, cols], k_ref[:, cols], _NT, preferred_element_type=_F32) * scale
        p = jnp.exp(s - jnp.max(s, axis=-1, keepdims=True))
        l = jnp.sum(p, axis=-1, keepdims=True)
        o = jnp.dot(p.astype(_BF16), v_ref[:, cols], preferred_element_type=_F32) / l
        o_ref[:, cols] = (o * _silu(z_ref[:, cols].astype(_F32))).astype(o_ref.dtype)


def _memory_attention(proj, mem, w_mem_k, w_mem_v, seq, *, tm=512):
    batch, n_mem, d = mem.shape
    proj3 = proj.reshape(batch, seq, PROJ_WIDTH)
    weight = pl.BlockSpec((d, MEM_WIDTH), lambda b, i: (0, 0), pipeline_mode=pl.Buffered(1))
    return pl.pallas_call(
        _mem_attn_kernel,
        grid=(batch, seq // tm),
        in_specs=[pl.BlockSpec((None, tm, MEM_WIDTH), lambda b, i: (b, i, Q_C)),
                  pl.BlockSpec((None, tm, MEM_WIDTH), lambda b, i: (b, i, Z_C)),
                  pl.BlockSpec((None, n_mem, d), lambda b, i: (b, 0, 0)),
                  weight, weight],
        out_specs=pl.BlockSpec((None, tm, MEM_WIDTH), lambda b, i: (b, i, 0)),
        out_shape=jax.ShapeDtypeStruct((batch, seq, MEM_WIDTH), _BF16),
        scratch_shapes=[pltpu.VMEM((n_mem, MEM_WIDTH), _BF16), pltpu.VMEM((n_mem, MEM_WIDTH), _BF16)],
        compiler_params=pltpu.CompilerParams(
            dimension_semantics=("arbitrary", "arbitrary"), vmem_limit_bytes=VMEM_LIMIT_BYTES),
        name="memory_attention",
    )(proj3, proj3, mem, w_mem_k, w_mem_v)


MERGE_TM = 512
OUT_NORM_TM = 512
OUT_NORM_CHUNKS = (256, 128, 128)
MERGE_ROW_CHUNK = 256


def _branch_merge_kernel(ya_ref, yc_ref, u_ref, v_ref, z_ref, ws_ref, bt_ref, lvg_ref, lvb_ref, *rest):
    n_gate = N_BRANCH * D_MODEL // PROJ_TILE
    gate_refs = rest[:n_gate]
    wa_ref, wg_ref, wc_ref, o_ref, yg_ref = rest[n_gate:]
    per_branch = D_MODEL // PROJ_TILE
    col_tiles = [slice(c * PROJ_TILE, (c + 1) * PROJ_TILE) for c in range(per_branch)]
    chunks = [slice(r * MERGE_ROW_CHUNK, (r + 1) * MERGE_ROW_CHUNK)
              for r in range(o_ref.shape[0] // MERGE_ROW_CHUNK)]
    w_causal = _sgu_causal_weights(ws_ref)

    def branch_dots(rows):
        n_sub = len(col_tiles)
        sub_rows = (rows.stop - rows.start) // n_sub
        outer = []
        for c, cols in enumerate(col_tiles):
            outer.append((jnp.dot(ya_ref[rows, :], wa_ref[:, cols], preferred_element_type=_F32),
                          jnp.dot(yc_ref[rows, :], wc_ref[:, cols], preferred_element_type=_F32)))
            sub = slice(rows.start + c * sub_rows, rows.start + (c + 1) * sub_rows)
            _sgu_rows(u_ref, v_ref, z_ref, w_causal, bt_ref, lvg_ref, lvb_ref, sub, yg_ref)
        inner = [jnp.dot(yg_ref[rows, :], wg_ref[:, cols], preferred_element_type=_F32) for cols in col_tiles]
        return outer, inner

    dots = branch_dots(chunks[0])
    for k, rows in enumerate(chunks):
        next_dots = branch_dots(chunks[k + 1]) if k + 1 < len(chunks) else None
        outer, inner = dots
        for c, cols in enumerate(col_tiles):
            merged = (gate_refs[c][rows, :].astype(_F32) * outer[c][0]
                      + gate_refs[per_branch + c][rows, :].astype(_F32) * inner[c]
                      + gate_refs[2 * per_branch + c][rows, :].astype(_F32) * outer[c][1])
            o_ref[rows, cols] = merged.astype(o_ref.dtype)
        dots = next_dots


def _out_norm_kernel(m_ref, x_ref, wo_ref, lng_ref, lnb_ref, o_ref):
    bounds = [0]
    for size in OUT_NORM_CHUNKS:
        bounds.append(bounds[-1] + size)
    assert bounds[-1] == o_ref.shape[0]
    chunks = [slice(a, b) for a, b in zip(bounds[:-1], bounds[1:])]

    def out_dot(rows):
        return jnp.dot(m_ref[rows, :], wo_ref[...], preferred_element_type=_F32)

    y = out_dot(chunks[0])
    for k, rows in enumerate(chunks):
        y_next = out_dot(chunks[k + 1]) if k + 1 < len(chunks) else None
        h = DN_ALPHA * x_ref[rows, :] + y
        mu = jnp.mean(h, axis=-1, keepdims=True)
        d = h - mu
        var = jnp.mean(d * d, axis=-1, keepdims=True)
        o_ref[rows, :] = d * lax.rsqrt(var + LN_EPS) * lng_ref[...] + lnb_ref[...]
        y = y_next


def _merge_project_norm(ya, yc, proj, sgu_params, gates, x2, wa, wg, wc, wo, ln_g, ln_b, *, tm=MERGE_TM):
    t = x2.shape[0]
    w_s, b_s, ln_v_g, ln_v_b = sgu_params
    assert t % tm == 0 and tm % MERGE_ROW_CHUNK == 0 and t % OUT_NORM_TM == 0 and OUT_NORM_TM % MERGE_ROW_CHUNK == 0
    n_gate = N_BRANCH * D_MODEL // PROJ_TILE
    branch = pl.BlockSpec((tm, ATT_WIDTH), lambda i: (i, 0))
    gate = lambda c: pl.BlockSpec((tm, PROJ_TILE), lambda i: (i, c))
    rows = pl.BlockSpec((tm, D_MODEL), lambda i: (i, 0))
    resident = lambda shape: pl.BlockSpec(shape, lambda i: (0, 0), pipeline_mode=pl.Buffered(1))
    params = pltpu.CompilerParams(dimension_semantics=("arbitrary",), vmem_limit_bytes=VMEM_LIMIT_BYTES)
    section = lambda c: pl.BlockSpec((tm, PROJ_TILE), lambda i: (i, c))
    whole = lambda shape: pl.BlockSpec(shape, lambda i: (0,) * len(shape))
    merged = pl.pallas_call(
        _branch_merge_kernel,
        grid=(t // tm,),
        in_specs=[branch, branch, section(U_G), section(V_G), section(Z_G),
                  whole((SGU_GROUPS, SGU_CHUNK, SGU_CHUNK)), whole((SGU_CHUNK, SGU_GROUPS)),
                  whole((1, SGU_WIDTH)), whole((1, SGU_WIDTH)),
                  *[gate(c) for c in range(n_gate)],
                  resident((ATT_WIDTH, D_MODEL)), resident((SGU_WIDTH, D_MODEL)), resident((MEM_WIDTH, D_MODEL))],
        out_specs=rows,
        out_shape=jax.ShapeDtypeStruct((t, D_MODEL), _BF16),
        scratch_shapes=[pltpu.VMEM((tm, SGU_WIDTH), _BF16)],
        compiler_params=params,
        name="branch_merge",
    )(ya, yc, proj, proj, proj, w_s, b_s.T, ln_v_g.reshape(1, -1), ln_v_b.reshape(1, -1),
      *([gates] * n_gate), wa, wg, wc)
    out_rows = pl.BlockSpec((OUT_NORM_TM, D_MODEL), lambda i: (i, 0))
    return pl.pallas_call(
        _out_norm_kernel,
        grid=(t // OUT_NORM_TM,),
        in_specs=[out_rows, out_rows,
                  resident((D_MODEL, D_MODEL)), resident((1, D_MODEL)), resident((1, D_MODEL))],
        out_specs=out_rows,
        out_shape=jax.ShapeDtypeStruct((t, D_MODEL), _F32),
        compiler_params=params,
        name="out_proj_norm",
    )(merged, x2, wo, ln_g.reshape(1, -1), ln_b.reshape(1, -1))


def kernel(x, mem, w_in, w_mem_k, w_mem_v, w_s, b_s, ln_v_g, ln_v_b,
           w_branch_attn, w_branch_sgu, w_branch_mem, w_out, ln_g, ln_b):
    batch, seq, d = x.shape
    assert d == D_MODEL and seq % MOBA_BLOCK == 0 and mem.shape[1] == N_MEM
    t = batch * seq
    x2 = x.reshape(t, d)

    assert w_in.shape[1] == GATE_TILE0 * PROJ_TILE + N_BRANCH * D_MODEL and ATT_WIDTH == PROJ_TILE
    q_a, x_bf = _project(x2, w_in, tm=1024, tn=PROJ_TILE, name="in_proj_q", n_col_tiles=1,
                         scale=MOBA_Q_PRESCALE, row_chunks=4, emit_x_bf16=True)
    proj = _project(x_bf, w_in, tm=IN_PROJ_TM, tn=PROJ_TILE, name="in_proj_branches", first_col_tile=PROJ_TILE0,
                    n_col_tiles=GATE_TILE0 - PROJ_TILE0, row_chunks=IN_PROJ_ROW_CHUNKS)
    gates = _project(x_bf, w_in, tm=IN_PROJ_TM, tn=PROJ_TILE, name="in_proj_gates", first_col_tile=GATE_TILE0,
                     sigmoid=True, row_chunks=IN_PROJ_ROW_CHUNKS)

    ya, wa, wg, wc, wo = _moba_attention(q_a, proj, batch, seq,
                                         cast_jobs=(w_branch_attn, w_branch_sgu, w_branch_mem, w_out))
    ya = ya.reshape(t, ATT_WIDTH)
    yc = _memory_attention(proj, mem, w_mem_k, w_mem_v, seq).reshape(t, MEM_WIDTH)

    out = _merge_project_norm(ya, yc, proj, (w_s, b_s, ln_v_g, ln_v_b), gates, x2, wa, wg, wc, wo, ln_g, ln_b)
    return out.reshape(batch, seq, d)
```

```python
import functools

import jax
import jax.numpy as jnp
from jax import lax
from jax.experimental import pallas as pl
from jax.experimental.pallas import tpu as pltpu

D_MODEL = 2048
DEPTH = 1
N_MEM = 256
ATT_HEAD_DIM = 128
ATT_WIDTH = D_MODEL // 2
MOBA_BLOCK = 256
MOBA_TOPK = 3
SGU_WIDTH = D_MODEL // 2
SGU_CHUNK = 128
SGU_GROUP_DIM = 128
SGU_GROUPS = SGU_WIDTH // SGU_GROUP_DIM
MEM_HEADS = 4
MEM_WIDTH = D_MODEL // 2
MEM_HEAD_DIM = MEM_WIDTH // MEM_HEADS
N_BRANCH = 3
DN_ALPHA = (2 * DEPTH) ** 0.25
LN_EPS = 1e-5

PROJ_TILE = 1024
PROJ_TILE0 = 1
GATE_TILE0 = 9
PROJ_WIDTH = (GATE_TILE0 - PROJ_TILE0) * PROJ_TILE
K_A, V_A, Z_A, U_G, V_G, Z_G, Q_C, Z_C = range(8)

VMEM_LIMIT_BYTES = 56 * 1024 * 1024
IN_PROJ_TM = 2048
IN_PROJ_ROW_CHUNKS = 8
CAST_JOB_ROWS = 256

_NT = (((1,), (1,)), ((), ()))
_F32 = jnp.float32
_BF16 = jnp.bfloat16


_LOG2E = 1.4426950408889634


def _sigmoid(x):
    return 1.0 / (1.0 + jnp.exp2(x * -_LOG2E))


def _silu(x):
    return x * _sigmoid(x)


def _gelu_tanh(x):
    a = -2.0 * 0.7978845608028654 * _LOG2E
    return x / (1.0 + jnp.exp2(x * (a + (a * 0.044715) * (x * x))))


def _project_kernel(x_ref, w_ref, o_ref, *rest, scale, sigmoid, row_chunks, emit_x_bf16):
    x_bf_ref = rest[0] if emit_x_bf16 else None
    w_bf_ref = rest[-1]

    @pl.when(pl.program_id(1) == 0)
    def _cast_weight_tile():
        w_bf_ref[...] = w_ref[...].astype(_BF16)

    tc = x_ref.shape[0] // row_chunks
    for c in range(row_chunks):
        rows = slice(c * tc, (c + 1) * tc)
        xc = x_ref[rows, :]
        if emit_x_bf16:
            xc = xc.astype(_BF16)
            x_bf_ref[rows, :] = xc
        acc = jnp.dot(xc, w_bf_ref[...], preferred_element_type=_F32)
        if sigmoid:
            acc = _sigmoid(acc)
        if scale is not None:
            acc = acc * scale
        o_ref[rows, :] = acc.astype(o_ref.dtype)


def _project(x, w, *, tm, tn, name, first_col_tile=0, n_col_tiles=None,
             scale=None, sigmoid=False, row_chunks=1, emit_x_bf16=False):
    m, k = x.shape
    if n_col_tiles is None:
        n_col_tiles = w.shape[1] // tn - first_col_tile
    assert m % tm == 0 and tm % row_chunks == 0 and (first_col_tile + n_col_tiles) * tn <= w.shape[1]
    assert not emit_x_bf16 or n_col_tiles == 1
    out_specs = [pl.BlockSpec((tm, tn), lambda j, i: (i, j))]
    out_shape = [jax.ShapeDtypeStruct((m, n_col_tiles * tn), _BF16)]
    if emit_x_bf16:
        out_specs.append(pl.BlockSpec((tm, k), lambda j, i: (i, 0)))
        out_shape.append(jax.ShapeDtypeStruct((m, k), _BF16))
    outs = pl.pallas_call(
        functools.partial(_project_kernel, scale=scale, sigmoid=sigmoid, row_chunks=row_chunks,
                          emit_x_bf16=emit_x_bf16),
        grid=(n_col_tiles, m // tm),
        in_specs=[pl.BlockSpec((tm, k), lambda j, i: (i, 0)),
                  pl.BlockSpec((k, tn), lambda j, i: (0, first_col_tile + j))],
        out_specs=out_specs,
        out_shape=out_shape,
        scratch_shapes=[pltpu.VMEM((k, tn), _BF16)],
        compiler_params=pltpu.CompilerParams(
            dimension_semantics=("arbitrary", "arbitrary"), vmem_limit_bytes=VMEM_LIMIT_BYTES),
        name=name,
    )(x, w)
    return outs if emit_x_bf16 else outs[0]


def _cast_job_plan(arrays, n_steps, linear_step):
    specs, shapes, blocks, first = [], [], [], 0
    for a in arrays:
        assert a.shape[0] % CAST_JOB_ROWS == 0
        n_blocks = a.shape[0] // CAST_JOB_ROWS
        specs.append(pl.BlockSpec(
            (CAST_JOB_ROWS, a.shape[1]),
            lambda *idx, first=first, n_blocks=n_blocks: (jnp.clip(linear_step(*idx) - first, 0, n_blocks - 1), 0)))
        shapes.append(jax.ShapeDtypeStruct(a.shape, _BF16))
        blocks.append(n_blocks)
        first += n_blocks
    assert first <= n_steps
    return specs, shapes, tuple(blocks)


def _run_cast_jobs(step, job_in, job_out, blocks):
    first = 0
    for src, dst, n_blocks in zip(job_in, job_out, blocks):
        @pl.when((step >= first) & (step < first + n_blocks))
        def _cast_job_block(src=src, dst=dst):
            dst[...] = src[...].astype(_BF16)
        first += n_blocks


MOBA_HEADS_PER_STEP = 4
MOBA_Q_BLOCKS_PER_STEP = 2
_ACC_ROWS = ATT_HEAD_DIM + 16
MOBA_Q_PRESCALE = ATT_HEAD_DIM ** -0.5 * _LOG2E


def _moba_kernel(q_ref, k_ref, v_ref, z_ref, *rest, n_blocks, heads, cast_job_blocks):
    n_jobs = len(cast_job_blocks)
    job_in, o_ref, job_out = rest[:n_jobs], rest[n_jobs], rest[n_jobs + 1:2 * n_jobs + 1]
    vt_ref, kmean_ref, neg_ref, pos_ref, qt_ref, s_ref, acc_ref = rest[2 * n_jobs + 1:]
    blk = MOBA_BLOCK
    hd = ATT_HEAD_DIM
    step = (pl.program_id(0) * pl.num_programs(1) + pl.program_id(1)) * pl.num_programs(2) + pl.program_id(2)
    _run_cast_jobs(step, job_in, job_out, cast_job_blocks)

    @pl.when(pl.program_id(2) == 0)
    def _per_head_setup():
        for g in range(heads):
            cols = slice(g * hd, (g + 1) * hd)
            for j in range(n_blocks):
                rows = slice(j * blk, (j + 1) * blk)
                vt_ref[g, j, :hd, :] = v_ref[rows, cols].astype(_F32).T.astype(_BF16)
                vt_ref[g, j, hd:, :] = jnp.ones((_ACC_ROWS - hd, blk), _BF16)
                kmean_ref[g, j:j + 1, :] = (
                    jnp.sum(k_ref[rows, cols].astype(_F32), axis=0, keepdims=True) * (1.0 / blk))

    for sub in range(MOBA_Q_BLOCKS_PER_STEP):
        rows = slice(sub * blk, (sub + 1) * blk)
        _moba_query_block(pl.program_id(2) * MOBA_Q_BLOCKS_PER_STEP + sub, q_ref.at[rows], z_ref.at[rows],
                          o_ref.at[rows], k_ref, vt_ref, kmean_ref, neg_ref, pos_ref, qt_ref, s_ref, acc_ref,
                          n_blocks=n_blocks, heads=heads)


def _moba_query_block(qi, q_ref, z_ref, o_ref, k_ref, vt_ref, kmean_ref, neg_ref, pos_ref, qt_ref, s_ref, acc_ref,
                      *, n_blocks, heads):
    blk = MOBA_BLOCK
    hd = ATT_HEAD_DIM
    head_cols = [slice(g * hd, (g + 1) * hd) for g in range(heads)]
    qs = [q_ref[:, c] for c in head_cols]
    for g in range(heads):
        qt_ref[g] = qs[g].astype(_F32).T.astype(_BF16)

    def issue_scores(j, slot):
        start = j * blk if isinstance(j, int) else pl.multiple_of(j * blk, blk)
        for g in range(heads):
            s_ref[slot, g] = jnp.dot(k_ref[pl.ds(start, blk), head_cols[g]], qt_ref[g],
                                     preferred_element_type=_F32)

    sel_scores = []
    for g in range(heads):
        km = kmean_ref[g]
        km_hi = km.astype(_BF16)
        km_lo = (km - km_hi.astype(_F32)).astype(_BF16)
        sel_scores.append(jnp.dot(km_hi, qt_ref[g], preferred_element_type=_F32)
                          + jnp.dot(km_lo, qt_ref[g], preferred_element_type=_F32))
    issue_scores(qi, 2)
    row_id = lax.broadcasted_iota(jnp.int32, (n_blocks, blk), 0)
    past = row_id < qi
    later_rows = [jnp.where(row_id > j, 1.0, 0.0) for j in range(n_blocks)]
    for g in range(heads):
        sb = jnp.where(past, sel_scores[g], -jnp.inf)
        rank = jnp.zeros(sb.shape, _F32)
        for j in range(n_blocks):
            other = sb[j:j + 1, :]
            rank = rank + jnp.where(other > sb, 1.0, 0.0) + jnp.where(other == sb, later_rows[j], 0.0)
        selected = past & (rank < MOBA_TOPK)
        neg_ref[g] = jnp.where(selected, 0.0, -jnp.inf)
        pos_ref[g] = jnp.where(selected, -jnp.inf, jnp.inf)

    def accumulate(j, slot, ms, causal):
        probs, m_news = [], []
        for g in range(heads):
            s = s_ref[slot, g]
            if causal:
                key_pos = lax.broadcasted_iota(jnp.int32, s.shape, 0)
                q_pos = lax.broadcasted_iota(jnp.int32, s.shape, 1)
                s = jnp.where(key_pos <= q_pos, s, -jnp.inf)
                m_new = jnp.maximum(ms[g], jnp.max(s, axis=0, keepdims=True))
                m_sub = m_new
            else:
                m_new = jnp.maximum(ms[g], jnp.max(s, axis=0, keepdims=True) + neg_ref[g, pl.ds(j, 1), :])
                m_sub = jnp.maximum(m_new, pos_ref[g, pl.ds(j, 1), :])
            probs.append(jnp.exp2(s - m_sub).astype(_BF16))
            m_news.append(m_new)
        for g in range(heads):
            pv = jnp.dot(vt_ref[g, j], probs[g], preferred_element_type=_F32)
            if causal:
                acc_ref[g] = pv
            else:
                acc_ref[g] = jnp.exp2(ms[g] - m_news[g]) * acc_ref[g] + pv
        return m_news

    def past_block_pair(t, ms):
        issue_scores(2 * t + 1, 1)
        ms = accumulate(2 * t, 0, ms, causal=False)
        issue_scores(jnp.minimum(2 * t + 2, n_blocks - 1), 0)
        return accumulate(2 * t + 1, 1, ms, causal=False)

    issue_scores(0, 0)
    ms = accumulate(qi, 2, [jnp.full((1, blk), -jnp.inf, _F32)] * heads, causal=True)
    lax.fori_loop(0, (qi + 1) // 2, past_block_pair, ms)

    for g in range(heads):
        attn = (acc_ref[g, :hd, :] / acc_ref[g, hd:hd + 1, :]).T
        o_ref[:, head_cols[g]] = (attn * _silu(z_ref[:, head_cols[g]].astype(_F32))).astype(o_ref.dtype)


def _moba_attention(q, proj, batch, seq, *, heads=MOBA_HEADS_PER_STEP, cast_jobs=()):
    n_blocks = seq // MOBA_BLOCK
    q3 = q.reshape(batch, seq, ATT_WIDTH)
    proj3 = proj.reshape(batch, seq, PROJ_WIDTH)
    width = heads * ATT_HEAD_DIM
    cols = PROJ_TILE // width
    q_rows = MOBA_Q_BLOCKS_PER_STEP * MOBA_BLOCK
    assert n_blocks % MOBA_Q_BLOCKS_PER_STEP == 0
    grid = (batch, ATT_WIDTH // width, n_blocks // MOBA_Q_BLOCKS_PER_STEP)
    job_specs, job_shapes, job_blocks = _cast_job_plan(
        cast_jobs, grid[0] * grid[1] * grid[2], lambda b, h, i: (b * grid[1] + h) * grid[2] + i)
    return pl.pallas_call(
        functools.partial(_moba_kernel, n_blocks=n_blocks, heads=heads, cast_job_blocks=job_blocks),
        grid=grid,
        in_specs=[pl.BlockSpec((None, q_rows, width), lambda b, h, i: (b, i, h)),
                  pl.BlockSpec((None, seq, width), lambda b, h, i: (b, 0, K_A * cols + h)),
                  pl.BlockSpec((None, seq, width), lambda b, h, i: (b, 0, V_A * cols + h)),
                  pl.BlockSpec((None, q_rows, width), lambda b, h, i: (b, i, Z_A * cols + h)),
                  *job_specs],
        out_specs=[pl.BlockSpec((None, q_rows, width), lambda b, h, i: (b, i, h)), *job_specs],
        out_shape=[jax.ShapeDtypeStruct((batch, seq, ATT_WIDTH), _BF16), *job_shapes],
        scratch_shapes=[pltpu.VMEM((heads, n_blocks, _ACC_ROWS, MOBA_BLOCK), _BF16),
                        pltpu.VMEM((heads, n_blocks, ATT_HEAD_DIM), _F32),
                        pltpu.VMEM((heads, n_blocks, MOBA_BLOCK), _F32),
                        pltpu.VMEM((heads, n_blocks, MOBA_BLOCK), _F32),
                        pltpu.VMEM((heads, ATT_HEAD_DIM, MOBA_BLOCK), _BF16),
                        pltpu.VMEM((3, heads, MOBA_BLOCK, MOBA_BLOCK), _F32),
                        pltpu.VMEM((heads, _ACC_ROWS, MOBA_BLOCK), _F32)],
        compiler_params=pltpu.CompilerParams(
            dimension_semantics=("arbitrary", "arbitrary", "arbitrary"), vmem_limit_bytes=VMEM_LIMIT_BYTES),
        name="moba_attention",
    )(q3, proj3, proj3, proj3, *cast_jobs)


def _sgu_causal_weights(w_ref):
    t_pos = lax.broadcasted_iota(jnp.int32, (SGU_CHUNK, SGU_CHUNK), 0)
    s_pos = lax.broadcasted_iota(jnp.int32, (SGU_CHUNK, SGU_CHUNK), 1)
    return [jnp.where(s_pos <= t_pos, w_ref[g], 0.0).astype(_BF16) for g in range(SGU_GROUPS)]


def _sgu_rows(u_ref, v_ref, z_ref, w_causal, bt_ref, g_ref, b_ref, rows, yg_ref):
    u = _gelu_tanh(u_ref[rows, :].astype(_F32))
    v = _gelu_tanh(v_ref[rows, :].astype(_F32))
    mu = jnp.mean(v, axis=-1, keepdims=True)
    d = v - mu
    var = jnp.mean(d * d, axis=-1, keepdims=True)
    vn = (d * lax.rsqrt(var + LN_EPS) * g_ref[...] + b_ref[...]).astype(_BF16)
    gate = u * _silu(z_ref[rows, :].astype(_F32))
    for g in range(SGU_GROUPS):
        bias = bt_ref[:, g:g + 1]
        cols = slice(g * SGU_GROUP_DIM, (g + 1) * SGU_GROUP_DIM)
        for c in range((rows.stop - rows.start) // SGU_CHUNK):
            sub = slice(c * SGU_CHUNK, (c + 1) * SGU_CHUNK)
            mixed = jnp.dot(w_causal[g], vn[sub, cols], preferred_element_type=_F32) + bias
            yg_ref[rows.start + c * SGU_CHUNK:rows.start + (c + 1) * SGU_CHUNK, cols] = (
                gate[sub, cols] * mixed).astype(yg_ref.dtype)


def _mem_attn_kernel(q_ref, z_ref, mem_ref, wk_ref, wv_ref, o_ref, k_ref, v_ref):
    @pl.when(pl.program_id(1) == 0)
    def _project_memory():
        mem_bf = mem_ref[...].astype(_BF16)
        for w_ref, dst in ((wk_ref, k_ref), (wv_ref, v_ref)):
            for c in range(0, MEM_WIDTH, MEM_HEAD_DIM):
                dst[:, c:c + MEM_HEAD_DIM] = jnp.dot(
                    mem_bf, w_ref[:, c:c + MEM_HEAD_DIM].astype(_BF16), preferred_element_type=_F32
                ).astype(_BF16)

    scale_log2e = MEM_HEAD_DIM ** -0.5 * _LOG2E
    for h in range(MEM_HEADS):
        cols = slice(h * MEM_HEAD_DIM, (h + 1) * MEM_HEAD_DIM)
        s = lax.dot_general(q_ref[:, cols], k_ref[:, cols], _NT, preferred_element_type=_F32)
        p = jnp.exp2((s - jnp.max(s, axis=-1, keepdims=True)) * scale_log2e)
        l = jnp.sum(p, axis=-1, keepdims=True)
        o = jnp.dot(p.astype(_BF16), v_ref[:, cols], preferred_element_type=_F32) / l
        o_ref[:, cols] = (o * _silu(z_ref[:, cols].astype(_F32))).astype(o_ref.dtype)


def _memory_attention(proj, mem, w_mem_k, w_mem_v, seq, *, tm=1024):
    batch, n_mem, d = mem.shape
    proj3 = proj.reshape(batch, seq, PROJ_WIDTH)
    weight = pl.BlockSpec((d, MEM_WIDTH), lambda b, i: (0, 0), pipeline_mode=pl.Buffered(1))
    return pl.pallas_call(
        _mem_attn_kernel,
        grid=(batch, seq // tm),
        in_specs=[pl.BlockSpec((None, tm, MEM_WIDTH), lambda b, i: (b, i, Q_C)),
                  pl.BlockSpec((None, tm, MEM_WIDTH), lambda b, i: (b, i, Z_C)),
                  pl.BlockSpec((None, n_mem, d), lambda b, i: (b, 0, 0)),
                  weight, weight],
        out_specs=pl.BlockSpec((None, tm, MEM_WIDTH), lambda b, i: (b, i, 0)),
        out_shape=jax.ShapeDtypeStruct((batch, seq, MEM_WIDTH), _BF16),
        scratch_shapes=[pltpu.VMEM((n_mem, MEM_WIDTH), _BF16), pltpu.VMEM((n_mem, MEM_WIDTH), _BF16)],
        compiler_params=pltpu.CompilerParams(
            dimension_semantics=("arbitrary", "arbitrary"), vmem_limit_bytes=VMEM_LIMIT_BYTES),
        name="memory_attention",
    )(proj3, proj3, mem, w_mem_k, w_mem_v)


MERGE_TM = 512
OUT_NORM_TM = 512
OUT_NORM_CHUNKS = (256, 128, 128)
MERGE_ROW_CHUNK = 256


def _branch_merge_kernel(ya_ref, yc_ref, u_ref, v_ref, z_ref, ws_ref, bt_ref, lvg_ref, lvb_ref, *rest):
    n_gate = N_BRANCH * D_MODEL // PROJ_TILE
    gate_refs = rest[:n_gate]
    wa_ref, wg_ref, wc_ref, o_ref, yg_ref = rest[n_gate:]
    per_branch = D_MODEL // PROJ_TILE
    col_tiles = [slice(c * PROJ_TILE, (c + 1) * PROJ_TILE) for c in range(per_branch)]
    chunks = [slice(r * MERGE_ROW_CHUNK, (r + 1) * MERGE_ROW_CHUNK)
              for r in range(o_ref.shape[0] // MERGE_ROW_CHUNK)]
    w_causal = _sgu_causal_weights(ws_ref)

    def branch_dots(rows):
        n_sub = len(col_tiles)
        sub_rows = (rows.stop - rows.start) // n_sub
        outer = []
        for c, cols in enumerate(col_tiles):
            outer.append((jnp.dot(ya_ref[rows, :], wa_ref[:, cols], preferred_element_type=_F32),
                          jnp.dot(yc_ref[rows, :], wc_ref[:, cols], preferred_element_type=_F32)))
            sub = slice(rows.start + c * sub_rows, rows.start + (c + 1) * sub_rows)
            _sgu_rows(u_ref, v_ref, z_ref, w_causal, bt_ref, lvg_ref, lvb_ref, sub, yg_ref)
        inner = [jnp.dot(yg_ref[rows, :], wg_ref[:, cols], preferred_element_type=_F32) for cols in col_tiles]
        return outer, inner

    dots = branch_dots(chunks[0])
    for k, rows in enumerate(chunks):
        next_dots = branch_dots(chunks[k + 1]) if k + 1 < len(chunks) else None
        outer, inner = dots
        for c, cols in enumerate(col_tiles):
            merged = (gate_refs[c][rows, :].astype(_F32) * outer[c][0]
                      + gate_refs[per_branch + c][rows, :].astype(_F32) * inner[c]
                      + gate_refs[2 * per_branch + c][rows, :].astype(_F32) * outer[c][1])
            o_ref[rows, cols] = merged.astype(o_ref.dtype)
        dots = next_dots


def _out_norm_kernel(m_ref, x_ref, wo_ref, lng_ref, lnb_ref, o_ref):
    bounds = [0]
    for size in OUT_NORM_CHUNKS:
        bounds.append(bounds[-1] + size)
    assert bounds[-1] == o_ref.shape[0]
    chunks = [slice(a, b) for a, b in zip(bounds[:-1], bounds[1:])]

    def out_dot(rows):
        return jnp.dot(m_ref[rows, :], wo_ref[...], preferred_element_type=_F32)

    y = out_dot(chunks[0])
    for k, rows in enumerate(chunks):
        y_next = out_dot(chunks[k + 1]) if k + 1 < len(chunks) else None
        h = DN_ALPHA * x_ref[rows, :] + y
        mu = jnp.mean(h, axis=-1, keepdims=True)
        d = h - mu
        var = jnp.mean(d * d, axis=-1, keepdims=True)
        o_ref[rows, :] = d * lax.rsqrt(var + LN_EPS) * lng_ref[...] + lnb_ref[...]
        y = y_next


def _merge_project_norm(ya, yc, proj, sgu_params, gates, x2, wa, wg, wc, wo, ln_g, ln_b, *, tm=MERGE_TM):
    t = x2.shape[0]
    w_s, b_s, ln_v_g, ln_v_b = sgu_params
    assert t % tm == 0 and tm % MERGE_ROW_CHUNK == 0 and t % OUT_NORM_TM == 0 and OUT_NORM_TM % MERGE_ROW_CHUNK == 0
    n_gate = N_BRANCH * D_MODEL // PROJ_TILE
    branch = pl.BlockSpec((tm, ATT_WIDTH), lambda i: (i, 0))
    gate = lambda c: pl.BlockSpec((tm, PROJ_TILE), lambda i: (i, c))
    rows = pl.BlockSpec((tm, D_MODEL), lambda i: (i, 0))
    resident = lambda shape: pl.BlockSpec(shape, lambda i: (0, 0), pipeline_mode=pl.Buffered(1))
    params = pltpu.CompilerParams(dimension_semantics=("arbitrary",), vmem_limit_bytes=VMEM_LIMIT_BYTES)
    section = lambda c: pl.BlockSpec((tm, PROJ_TILE), lambda i: (i, c))
    whole = lambda shape: pl.BlockSpec(shape, lambda i: (0,) * len(shape))
    merged = pl.pallas_call(
        _branch_merge_kernel,
        grid=(t // tm,),
        in_specs=[branch, branch, section(U_G), section(V_G), section(Z_G),
                  whole((SGU_GROUPS, SGU_CHUNK, SGU_CHUNK)), whole((SGU_CHUNK, SGU_GROUPS)),
                  whole((1, SGU_WIDTH)), whole((1, SGU_WIDTH)),
                  *[gate(c) for c in range(n_gate)],
                  resident((ATT_WIDTH, D_MODEL)), resident((SGU_WIDTH, D_MODEL)), resident((MEM_WIDTH, D_MODEL))],
        out_specs=rows,
        out_shape=jax.ShapeDtypeStruct((t, D_MODEL), _BF16),
        scratch_shapes=[pltpu.VMEM((tm, SGU_WIDTH), _BF16)],
        compiler_params=params,
        name="branch_merge",
    )(ya, yc, proj, proj, proj, w_s, b_s.T, ln_v_g.reshape(1, -1), ln_v_b.reshape(1, -1),
      *([gates] * n_gate), wa, wg, wc)
    out_rows = pl.BlockSpec((OUT_NORM_TM, D_MODEL), lambda i: (i, 0))
    return pl.pallas_call(
        _out_norm_kernel,
        grid=(t // OUT_NORM_TM,),
        in_specs=[out_rows, out_rows,
                  resident((D_MODEL, D_MODEL)), resident((1, D_MODEL)), resident((1, D_MODEL))],
        out_specs=out_rows,
        out_shape=jax.ShapeDtypeStruct((t, D_MODEL), _F32),
        compiler_params=params,
        name="out_proj_norm",
    )(merged, x2, wo, ln_g.reshape(1, -1), ln_b.reshape(1, -1))


def kernel(x, mem, w_in, w_mem_k, w_mem_v, w_s, b_s, ln_v_g, ln_v_b,
           w_branch_attn, w_branch_sgu, w_branch_mem, w_out, ln_g, ln_b):
    batch, seq, d = x.shape
    assert d == D_MODEL and seq % MOBA_BLOCK == 0 and mem.shape[1] == N_MEM
    t = batch * seq
    x2 = x.reshape(t, d)

    assert w_in.shape[1] == GATE_TILE0 * PROJ_TILE + N_BRANCH * D_MODEL and ATT_WIDTH == PROJ_TILE
    q_a, x_bf = _project(x2, w_in, tm=1024, tn=PROJ_TILE, name="in_proj_q", n_col_tiles=1,
                         scale=MOBA_Q_PRESCALE, row_chunks=4, emit_x_bf16=True)
    proj = _project(x_bf, w_in, tm=IN_PROJ_TM, tn=PROJ_TILE, name="in_proj_branches", first_col_tile=PROJ_TILE0,
                    n_col_tiles=GATE_TILE0 - PROJ_TILE0, row_chunks=IN_PROJ_ROW_CHUNKS)
    gates = _project(x_bf, w_in, tm=IN_PROJ_TM, tn=PROJ_TILE, name="in_proj_gates", first_col_tile=GATE_TILE0,
                     sigmoid=True, row_chunks=IN_PROJ_ROW_CHUNKS)

    ya, wa, wg, wc, wo = _moba_attention(q_a, proj, batch, seq,
                                         cast_jobs=(w_branch_attn, w_branch_sgu, w_branch_mem, w_out))
    ya = ya.reshape(t, ATT_WIDTH)
    yc = _memory_attention(proj, mem, w_mem_k, w_mem_v, seq).reshape(t, MEM_WIDTH)

    out = _merge_project_norm(ya, yc, proj, (w_s, b_s, ln_v_g, ln_v_b), gates, x2, wa, wg, wc, wo, ln_g, ln_b)
    return out.reshape(batch, seq, d)
```

```python
import functools

import jax
import jax.numpy as jnp
from jax import lax
from jax.experimental import pallas as pl
from jax.experimental.pallas import tpu as pltpu

D_MODEL = 2048
DEPTH = 1
N_MEM = 256
ATT_HEAD_DIM = 128
ATT_WIDTH = D_MODEL // 2
MOBA_BLOCK = 256
MOBA_TOPK = 3
SGU_WIDTH = D_MODEL // 2
SGU_CHUNK = 128
SGU_GROUP_DIM = 128
SGU_GROUPS = SGU_WIDTH // SGU_GROUP_DIM
MEM_HEADS = 4
MEM_WIDTH = D_MODEL // 2
MEM_HEAD_DIM = MEM_WIDTH // MEM_HEADS
N_BRANCH = 3
DN_ALPHA = (2 * DEPTH) ** 0.25
LN_EPS = 1e-5

PROJ_TILE = 1024
PROJ_TILE0 = 1
GATE_TILE0 = 9
PROJ_WIDTH = (GATE_TILE0 - PROJ_TILE0) * PROJ_TILE
K_A, V_A, Z_A, U_G, V_G, Z_G, Q_C, Z_C = range(8)

VMEM_LIMIT_BYTES = 56 * 1024 * 1024
IN_PROJ_TM = 2048
IN_PROJ_ROW_CHUNKS = 8
CAST_JOB_ROWS = 256

_NT = (((1,), (1,)), ((), ()))
_F32 = jnp.float32
_BF16 = jnp.bfloat16


_LOG2E = 1.4426950408889634


def _sigmoid(x):
    return 1.0 / (1.0 + jnp.exp2(x * -_LOG2E))


def _silu(x):
    return x * _sigmoid(x)


def _gelu_tanh(x):
    a = -2.0 * 0.7978845608028654 * _LOG2E
    return x / (1.0 + jnp.exp2(x * (a + (a * 0.044715) * (x * x))))


def _project_kernel(x_ref, w_ref, o_ref, *rest, scale, sigmoid, row_chunks, emit_x_bf16):
    x_bf_ref = rest[0] if emit_x_bf16 else None
    w_bf_ref = rest[-1]

    @pl.when(pl.program_id(1) == 0)
    def _cast_weight_tile():
        w_bf_ref[...] = w_ref[...].astype(_BF16)

    tc = x_ref.shape[0] // row_chunks
    for c in range(row_chunks):
        rows = slice(c * tc, (c + 1) * tc)
        xc = x_ref[rows, :]
        if emit_x_bf16:
            xc = xc.astype(_BF16)
            x_bf_ref[rows, :] = xc
        acc = jnp.dot(xc, w_bf_ref[...], preferred_element_type=_F32)
        if sigmoid:
            acc = _sigmoid(acc)
        if scale is not None:
            acc = acc * scale
        o_ref[rows, :] = acc.astype(o_ref.dtype)


def _project(x, w, *, tm, tn, name, first_col_tile=0, n_col_tiles=None,
             scale=None, sigmoid=False, row_chunks=1, emit_x_bf16=False):
    m, k = x.shape
    if n_col_tiles is None:
        n_col_tiles = w.shape[1] // tn - first_col_tile
    assert m % tm == 0 and tm % row_chunks == 0 and (first_col_tile + n_col_tiles) * tn <= w.shape[1]
    assert not emit_x_bf16 or n_col_tiles == 1
    out_specs = [pl.BlockSpec((tm, tn), lambda j, i: (i, j))]
    out_shape = [jax.ShapeDtypeStruct((m, n_col_tiles * tn), _BF16)]
    if emit_x_bf16:
        out_specs.append(pl.BlockSpec((tm, k), lambda j, i: (i, 0)))
        out_shape.append(jax.ShapeDtypeStruct((m, k), _BF16))
    outs = pl.pallas_call(
        functools.partial(_project_kernel, scale=scale, sigmoid=sigmoid, row_chunks=row_chunks,
                          emit_x_bf16=emit_x_bf16),
        grid=(n_col_tiles, m // tm),
        in_specs=[pl.BlockSpec((tm, k), lambda j, i: (i, 0)),
                  pl.BlockSpec((k, tn), lambda j, i: (0, first_col_tile + j))],
        out_specs=out_specs,
        out_shape=out_shape,
        scratch_shapes=[pltpu.VMEM((k, tn), _BF16)],
        compiler_params=pltpu.CompilerParams(
            dimension_semantics=("arbitrary", "arbitrary"), vmem_limit_bytes=VMEM_LIMIT_BYTES),
        name=name,
    )(x, w)
    return outs if emit_x_bf16 else outs[0]


def _cast_job_plan(arrays, n_steps, linear_step):
    specs, shapes, blocks, first = [], [], [], 0
    for a in arrays:
        assert a.shape[0] % CAST_JOB_ROWS == 0
        n_blocks = a.shape[0] // CAST_JOB_ROWS
        specs.append(pl.BlockSpec(
            (CAST_JOB_ROWS, a.shape[1]),
            lambda *idx, first=first, n_blocks=n_blocks: (jnp.clip(linear_step(*idx) - first, 0, n_blocks - 1), 0)))
        shapes.append(jax.ShapeDtypeStruct(a.shape, _BF16))
        blocks.append(n_blocks)
        first += n_blocks
    assert first <= n_steps
    return specs, shapes, tuple(blocks)


def _run_cast_jobs(step, job_in, job_out, blocks):
    first = 0
    for src, dst, n_blocks in zip(job_in, job_out, blocks):
        @pl.when((step >= first) & (step < first + n_blocks))
        def _cast_job_block(src=src, dst=dst):
            dst[...] = src[...].astype(_BF16)
        first += n_blocks


MOBA_HEADS_PER_STEP = 4
MOBA_Q_BLOCKS_PER_STEP = 2
_ACC_ROWS = ATT_HEAD_DIM + 16
MOBA_Q_PRESCALE = ATT_HEAD_DIM ** -0.5 * _LOG2E


def _moba_kernel(q_ref, k_ref, v_ref, z_ref, *rest, n_blocks, heads, cast_job_blocks):
    n_jobs = len(cast_job_blocks)
    job_in, o_ref, job_out = rest[:n_jobs], rest[n_jobs], rest[n_jobs + 1:2 * n_jobs + 1]
    vt_ref, kmean_ref, neg_ref, pos_ref, qt_ref, s_ref, acc_ref = rest[2 * n_jobs + 1:]
    blk = MOBA_BLOCK
    hd = ATT_HEAD_DIM
    step = (pl.program_id(0) * pl.num_programs(1) + pl.program_id(1)) * pl.num_programs(2) + pl.program_id(2)
    _run_cast_jobs(step, job_in, job_out, cast_job_blocks)

    @pl.when(pl.program_id(2) == 0)
    def _per_head_setup():
        for g in range(heads):
            cols = slice(g * hd, (g + 1) * hd)
            for j in range(n_blocks):
                rows = slice(j * blk, (j + 1) * blk)
                vt_ref[g, j, :hd, :] = v_ref[rows, cols].T
                vt_ref[g, j, hd:, :] = jnp.ones((_ACC_ROWS - hd, blk), _BF16)
                kmean_ref[g, j:j + 1, :] = (
                    jnp.sum(k_ref[rows, cols].astype(_F32), axis=0, keepdims=True) * (1.0 / blk))

    for sub in range(MOBA_Q_BLOCKS_PER_STEP):
        rows = slice(sub * blk, (sub + 1) * blk)
        _moba_query_block(pl.program_id(2) * MOBA_Q_BLOCKS_PER_STEP + sub, q_ref.at[rows], z_ref.at[rows],
                          o_ref.at[rows], k_ref, vt_ref, kmean_ref, neg_ref, pos_ref, qt_ref, s_ref, acc_ref,
                          n_blocks=n_blocks, heads=heads)


def _moba_query_block(qi, q_ref, z_ref, o_ref, k_ref, vt_ref, kmean_ref, neg_ref, pos_ref, qt_ref, s_ref, acc_ref,
                      *, n_blocks, heads):
    blk = MOBA_BLOCK
    hd = ATT_HEAD_DIM
    head_cols = [slice(g * hd, (g + 1) * hd) for g in range(heads)]
    qs = [q_ref[:, c] for c in head_cols]
    for g in range(heads):
        qt_ref[g] = qs[g].T

    def issue_scores(j, slot):
        start = j * blk if isinstance(j, int) else pl.multiple_of(j * blk, blk)
        for g in range(heads):
            s_ref[slot, g] = jnp.dot(k_ref[pl.ds(start, blk), head_cols[g]], qt_ref[g],
                                     preferred_element_type=_F32)

    sel_scores = []
    for g in range(heads):
        km = kmean_ref[g]
        km_hi = km.astype(_BF16)
        km_lo = (km - km_hi.astype(_F32)).astype(_BF16)
        sel_scores.append(jnp.dot(km_hi, qt_ref[g], preferred_element_type=_F32)
                          + jnp.dot(km_lo, qt_ref[g], preferred_element_type=_F32))
    issue_scores(qi, 2)
    row_id = lax.broadcasted_iota(jnp.int32, (n_blocks, blk), 0)
    past = row_id < qi
    later_rows = [jnp.where(row_id > j, 1.0, 0.0) for j in range(n_blocks)]
    for g in range(heads):
        sb = jnp.where(past, sel_scores[g], -jnp.inf)
        rank = jnp.zeros(sb.shape, _F32)
        for j in range(n_blocks):
            other = sb[j:j + 1, :]
            rank = rank + jnp.where(other > sb, 1.0, 0.0) + jnp.where(other == sb, later_rows[j], 0.0)
        selected = past & (rank < MOBA_TOPK)
        neg_ref[g] = jnp.where(selected, 0.0, -jnp.inf)
        pos_ref[g] = jnp.where(selected, -jnp.inf, jnp.inf)

    def accumulate(j, slot, ms, causal):
        probs, m_news = [], []
        for g in range(heads):
            s = s_ref[slot, g]
            if causal:
                key_pos = lax.broadcasted_iota(jnp.int32, s.shape, 0)
                q_pos = lax.broadcasted_iota(jnp.int32, s.shape, 1)
                s = jnp.where(key_pos <= q_pos, s, -jnp.inf)
                m_new = jnp.maximum(ms[g], jnp.max(s, axis=0, keepdims=True))
                m_sub = m_new
            else:
                m_new = jnp.maximum(ms[g], jnp.max(s, axis=0, keepdims=True) + neg_ref[g, pl.ds(j, 1), :])
                m_sub = jnp.maximum(m_new, pos_ref[g, pl.ds(j, 1), :])
            probs.append(jnp.exp2(s - m_sub).astype(_BF16))
            m_news.append(m_new)
        for g in range(heads):
            pv = jnp.dot(vt_ref[g, j], probs[g], preferred_element_type=_F32)
            if causal:
                acc_ref[g] = pv
            else:
                acc_ref[g] = jnp.exp2(ms[g] - m_news[g]) * acc_ref[g] + pv
        return m_news

    def past_block_pair(t, ms):
        issue_scores(2 * t + 1, 1)
        ms = accumulate(2 * t, 0, ms, causal=False)
        issue_scores(jnp.minimum(2 * t + 2, n_blocks - 1), 0)
        return accumulate(2 * t + 1, 1, ms, causal=False)

    issue_scores(0, 0)
    ms = accumulate(qi, 2, [jnp.full((1, blk), -jnp.inf, _F32)] * heads, causal=True)
    lax.fori_loop(0, (qi + 1) // 2, past_block_pair, ms)

    for g in range(heads):
        attn = (acc_ref[g, :hd, :] / acc_ref[g, hd:hd + 1, :]).T
        o_ref[:, head_cols[g]] = (attn * _silu(z_ref[:, head_cols[g]].astype(_F32))).astype(o_ref.dtype)


def _moba_attention(q, proj, batch, seq, *, heads=MOBA_HEADS_PER_STEP, cast_jobs=()):
    n_blocks = seq // MOBA_BLOCK
    q3 = q.reshape(batch, seq, ATT_WIDTH)
    proj3 = proj.reshape(batch, seq, PROJ_WIDTH)
    width = heads * ATT_HEAD_DIM
    cols = PROJ_TILE // width
    q_rows = MOBA_Q_BLOCKS_PER_STEP * MOBA_BLOCK
    assert n_blocks % MOBA_Q_BLOCKS_PER_STEP == 0
    grid = (batch, ATT_WIDTH // width, n_blocks // MOBA_Q_BLOCKS_PER_STEP)
    job_specs, job_shapes, job_blocks = _cast_job_plan(
        cast_jobs, grid[0] * grid[1] * grid[2], lambda b, h, i: (b * grid[1] + h) * grid[2] + i)
    return pl.pallas_call(
        functools.partial(_moba_kernel, n_blocks=n_blocks, heads=heads, cast_job_blocks=job_blocks),
        grid=grid,
        in_specs=[pl.BlockSpec((None, q_rows, width), lambda b, h, i: (b, i, h)),
                  pl.BlockSpec((None, seq, width), lambda b, h, i: (b, 0, K_A * cols + h)),
                  pl.BlockSpec((None, seq, width), lambda b, h, i: (b, 0, V_A * cols + h)),
                  pl.BlockSpec((None, q_rows, width), lambda b, h, i: (b, i, Z_A * cols + h)),
                  *job_specs],
        out_specs=[pl.BlockSpec((None, q_rows, width), lambda b, h, i: (b, i, h)), *job_specs],
        out_shape=[jax.ShapeDtypeStruct((batch, seq, ATT_WIDTH), _BF16), *job_shapes],
        scratch_shapes=[pltpu.VMEM((heads, n_blocks, _ACC_ROWS, MOBA_BLOCK), _BF16),
                        pltpu.VMEM((heads, n_blocks, ATT_HEAD_DIM), _F32),
                        pltpu.VMEM((heads, n_blocks, MOBA_BLOCK), _F32),
                        pltpu.VMEM((heads, n_blocks, MOBA_BLOCK), _F32),
                        pltpu.VMEM((heads, ATT_HEAD_DIM, MOBA_BLOCK), _BF16),
                        pltpu.VMEM((3, heads, MOBA_BLOCK, MOBA_BLOCK), _F32),
                        pltpu.VMEM((heads, _ACC_ROWS, MOBA_BLOCK), _F32)],
        compiler_params=pltpu.CompilerParams(
            dimension_semantics=("arbitrary", "arbitrary", "arbitrary"), vmem_limit_bytes=VMEM_LIMIT_BYTES),
        name="moba_attention",
    )(q3, proj3, proj3, proj3, *cast_jobs)


def _sgu_causal_weights(w_ref):
    t_pos = lax.broadcasted_iota(jnp.int32, (SGU_CHUNK, SGU_CHUNK), 0)
    s_pos = lax.broadcasted_iota(jnp.int32, (SGU_CHUNK, SGU_CHUNK), 1)
    return [jnp.where(s_pos <= t_pos, w_ref[g], 0.0).astype(_BF16) for g in range(SGU_GROUPS)]


def _sgu_rows(u_ref, v_ref, z_ref, w_causal, bt_ref, g_ref, b_ref, rows, yg_ref):
    u = _gelu_tanh(u_ref[rows, :].astype(_F32))
    v = _gelu_tanh(v_ref[rows, :].astype(_F32))
    mu = jnp.mean(v, axis=-1, keepdims=True)
    d = v - mu
    var = jnp.mean(d * d, axis=-1, keepdims=True)
    vn = (d * lax.rsqrt(var + LN_EPS) * g_ref[...] + b_ref[...]).astype(_BF16)
    gate = u * _silu(z_ref[rows, :].astype(_F32))
    for g in range(SGU_GROUPS):
        bias = bt_ref[:, g:g + 1]
        cols = slice(g * SGU_GROUP_DIM, (g + 1) * SGU_GROUP_DIM)
        for c in range((rows.stop - rows.start) // SGU_CHUNK):
            sub = slice(c * SGU_CHUNK, (c + 1) * SGU_CHUNK)
            mixed = jnp.dot(w_causal[g], vn[sub, cols], preferred_element_type=_F32) + bias
            yg_ref[rows.start + c * SGU_CHUNK:rows.start + (c + 1) * SGU_CHUNK, cols] = (
                gate[sub, cols] * mixed).astype(yg_ref.dtype)


def _mem_attn_kernel(q_ref, z_ref, mem_ref, wk_ref, wv_ref, o_ref, k_ref, v_ref):
    @pl.when(pl.program_id(1) == 0)
    def _project_memory():
        mem_bf = mem_ref[...].astype(_BF16)
        for w_ref, dst in ((wk_ref, k_ref), (wv_ref, v_ref)):
            for c in range(0, MEM_WIDTH, MEM_HEAD_DIM):
                dst[:, c:c + MEM_HEAD_DIM] = jnp.dot(
                    mem_bf, w_ref[:, c:c + MEM_HEAD_DIM].astype(_BF16), preferred_element_type=_F32
                ).astype(_BF16)

    scale_log2e = MEM_HEAD_DIM ** -0.5 * _LOG2E
    for h in range(MEM_HEADS):
        cols = slice(h * MEM_HEAD_DIM, (h + 1) * MEM_HEAD_DIM)
        s = lax.dot_general(q_ref[:, cols], k_ref[:, cols], _NT, preferred_element_type=_F32)
        p = jnp.exp2((s - jnp.max(s, axis=-1, keepdims=True)) * scale_log2e)
        l = jnp.sum(p, axis=-1, keepdims=True)
        o = jnp.dot(p.astype(_BF16), v_ref[:, cols], preferred_element_type=_F32) / l
        o_ref[:, cols] = (o * _silu(z_ref[:, cols].astype(_F32))).astype(o_ref.dtype)


def _memory_attention(proj, mem, w_mem_k, w_mem_v, seq, *, tm=1024):
    batch, n_mem, d = mem.shape
    proj3 = proj.reshape(batch, seq, PROJ_WIDTH)
    weight = pl.BlockSpec((d, MEM_WIDTH), lambda b, i: (0, 0), pipeline_mode=pl.Buffered(1))
    return pl.pallas_call(
        _mem_attn_kernel,
        grid=(batch, seq // tm),
        in_specs=[pl.BlockSpec((None, tm, MEM_WIDTH), lambda b, i: (b, i, Q_C)),
                  pl.BlockSpec((None, tm, MEM_WIDTH), lambda b, i: (b, i, Z_C)),
                  pl.BlockSpec((None, n_mem, d), lambda b, i: (b, 0, 0)),
                  weight, weight],
        out_specs=pl.BlockSpec((None, tm, MEM_WIDTH), lambda b, i: (b, i, 0)),
        out_shape=jax.ShapeDtypeStruct((batch, seq, MEM_WIDTH), _BF16),
        scratch_shapes=[pltpu.VMEM((n_mem, MEM_WIDTH), _BF16), pltpu.VMEM((n_mem, MEM_WIDTH), _BF16)],
        compiler_params=pltpu.CompilerParams(
            dimension_semantics=("arbitrary", "arbitrary"), vmem_limit_bytes=VMEM_LIMIT_BYTES),
        name="memory_attention",
    )(proj3, proj3, mem, w_mem_k, w_mem_v)


MERGE_TM = 512
OUT_NORM_TM = 512
OUT_NORM_CHUNKS = (256, 128, 128)
MERGE_ROW_CHUNK = 256


def _branch_merge_kernel(ya_ref, yc_ref, u_ref, v_ref, z_ref, ws_ref, bt_ref, lvg_ref, lvb_ref, *rest):
    n_gate = N_BRANCH * D_MODEL // PROJ_TILE
    gate_refs = rest[:n_gate]
    wa_ref, wg_ref, wc_ref, o_ref, yg_ref = rest[n_gate:]
    per_branch = D_MODEL // PROJ_TILE
    col_tiles = [slice(c * PROJ_TILE, (c + 1) * PROJ_TILE) for c in range(per_branch)]
    chunks = [slice(r * MERGE_ROW_CHUNK, (r + 1) * MERGE_ROW_CHUNK)
              for r in range(o_ref.shape[0] // MERGE_ROW_CHUNK)]
    w_causal = _sgu_causal_weights(ws_ref)

    def branch_dots(rows):
        n_sub = len(col_tiles)
        sub_rows = (rows.stop - rows.start) // n_sub
        outer = []
        for c, cols in enumerate(col_tiles):
            outer.append((jnp.dot(ya_ref[rows, :], wa_ref[:, cols], preferred_element_type=_F32),
                          jnp.dot(yc_ref[rows, :], wc_ref[:, cols], preferred_element_type=_F32)))
            sub = slice(rows.start + c * sub_rows, rows.start + (c + 1) * sub_rows)
            _sgu_rows(u_ref, v_ref, z_ref, w_causal, bt_ref, lvg_ref, lvb_ref, sub, yg_ref)
        inner = [jnp.dot(yg_ref[rows, :], wg_ref[:, cols], preferred_element_type=_F32) for cols in col_tiles]
        return outer, inner

    dots = branch_dots(chunks[0])
    for k, rows in enumerate(chunks):
        next_dots = branch_dots(chunks[k + 1]) if k + 1 < len(chunks) else None
        outer, inner = dots
        for c, cols in enumerate(col_tiles):
            merged = (gate_refs[c][rows, :].astype(_F32) * outer[c][0]
                      + gate_refs[per_branch + c][rows, :].astype(_F32) * inner[c]
                      + gate_refs[2 * per_branch + c][rows, :].astype(_F32) * outer[c][1])
            o_ref[rows, cols] = merged.astype(o_ref.dtype)
        dots = next_dots


def _out_norm_kernel(m_ref, x_ref, wo_ref, lng_ref, lnb_ref, o_ref):
    bounds = [0]
    for size in OUT_NORM_CHUNKS:
        bounds.append(bounds[-1] + size)
    assert bounds[-1] == o_ref.shape[0]
    chunks = [slice(a, b) for a, b in zip(bounds[:-1], bounds[1:])]

    def out_dot(rows):
        return jnp.dot(m_ref[rows, :], wo_ref[...], preferred_element_type=_F32)

    y = out_dot(chunks[0])
    for k, rows in enumerate(chunks):
        y_next = out_dot(chunks[k + 1]) if k + 1 < len(chunks) else None
        h = DN_ALPHA * x_ref[rows, :] + y
        mu = jnp.mean(h, axis=-1, keepdims=True)
        d = h - mu
        var = jnp.mean(d * d, axis=-1, keepdims=True)
        o_ref[rows, :] = d * lax.rsqrt(var + LN_EPS) * lng_ref[...] + lnb_ref[...]
        y = y_next


def _merge_project_norm(ya, yc, proj, sgu_params, gates, x2, wa, wg, wc, wo, ln_g, ln_b, *, tm=MERGE_TM):
    t = x2.shape[0]
    w_s, b_s, ln_v_g, ln_v_b = sgu_params
    assert t % tm == 0 and tm % MERGE_ROW_CHUNK == 0 and t % OUT_NORM_TM == 0 and OUT_NORM_TM % MERGE_ROW_CHUNK == 0
    n_gate = N_BRANCH * D_MODEL // PROJ_TILE
    branch = pl.BlockSpec((tm, ATT_WIDTH), lambda i: (i, 0))
    gate = lambda c: pl.BlockSpec((tm, PROJ_TILE), lambda i: (i, c))
    rows = pl.BlockSpec((tm, D_MODEL), lambda i: (i, 0))
    resident = lambda shape: pl.BlockSpec(shape, lambda i: (0, 0), pipeline_mode=pl.Buffered(1))
    params = pltpu.CompilerParams(dimension_semantics=("arbitrary",), vmem_limit_bytes=VMEM_LIMIT_BYTES)
    section = lambda c: pl.BlockSpec((tm, PROJ_TILE), lambda i: (i, c))
    whole = lambda shape: pl.BlockSpec(shape, lambda i: (0,) * len(shape))
    merged = pl.pallas_call(
        _branch_merge_kernel,
        grid=(t // tm,),
        in_specs=[branch, branch, section(U_G), section(V_G), section(Z_G),
                  whole((SGU_GROUPS, SGU_CHUNK, SGU_CHUNK)), whole((SGU_CHUNK, SGU_GROUPS)),
                  whole((1, SGU_WIDTH)), whole((1, SGU_WIDTH)),
                  *[gate(c) for c in range(n_gate)],
                  resident((ATT_WIDTH, D_MODEL)), resident((SGU_WIDTH, D_MODEL)), resident((MEM_WIDTH, D_MODEL))],
        out_specs=rows,
        out_shape=jax.ShapeDtypeStruct((t, D_MODEL), _BF16),
        scratch_shapes=[pltpu.VMEM((tm, SGU_WIDTH), _BF16)],
        compiler_params=params,
        name="branch_merge",
    )(ya, yc, proj, proj, proj, w_s, b_s.T, ln_v_g.reshape(1, -1), ln_v_b.reshape(1, -1),
      *([gates] * n_gate), wa, wg, wc)
    out_rows = pl.BlockSpec((OUT_NORM_TM, D_MODEL), lambda i: (i, 0))
    return pl.pallas_call(
        _out_norm_kernel,
        grid=(t // OUT_NORM_TM,),
        in_specs=[out_rows, out_rows,
                  resident((D_MODEL, D_MODEL)), resident((1, D_MODEL)), resident((1, D_MODEL))],
        out_specs=out_rows,
        out_shape=jax.ShapeDtypeStruct((t, D_MODEL), _F32),
        compiler_params=params,
        name="out_proj_norm",
    )(merged, x2, wo, ln_g.reshape(1, -1), ln_b.reshape(1, -1))


def kernel(x, mem, w_in, w_mem_k, w_mem_v, w_s, b_s, ln_v_g, ln_v_b,
           w_branch_attn, w_branch_sgu, w_branch_mem, w_out, ln_g, ln_b):
    batch, seq, d = x.shape
    assert d == D_MODEL and seq % MOBA_BLOCK == 0 and mem.shape[1] == N_MEM
    t = batch * seq
    x2 = x.reshape(t, d)

    assert w_in.shape[1] == GATE_TILE0 * PROJ_TILE + N_BRANCH * D_MODEL and ATT_WIDTH == PROJ_TILE
    q_a, x_bf = _project(x2, w_in, tm=1024, tn=PROJ_TILE, name="in_proj_q", n_col_tiles=1,
                         scale=MOBA_Q_PRESCALE, row_chunks=4, emit_x_bf16=True)
    proj = _project(x_bf, w_in, tm=IN_PROJ_TM, tn=PROJ_TILE, name="in_proj_branches", first_col_tile=PROJ_TILE0,
                    n_col_tiles=GATE_TILE0 - PROJ_TILE0, row_chunks=IN_PROJ_ROW_CHUNKS)
    gates = _project(x_bf, w_in, tm=IN_PROJ_TM, tn=PROJ_TILE, name="in_proj_gates", first_col_tile=GATE_TILE0,
                     sigmoid=True, row_chunks=IN_PROJ_ROW_CHUNKS)

    ya, wa, wg, wc, wo = _moba_attention(q_a, proj, batch, seq,
                                         cast_jobs=(w_branch_attn, w_branch_sgu, w_branch_mem, w_out))
    ya = ya.reshape(t, ATT_WIDTH)
    yc = _memory_attention(proj, mem, w_mem_k, w_mem_v, seq).reshape(t, MEM_WIDTH)

    out = _merge_project_norm(ya, yc, proj, (w_s, b_s, ln_v_g, ln_v_b), gates, x2, wa, wg, wc, wo, ln_g, ln_b)
    return out.reshape(batch, seq, d)
```

```python
import functools

import jax
import jax.numpy as jnp
from jax import lax
from jax.experimental import pallas as pl
from jax.experimental.pallas import tpu as pltpu

D_MODEL = 2048
DEPTH = 1
N_MEM = 256
ATT_HEAD_DIM = 128
ATT_WIDTH = D_MODEL // 2
MOBA_BLOCK = 256
MOBA_TOPK = 3
SGU_WIDTH = D_MODEL // 2
SGU_CHUNK = 128
SGU_GROUP_DIM = 128
SGU_GROUPS = SGU_WIDTH // SGU_GROUP_DIM
MEM_HEADS = 4
MEM_WIDTH = D_MODEL // 2
MEM_HEAD_DIM = MEM_WIDTH // MEM_HEADS
N_BRANCH = 3
DN_ALPHA = (2 * DEPTH) ** 0.25
LN_EPS = 1e-5

PROJ_TILE = 1024
PROJ_TILE0 = 1
GATE_TILE0 = 9
PROJ_WIDTH = (GATE_TILE0 - PROJ_TILE0) * PROJ_TILE
K_A, V_A, Z_A, U_G, V_G, Z_G, Q_C, Z_C = range(8)

VMEM_LIMIT_BYTES = 56 * 1024 * 1024
IN_PROJ_TM = 2048
IN_PROJ_ROW_CHUNKS = 8
CAST_JOB_ROWS = 256

_NT = (((1,), (1,)), ((), ()))
_F32 = jnp.float32
_BF16 = jnp.bfloat16


_LOG2E = 1.4426950408889634


def _sigmoid(x):
    return 1.0 / (1.0 + jnp.exp2(x * -_LOG2E))


def _silu(x):
    return x * _sigmoid(x)


def _gelu_tanh(x):
    a = -2.0 * 0.7978845608028654 * _LOG2E
    return x / (1.0 + jnp.exp2(x * (a + (a * 0.044715) * (x * x))))


def _project_kernel(x_ref, w_ref, o_ref, *rest, scale, sigmoid, row_chunks, emit_x_bf16):
    x_bf_ref = rest[0] if emit_x_bf16 else None
    w_bf_ref = rest[-1]

    @pl.when(pl.program_id(1) == 0)
    def _cast_weight_tile():
        w_bf_ref[...] = w_ref[...].astype(_BF16)

    tc = x_ref.shape[0] // row_chunks
    for c in range(row_chunks):
        rows = slice(c * tc, (c + 1) * tc)
        xc = x_ref[rows, :]
        if emit_x_bf16:
            xc = xc.astype(_BF16)
            x_bf_ref[rows, :] = xc
        acc = jnp.dot(xc, w_bf_ref[...], preferred_element_type=_F32)
        if sigmoid:
            acc = _sigmoid(acc)
        if scale is not None:
            acc = acc * scale
        o_ref[rows, :] = acc.astype(o_ref.dtype)


def _project(x, w, *, tm, tn, name, first_col_tile=0, n_col_tiles=None,
             scale=None, sigmoid=False, row_chunks=1, emit_x_bf16=False):
    m, k = x.shape
    if n_col_tiles is None:
        n_col_tiles = w.shape[1] // tn - first_col_tile
    assert m % tm == 0 and tm % row_chunks == 0 and (first_col_tile + n_col_tiles) * tn <= w.shape[1]
    assert not emit_x_bf16 or n_col_tiles == 1
    out_specs = [pl.BlockSpec((tm, tn), lambda j, i: (i, j))]
    out_shape = [jax.ShapeDtypeStruct((m, n_col_tiles * tn), _BF16)]
    if emit_x_bf16:
        out_specs.append(pl.BlockSpec((tm, k), lambda j, i: (i, 0)))
        out_shape.append(jax.ShapeDtypeStruct((m, k), _BF16))
    outs = pl.pallas_call(
        functools.partial(_project_kernel, scale=scale, sigmoid=sigmoid, row_chunks=row_chunks,
                          emit_x_bf16=emit_x_bf16),
        grid=(n_col_tiles, m // tm),
        in_specs=[pl.BlockSpec((tm, k), lambda j, i: (i, 0)),
                  pl.BlockSpec((k, tn), lambda j, i: (0, first_col_tile + j))],
        out_specs=out_specs,
        out_shape=out_shape,
        scratch_shapes=[pltpu.VMEM((k, tn), _BF16)],
        compiler_params=pltpu.CompilerParams(
            dimension_semantics=("arbitrary", "arbitrary"), vmem_limit_bytes=VMEM_LIMIT_BYTES),
        name=name,
    )(x, w)
    return outs if emit_x_bf16 else outs[0]


def _cast_job_plan(arrays, n_steps, linear_step):
    specs, shapes, blocks, first = [], [], [], 0
    for a in arrays:
        assert a.shape[0] % CAST_JOB_ROWS == 0
        n_blocks = a.shape[0] // CAST_JOB_ROWS
        specs.append(pl.BlockSpec(
            (CAST_JOB_ROWS, a.shape[1]),
            lambda *idx, first=first, n_blocks=n_blocks: (jnp.clip(linear_step(*idx) - first, 0, n_blocks - 1), 0)))
        shapes.append(jax.ShapeDtypeStruct(a.shape, _BF16))
        blocks.append(n_blocks)
        first += n_blocks
    assert first <= n_steps
    return specs, shapes, tuple(blocks)


def _run_cast_jobs(step, job_in, job_out, blocks):
    first = 0
    for src, dst, n_blocks in zip(job_in, job_out, blocks):
        @pl.when((step >= first) & (step < first + n_blocks))
        def _cast_job_block(src=src, dst=dst):
            dst[...] = src[...].astype(_BF16)
        first += n_blocks


MOBA_HEADS_PER_STEP = 4
MOBA_Q_BLOCKS_PER_STEP = 2
_ACC_ROWS = ATT_HEAD_DIM + 16
MOBA_Q_PRESCALE = ATT_HEAD_DIM ** -0.5 * _LOG2E


def _moba_kernel(q_ref, k_ref, v_ref, z_ref, *rest, n_blocks, heads, cast_job_blocks):
    n_jobs = len(cast_job_blocks)
    job_in, o_ref, job_out = rest[:n_jobs], rest[n_jobs], rest[n_jobs + 1:2 * n_jobs + 1]
    vt_ref, kmean_ref, neg_ref, pos_ref, qt_ref, s_ref, acc_ref = rest[2 * n_jobs + 1:]
    blk = MOBA_BLOCK
    hd = ATT_HEAD_DIM
    step = (pl.program_id(0) * pl.num_programs(1) + pl.program_id(1)) * pl.num_programs(2) + pl.program_id(2)
    _run_cast_jobs(step, job_in, job_out, cast_job_blocks)

    @pl.when(pl.program_id(2) == 0)
    def _per_head_setup():
        for g in range(heads):
            cols = slice(g * hd, (g + 1) * hd)
            for j in range(n_blocks):
                rows = slice(j * blk, (j + 1) * blk)
                vt_ref[g, j, :hd, :] = v_ref[rows, cols].T
                vt_ref[g, j, hd:, :] = jnp.ones((_ACC_ROWS - hd, blk), _BF16)
                kmean_ref[g, j:j + 1, :] = (
                    jnp.sum(k_ref[rows, cols].astype(_F32), axis=0, keepdims=True) * (1.0 / blk))

    for sub in range(MOBA_Q_BLOCKS_PER_STEP):
        rows = slice(sub * blk, (sub + 1) * blk)
        _moba_query_block(pl.program_id(2) * MOBA_Q_BLOCKS_PER_STEP + sub, q_ref.at[rows], z_ref.at[rows],
                          o_ref.at[rows], k_ref, vt_ref, kmean_ref, neg_ref, pos_ref, qt_ref, s_ref, acc_ref,
                          n_blocks=n_blocks, heads=heads)


def _moba_query_block(qi, q_ref, z_ref, o_ref, k_ref, vt_ref, kmean_ref, neg_ref, pos_ref, qt_ref, s_ref, acc_ref,
                      *, n_blocks, heads):
    blk = MOBA_BLOCK
    hd = ATT_HEAD_DIM
    head_cols = [slice(g * hd, (g + 1) * hd) for g in range(heads)]
    qs = [q_ref[:, c] for c in head_cols]
    for g in range(heads):
        qt_ref[g] = qs[g].T

    def issue_scores(j, slot):
        start = j * blk if isinstance(j, int) else pl.multiple_of(j * blk, blk)
        for g in range(heads):
            s_ref[slot, g] = jnp.dot(k_ref[pl.ds(start, blk), head_cols[g]], qt_ref[g],
                                     preferred_element_type=_F32)

    sel_scores = []
    for g in range(heads):
        km = kmean_ref[g]
        km_hi = km.astype(_BF16)
        km_lo = (km - km_hi.astype(_F32)).astype(_BF16)
        sel_scores.append(jnp.dot(km_hi, qt_ref[g], preferred_element_type=_F32)
                          + jnp.dot(km_lo, qt_ref[g], preferred_element_type=_F32))
    issue_scores(qi, 2)
    row_id = lax.broadcasted_iota(jnp.int32, (n_blocks, blk), 0)
    past = row_id < qi
    later_rows = [jnp.where(row_id > j, 1.0, 0.0) for j in range(n_blocks)]
    for g in range(heads):
        sb = jnp.where(past, sel_scores[g], -jnp.inf)
        rank = jnp.zeros(sb.shape, _F32)
        for j in range(n_blocks):
            other = sb[j:j + 1, :]
            rank = rank + jnp.where(other > sb, 1.0, 0.0) + jnp.where(other == sb, later_rows[j], 0.0)
        selected = past & (rank < MOBA_TOPK)
        neg_ref[g] = jnp.where(selected, 0.0, -jnp.inf)
        pos_ref[g] = jnp.where(selected, -jnp.inf, jnp.inf)

    def accumulate(j, slot, ms, causal):
        probs, m_news = [], []
        for g in range(heads):
            s = s_ref[slot, g]
            if causal:
                key_pos = lax.broadcasted_iota(jnp.int32, s.shape, 0)
                q_pos = lax.broadcasted_iota(jnp.int32, s.shape, 1)
                s = jnp.where(key_pos <= q_pos, s, -jnp.inf)
                m_new = jnp.maximum(ms[g], jnp.max(s, axis=0, keepdims=True))
                m_sub = m_new
            else:
                m_new = jnp.maximum(ms[g], jnp.max(s, axis=0, keepdims=True) + neg_ref[g, pl.ds(j, 1), :])
                m_sub = jnp.maximum(m_new, pos_ref[g, pl.ds(j, 1), :])
            probs.append(jnp.exp2(s - m_sub).astype(_BF16))
            m_news.append(m_new)
        for g in range(heads):
            pv = jnp.dot(vt_ref[g, j], probs[g], preferred_element_type=_F32)
            if causal:
                acc_ref[g] = pv
            else:
                acc_ref[g] = jnp.exp2(ms[g] - m_news[g]) * acc_ref[g] + pv
        return m_news

    def past_block_pair(t, ms):
        issue_scores(2 * t + 1, 1)
        ms = accumulate(2 * t, 0, ms, causal=False)
        issue_scores(2 * t + 2, 0)
        return accumulate(2 * t + 1, 1, ms, causal=False)

    issue_scores(0, 0)
    ms = accumulate(qi, 2, [jnp.full((1, blk), -jnp.inf, _F32)] * heads, causal=True)
    ms = lax.fori_loop(0, qi // 2, past_block_pair, ms)

    @pl.when(qi % 2 == 1)
    def _last_past_block():
        accumulate(qi - 1, 0, ms, causal=False)

    for g in range(heads):
        attn = (acc_ref[g, :hd, :] / acc_ref[g, hd:hd + 1, :]).T
        o_ref[:, head_cols[g]] = (attn * _silu(z_ref[:, head_cols[g]].astype(_F32))).astype(o_ref.dtype)


def _moba_attention(q, proj, batch, seq, *, heads=MOBA_HEADS_PER_STEP, cast_jobs=()):
    n_blocks = seq // MOBA_BLOCK
    q3 = q.reshape(batch, seq, ATT_WIDTH)
    proj3 = proj.reshape(batch, seq, PROJ_WIDTH)
    width = heads * ATT_HEAD_DIM
    cols = PROJ_TILE // width
    q_rows = MOBA_Q_BLOCKS_PER_STEP * MOBA_BLOCK
    assert n_blocks % MOBA_Q_BLOCKS_PER_STEP == 0
    grid = (batch, ATT_WIDTH // width, n_blocks // MOBA_Q_BLOCKS_PER_STEP)
    job_specs, job_shapes, job_blocks = _cast_job_plan(
        cast_jobs, grid[0] * grid[1] * grid[2], lambda b, h, i: (b * grid[1] + h) * grid[2] + i)
    return pl.pallas_call(
        functools.partial(_moba_kernel, n_blocks=n_blocks, heads=heads, cast_job_blocks=job_blocks),
        grid=grid,
        in_specs=[pl.BlockSpec((None, q_rows, width), lambda b, h, i: (b, i, h)),
                  pl.BlockSpec((None, seq, width), lambda b, h, i: (b, 0, K_A * cols + h)),
                  pl.BlockSpec((None, seq, width), lambda b, h, i: (b, 0, V_A * cols + h)),
                  pl.BlockSpec((None, q_rows, width), lambda b, h, i: (b, i, Z_A * cols + h)),
                  *job_specs],
        out_specs=[pl.BlockSpec((None, q_rows, width), lambda b, h, i: (b, i, h)), *job_specs],
        out_shape=[jax.ShapeDtypeStruct((batch, seq, ATT_WIDTH), _BF16), *job_shapes],
        scratch_shapes=[pltpu.VMEM((heads, n_blocks, _ACC_ROWS, MOBA_BLOCK), _BF16),
                        pltpu.VMEM((heads, n_blocks, ATT_HEAD_DIM), _F32),
                        pltpu.VMEM((heads, n_blocks, MOBA_BLOCK), _F32),
                        pltpu.VMEM((heads, n_blocks, MOBA_BLOCK), _F32),
                        pltpu.VMEM((heads, ATT_HEAD_DIM, MOBA_BLOCK), _BF16),
                        pltpu.VMEM((3, heads, MOBA_BLOCK, MOBA_BLOCK), _F32),
                        pltpu.VMEM((heads, _ACC_ROWS, MOBA_BLOCK), _F32)],
        compiler_params=pltpu.CompilerParams(
            dimension_semantics=("arbitrary", "arbitrary", "arbitrary"), vmem_limit_bytes=VMEM_LIMIT_BYTES),
        name="moba_attention",
    )(q3, proj3, proj3, proj3, *cast_jobs)


def _sgu_causal_weights(w_ref):
    t_pos = lax.broadcasted_iota(jnp.int32, (SGU_CHUNK, SGU_CHUNK), 0)
    s_pos = lax.broadcasted_iota(jnp.int32, (SGU_CHUNK, SGU_CHUNK), 1)
    return [jnp.where(s_pos <= t_pos, w_ref[g], 0.0).astype(_BF16) for g in range(SGU_GROUPS)]


def _sgu_rows(u_ref, v_ref, z_ref, w_causal, bt_ref, g_ref, b_ref, rows, yg_ref):
    u = _gelu_tanh(u_ref[rows, :].astype(_F32))
    v = _gelu_tanh(v_ref[rows, :].astype(_F32))
    mu = jnp.mean(v, axis=-1, keepdims=True)
    d = v - mu
    var = jnp.mean(d * d, axis=-1, keepdims=True)
    vn = (d * lax.rsqrt(var + LN_EPS) * g_ref[...] + b_ref[...]).astype(_BF16)
    gate = u * _silu(z_ref[rows, :].astype(_F32))
    for g in range(SGU_GROUPS):
        bias = bt_ref[:, g:g + 1]
        cols = slice(g * SGU_GROUP_DIM, (g + 1) * SGU_GROUP_DIM)
        for c in range((rows.stop - rows.start) // SGU_CHUNK):
            sub = slice(c * SGU_CHUNK, (c + 1) * SGU_CHUNK)
            mixed = jnp.dot(w_causal[g], vn[sub, cols], preferred_element_type=_F32) + bias
            yg_ref[rows.start + c * SGU_CHUNK:rows.start + (c + 1) * SGU_CHUNK, cols] = (
                gate[sub, cols] * mixed).astype(yg_ref.dtype)


def _mem_attn_kernel(q_ref, z_ref, mem_ref, wk_ref, wv_ref, o_ref, k_ref, v_ref):
    @pl.when(pl.program_id(1) == 0)
    def _project_memory():
        mem_bf = mem_ref[...].astype(_BF16)
        for w_ref, dst in ((wk_ref, k_ref), (wv_ref, v_ref)):
            for c in range(0, MEM_WIDTH, MEM_HEAD_DIM):
                dst[:, c:c + MEM_HEAD_DIM] = jnp.dot(
                    mem_bf, w_ref[:, c:c + MEM_HEAD_DIM].astype(_BF16), preferred_element_type=_F32
                ).astype(_BF16)

    scale_log2e = MEM_HEAD_DIM ** -0.5 * _LOG2E
    for h in range(MEM_HEADS):
        cols = slice(h * MEM_HEAD_DIM, (h + 1) * MEM_HEAD_DIM)
        s = lax.dot_general(q_ref[:, cols], k_ref[:, cols], _NT, preferred_element_type=_F32)
        p = jnp.exp2((s - jnp.max(s, axis=-1, keepdims=True)) * scale_log2e)
        l = jnp.sum(p, axis=-1, keepdims=True)
        o = jnp.dot(p.astype(_BF16), v_ref[:, cols], preferred_element_type=_F32) / l
        o_ref[:, cols] = (o * _silu(z_ref[:, cols].astype(_F32))).astype(o_ref.dtype)


def _memory_attention(proj, mem, w_mem_k, w_mem_v, seq, *, tm=1024):
    batch, n_mem, d = mem.shape
    proj3 = proj.reshape(batch, seq, PROJ_WIDTH)
    weight = pl.BlockSpec((d, MEM_WIDTH), lambda b, i: (0, 0), pipeline_mode=pl.Buffered(1))
    return pl.pallas_call(
        _mem_attn_kernel,
        grid=(batch, seq // tm),
        in_specs=[pl.BlockSpec((None, tm, MEM_WIDTH), lambda b, i: (b, i, Q_C)),
                  pl.BlockSpec((None, tm, MEM_WIDTH), lambda b, i: (b, i, Z_C)),
                  pl.BlockSpec((None, n_mem, d), lambda b, i: (b, 0, 0)),
                  weight, weight],
        out_specs=pl.BlockSpec((None, tm, MEM_WIDTH), lambda b, i: (b, i, 0)),
        out_shape=jax.ShapeDtypeStruct((batch, seq, MEM_WIDTH), _BF16),
        scratch_shapes=[pltpu.VMEM((n_mem, MEM_WIDTH), _BF16), pltpu.VMEM((n_mem, MEM_WIDTH), _BF16)],
        compiler_params=pltpu.CompilerParams(
            dimension_semantics=("arbitrary", "arbitrary"), vmem_limit_bytes=VMEM_LIMIT_BYTES),
        name="memory_attention",
    )(proj3, proj3, mem, w_mem_k, w_mem_v)


MERGE_TM = 512
OUT_NORM_TM = 512
OUT_NORM_CHUNKS = (256, 128, 128)
MERGE_ROW_CHUNK = 256


def _branch_merge_kernel(ya_ref, yc_ref, uvz_ref, ws_ref, bt_ref, lvg_ref, lvb_ref, ga_ref, gg_ref, gc_ref,
                         wa_ref, wg_ref, wc_ref, o_ref, yg_ref):
    u_ref, v_ref, z_ref = (uvz_ref.at[:, k * PROJ_TILE:(k + 1) * PROJ_TILE] for k in range(3))
    per_branch = D_MODEL // PROJ_TILE
    col_tiles = [slice(c * PROJ_TILE, (c + 1) * PROJ_TILE) for c in range(per_branch)]
    chunks = [slice(r * MERGE_ROW_CHUNK, (r + 1) * MERGE_ROW_CHUNK)
              for r in range(o_ref.shape[0] // MERGE_ROW_CHUNK)]
    w_causal = _sgu_causal_weights(ws_ref)

    def branch_dots(rows):
        n_sub = len(col_tiles)
        sub_rows = (rows.stop - rows.start) // n_sub
        outer = []
        for c, cols in enumerate(col_tiles):
            outer.append((jnp.dot(ya_ref[rows, :], wa_ref[:, cols], preferred_element_type=_F32),
                          jnp.dot(yc_ref[rows, :], wc_ref[:, cols], preferred_element_type=_F32)))
            sub = slice(rows.start + c * sub_rows, rows.start + (c + 1) * sub_rows)
            _sgu_rows(u_ref, v_ref, z_ref, w_causal, bt_ref, lvg_ref, lvb_ref, sub, yg_ref)
        inner = [jnp.dot(yg_ref[rows, :], wg_ref[:, cols], preferred_element_type=_F32) for cols in col_tiles]
        return outer, inner

    dots = branch_dots(chunks[0])
    for k, rows in enumerate(chunks):
        next_dots = branch_dots(chunks[k + 1]) if k + 1 < len(chunks) else None
        outer, inner = dots
        for c, cols in enumerate(col_tiles):
            merged = (ga_ref[rows, cols].astype(_F32) * outer[c][0]
                      + gg_ref[rows, cols].astype(_F32) * inner[c]
                      + gc_ref[rows, cols].astype(_F32) * outer[c][1])
            o_ref[rows, cols] = merged.astype(o_ref.dtype)
        dots = next_dots


def _out_norm_kernel(m_ref, x_ref, wo_ref, lng_ref, lnb_ref, o_ref):
    bounds = [0]
    for size in OUT_NORM_CHUNKS:
        bounds.append(bounds[-1] + size)
    assert bounds[-1] == o_ref.shape[0]
    chunks = [slice(a, b) for a, b in zip(bounds[:-1], bounds[1:])]

    def out_dot(rows):
        return jnp.dot(m_ref[rows, :], wo_ref[...], preferred_element_type=_F32)

    y = out_dot(chunks[0])
    for k, rows in enumerate(chunks):
        y_next = out_dot(chunks[k + 1]) if k + 1 < len(chunks) else None
        h = DN_ALPHA * x_ref[rows, :] + y
        mu = jnp.mean(h, axis=-1, keepdims=True)
        d = h - mu
        var = jnp.mean(d * d, axis=-1, keepdims=True)
        o_ref[rows, :] = d * lax.rsqrt(var + LN_EPS) * lng_ref[...] + lnb_ref[...]
        y = y_next


def _merge_project_norm(ya, yc, proj, sgu_params, gates, x2, wa, wg, wc, wo, ln_g, ln_b, *, tm=MERGE_TM):
    t = x2.shape[0]
    w_s, b_s, ln_v_g, ln_v_b = sgu_params
    assert t % tm == 0 and tm % MERGE_ROW_CHUNK == 0 and t % OUT_NORM_TM == 0 and OUT_NORM_TM % MERGE_ROW_CHUNK == 0
    branch = pl.BlockSpec((tm, ATT_WIDTH), lambda i: (i, 0))
    gate = lambda br: pl.BlockSpec((tm, D_MODEL), lambda i: (i, br))
    rows = pl.BlockSpec((tm, D_MODEL), lambda i: (i, 0))
    resident = lambda shape: pl.BlockSpec(shape, lambda i: (0, 0), pipeline_mode=pl.Buffered(1))
    params = pltpu.CompilerParams(dimension_semantics=("arbitrary",), vmem_limit_bytes=VMEM_LIMIT_BYTES)
    assert (V_G, Z_G) == (U_G + 1, U_G + 2) and U_G % 3 == 0
    whole = lambda shape: pl.BlockSpec(shape, lambda i: (0,) * len(shape))
    merged = pl.pallas_call(
        _branch_merge_kernel,
        grid=(t // tm,),
        in_specs=[branch, branch, pl.BlockSpec((tm, 3 * PROJ_TILE), lambda i: (i, U_G // 3)),
                  whole((SGU_GROUPS, SGU_CHUNK, SGU_CHUNK)), whole((SGU_CHUNK, SGU_GROUPS)),
                  whole((1, SGU_WIDTH)), whole((1, SGU_WIDTH)),
                  *[gate(br) for br in range(N_BRANCH)],
                  resident((ATT_WIDTH, D_MODEL)), resident((SGU_WIDTH, D_MODEL)), resident((MEM_WIDTH, D_MODEL))],
        out_specs=rows,
        out_shape=jax.ShapeDtypeStruct((t, D_MODEL), _BF16),
        scratch_shapes=[pltpu.VMEM((tm, SGU_WIDTH), _BF16)],
        compiler_params=params,
        name="branch_merge",
    )(ya, yc, proj, w_s, b_s.T, ln_v_g.reshape(1, -1), ln_v_b.reshape(1, -1),
      *([gates] * N_BRANCH), wa, wg, wc)
    out_rows = pl.BlockSpec((OUT_NORM_TM, D_MODEL), lambda i: (i, 0))
    return pl.pallas_call(
        _out_norm_kernel,
        grid=(t // OUT_NORM_TM,),
        in_specs=[out_rows, out_rows,
                  resident((D_MODEL, D_MODEL)), resident((1, D_MODEL)), resident((1, D_MODEL))],
        out_specs=out_rows,
        out_shape=jax.ShapeDtypeStruct((t, D_MODEL), _F32),
        compiler_params=params,
        name="out_proj_norm",
    )(merged, x2, wo, ln_g.reshape(1, -1), ln_b.reshape(1, -1))


def kernel(x, mem, w_in, w_mem_k, w_mem_v, w_s, b_s, ln_v_g, ln_v_b,
           w_branch_attn, w_branch_sgu, w_branch_mem, w_out, ln_g, ln_b):
    batch, seq, d = x.shape
    assert d == D_MODEL and seq % MOBA_BLOCK == 0 and mem.shape[1] == N_MEM
    t = batch * seq
    x2 = x.reshape(t, d)

    assert w_in.shape[1] == GATE_TILE0 * PROJ_TILE + N_BRANCH * D_MODEL and ATT_WIDTH == PROJ_TILE
    q_a, x_bf = _project(x2, w_in, tm=1024, tn=PROJ_TILE, name="in_proj_q", n_col_tiles=1,
                         scale=MOBA_Q_PRESCALE, row_chunks=4, emit_x_bf16=True)
    proj = _project(x_bf, w_in, tm=IN_PROJ_TM, tn=PROJ_TILE, name="in_proj_branches", first_col_tile=PROJ_TILE0,
                    n_col_tiles=GATE_TILE0 - PROJ_TILE0, row_chunks=IN_PROJ_ROW_CHUNKS)
    gates = _project(x_bf, w_in, tm=IN_PROJ_TM, tn=PROJ_TILE, name="in_proj_gates", first_col_tile=GATE_TILE0,
                     sigmoid=True, row_chunks=IN_PROJ_ROW_CHUNKS)

    ya, wa, wg, wc, wo = _moba_attention(q_a, proj, batch, seq,
                                         cast_jobs=(w_branch_attn, w_branch_sgu, w_branch_mem, w_out))
    ya = ya.reshape(t, ATT_WIDTH)
    yc = _memory_attention(proj, mem, w_mem_k, w_mem_v, seq).reshape(t, MEM_WIDTH)

    out = _merge_project_norm(ya, yc, proj, (w_s, b_s, ln_v_g, ln_v_b), gates, x2, wa, wg, wc, wo, ln_g, ln_b)
    return out.reshape(batch, seq, d)
```

```python
import functools

import jax
import jax.numpy as jnp
from jax import lax
from jax.experimental import pallas as pl
from jax.experimental.pallas import tpu as pltpu

D_MODEL = 2048
DEPTH = 1
N_MEM = 256
ATT_HEAD_DIM = 128
ATT_WIDTH = D_MODEL // 2
MOBA_BLOCK = 256
MOBA_TOPK = 3
SGU_WIDTH = D_MODEL // 2
SGU_CHUNK = 128
SGU_GROUP_DIM = 128
SGU_GROUPS = SGU_WIDTH // SGU_GROUP_DIM
MEM_HEADS = 4
MEM_WIDTH = D_MODEL // 2
MEM_HEAD_DIM = MEM_WIDTH // MEM_HEADS
N_BRANCH = 3
DN_ALPHA = (2 * DEPTH) ** 0.25
LN_EPS = 1e-5

PROJ_TILE = 1024
PROJ_TILE0 = 1
GATE_TILE0 = 9
PROJ_WIDTH = (GATE_TILE0 - PROJ_TILE0) * PROJ_TILE
K_A, V_A, Z_A, U_G, V_G, Z_G, Q_C, Z_C = range(8)
PROJ_ACTIVATIONS = ("id", "id", "silu", "gelu", "gelu", "silu", "id", "silu")

VMEM_LIMIT_BYTES = 56 * 1024 * 1024
IN_PROJ_TM = 2048
IN_PROJ_ROW_CHUNKS = 8
CAST_JOB_ROWS = 256

_NT = (((1,), (1,)), ((), ()))
_F32 = jnp.float32
_BF16 = jnp.bfloat16


_LOG2E = 1.4426950408889634


def _sigmoid(x):
    return 1.0 / (1.0 + jnp.exp2(x * -_LOG2E))


_GELU_A = -2.0 * 0.7978845608028654 * _LOG2E
_ACTIVATION_AB = {"silu": (-_LOG2E, 0.0), "gelu": (_GELU_A, _GELU_A * 0.044715)}


def _activation(x, kind):
    if kind == "id":
        return x
    a, b = _ACTIVATION_AB[kind]
    return x / (1.0 + jnp.exp2(x * (a + b * (x * x)) if b else x * a))


def _project_kernel(x_ref, w_ref, o_ref, *rest, scale, sigmoid, row_chunks, emit_x_bf16, activations):
    x_bf_ref = rest[0] if emit_x_bf16 else None
    w_bf_ref = rest[-1]

    @pl.when(pl.program_id(1) == 0)
    def _cast_weight_tile():
        w_bf_ref[...] = w_ref[...].astype(_BF16)

    def tile_body(kind):
        tc = x_ref.shape[0] // row_chunks
        for c in range(row_chunks):
            rows = slice(c * tc, (c + 1) * tc)
            xc = x_ref[rows, :]
            if emit_x_bf16:
                xc = xc.astype(_BF16)
                x_bf_ref[rows, :] = xc
            acc = jnp.dot(xc, w_bf_ref[...], preferred_element_type=_F32)
            if sigmoid:
                acc = _sigmoid(acc)
            if scale is not None:
                acc = acc * scale
            o_ref[rows, :] = _activation(acc, kind).astype(o_ref.dtype)

    if activations is None:
        tile_body("id")
        return
    for kind in sorted(set(activations)):
        hit = functools.reduce(jnp.logical_or,
                               [pl.program_id(0) == idx for idx, k in enumerate(activations) if k == kind])
        pl.when(hit)(functools.partial(tile_body, kind))


def _project(x, w, *, tm, tn, name, first_col_tile=0, n_col_tiles=None,
             scale=None, sigmoid=False, row_chunks=1, emit_x_bf16=False, activations=None):
    m, k = x.shape
    if n_col_tiles is None:
        n_col_tiles = w.shape[1] // tn - first_col_tile
    assert m % tm == 0 and tm % row_chunks == 0 and (first_col_tile + n_col_tiles) * tn <= w.shape[1]
    assert not emit_x_bf16 or n_col_tiles == 1
    assert activations is None or len(activations) == n_col_tiles
    out_specs = [pl.BlockSpec((tm, tn), lambda j, i: (i, j))]
    out_shape = [jax.ShapeDtypeStruct((m, n_col_tiles * tn), _BF16)]
    if emit_x_bf16:
        out_specs.append(pl.BlockSpec((tm, k), lambda j, i: (i, 0)))
        out_shape.append(jax.ShapeDtypeStruct((m, k), _BF16))
    outs = pl.pallas_call(
        functools.partial(_project_kernel, scale=scale, sigmoid=sigmoid, row_chunks=row_chunks,
                          emit_x_bf16=emit_x_bf16, activations=activations),
        grid=(n_col_tiles, m // tm),
        in_specs=[pl.BlockSpec((tm, k), lambda j, i: (i, 0)),
                  pl.BlockSpec((k, tn), lambda j, i: (0, first_col_tile + j))],
        out_specs=out_specs,
        out_shape=out_shape,
        scratch_shapes=[pltpu.VMEM((k, tn), _BF16)],
        compiler_params=pltpu.CompilerParams(
            dimension_semantics=("arbitrary", "arbitrary"), vmem_limit_bytes=VMEM_LIMIT_BYTES),
        name=name,
    )(x, w)
    return outs if emit_x_bf16 else outs[0]


def _cast_job_plan(arrays, n_steps, linear_step):
    specs, shapes, blocks, first = [], [], [], 0
    for a in arrays:
        assert a.shape[0] % CAST_JOB_ROWS == 0
        n_blocks = a.shape[0] // CAST_JOB_ROWS
        specs.append(pl.BlockSpec(
            (CAST_JOB_ROWS, a.shape[1]),
            lambda *idx, first=first, n_blocks=n_blocks: (jnp.clip(linear_step(*idx) - first, 0, n_blocks - 1), 0)))
        shapes.append(jax.ShapeDtypeStruct(a.shape, _BF16))
        blocks.append(n_blocks)
        first += n_blocks
    assert first <= n_steps
    return specs, shapes, tuple(blocks)


def _run_cast_jobs(step, job_in, job_out, blocks):
    first = 0
    for src, dst, n_blocks in zip(job_in, job_out, blocks):
        @pl.when((step >= first) & (step < first + n_blocks))
        def _cast_job_block(src=src, dst=dst):
            dst[...] = src[...].astype(_BF16)
        first += n_blocks


MOBA_HEADS_PER_STEP = 4
MOBA_Q_BLOCKS_PER_STEP = 2
_ACC_ROWS = ATT_HEAD_DIM + 16
MOBA_Q_PRESCALE = ATT_HEAD_DIM ** -0.5 * _LOG2E


def _moba_kernel(q_ref, k_ref, v_ref, z_ref, *rest, n_blocks, heads, cast_job_blocks):
    n_jobs = len(cast_job_blocks)
    job_in, o_ref, job_out = rest[:n_jobs], rest[n_jobs], rest[n_jobs + 1:2 * n_jobs + 1]
    vt_ref, kmean_ref, neg_ref, pos_ref, qt_ref, s_ref, acc_ref = rest[2 * n_jobs + 1:]
    blk = MOBA_BLOCK
    hd = ATT_HEAD_DIM
    step = (pl.program_id(0) * pl.num_programs(1) + pl.program_id(1)) * pl.num_programs(2) + pl.program_id(2)
    _run_cast_jobs(step, job_in, job_out, cast_job_blocks)

    @pl.when(pl.program_id(2) == 0)
    def _per_head_setup():
        for g in range(heads):
            cols = slice(g * hd, (g + 1) * hd)
            for j in range(n_blocks):
                rows = slice(j * blk, (j + 1) * blk)
                vt_ref[g, j, :hd, :] = v_ref[rows, cols].T
                vt_ref[g, j, hd:, :] = jnp.ones((_ACC_ROWS - hd, blk), _BF16)
                kmean_ref[g, j:j + 1, :] = (
                    jnp.sum(k_ref[rows, cols].astype(_F32), axis=0, keepdims=True) * (1.0 / blk))

    for sub in range(MOBA_Q_BLOCKS_PER_STEP):
        rows = slice(sub * blk, (sub + 1) * blk)
        _moba_query_block(pl.program_id(2) * MOBA_Q_BLOCKS_PER_STEP + sub, q_ref.at[rows], z_ref.at[rows],
                          o_ref.at[rows], k_ref, vt_ref, kmean_ref, neg_ref, pos_ref, qt_ref, s_ref, acc_ref,
                          n_blocks=n_blocks, heads=heads)


def _moba_query_block(qi, q_ref, z_ref, o_ref, k_ref, vt_ref, kmean_ref, neg_ref, pos_ref, qt_ref, s_ref, acc_ref,
                      *, n_blocks, heads):
    blk = MOBA_BLOCK
    hd = ATT_HEAD_DIM
    head_cols = [slice(g * hd, (g + 1) * hd) for g in range(heads)]
    qs = [q_ref[:, c] for c in head_cols]
    for g in range(heads):
        qt_ref[g] = qs[g].T

    def issue_scores(j, slot):
        start = j * blk if isinstance(j, int) else pl.multiple_of(j * blk, blk)
        for g in range(heads):
            s_ref[slot, g] = jnp.dot(k_ref[pl.ds(start, blk), head_cols[g]], qt_ref[g],
                                     preferred_element_type=_F32)

    sel_scores = []
    for g in range(heads):
        km = kmean_ref[g]
        km_hi = km.astype(_BF16)
        km_lo = (km - km_hi.astype(_F32)).astype(_BF16)
        sel_scores.append(jnp.dot(km_hi, qt_ref[g], preferred_element_type=_F32)
                          + jnp.dot(km_lo, qt_ref[g], preferred_element_type=_F32))
    issue_scores(qi, 2)
    row_id = lax.broadcasted_iota(jnp.int32, (n_blocks, blk), 0)
    past = row_id < qi
    later_rows = [jnp.where(row_id > j, 1.0, 0.0) for j in range(n_blocks)]
    for g in range(heads):
        sb = jnp.where(past, sel_scores[g], -jnp.inf)
        rank = jnp.zeros(sb.shape, _F32)
        for j in range(n_blocks):
            other = sb[j:j + 1, :]
            rank = rank + jnp.where(other > sb, 1.0, 0.0) + jnp.where(other == sb, later_rows[j], 0.0)
        selected = past & (rank < MOBA_TOPK)
        neg_ref[g] = jnp.where(selected, 0.0, -jnp.inf)
        pos_ref[g] = jnp.where(selected, -jnp.inf, jnp.inf)

    def accumulate(j, slot, ms, causal):
        probs, m_news = [], []
        for g in range(heads):
            s = s_ref[slot, g]
            if causal:
                key_pos = lax.broadcasted_iota(jnp.int32, s.shape, 0)
                q_pos = lax.broadcasted_iota(jnp.int32, s.shape, 1)
                s = jnp.where(key_pos <= q_pos, s, -jnp.inf)
                m_new = jnp.maximum(ms[g], jnp.max(s, axis=0, keepdims=True))
                m_sub = m_new
            else:
                m_new = jnp.maximum(ms[g], jnp.max(s, axis=0, keepdims=True) + neg_ref[g, pl.ds(j, 1), :])
                m_sub = jnp.maximum(m_new, pos_ref[g, pl.ds(j, 1), :])
            probs.append(jnp.exp2(s - m_sub).astype(_BF16))
            m_news.append(m_new)
        for g in range(heads):
            pv = jnp.dot(vt_ref[g, j], probs[g], preferred_element_type=_F32)
            if causal:
                acc_ref[g] = pv
            else:
                acc_ref[g] = jnp.exp2(ms[g] - m_news[g]) * acc_ref[g] + pv
        return m_news

    def past_block_pair(t, ms):
        issue_scores(2 * t + 1, 1)
        ms = accumulate(2 * t, 0, ms, causal=False)
        issue_scores(2 * t + 2, 0)
        return accumulate(2 * t + 1, 1, ms, causal=False)

    issue_scores(0, 0)
    ms = accumulate(qi, 2, [jnp.full((1, blk), -jnp.inf, _F32)] * heads, causal=True)
    ms = lax.fori_loop(0, qi // 2, past_block_pair, ms)

    @pl.when(qi % 2 == 1)
    def _last_past_block():
        accumulate(qi - 1, 0, ms, causal=False)

    for g in range(heads):
        attn = (acc_ref[g, :hd, :] / acc_ref[g, hd:hd + 1, :]).T
        o_ref[:, head_cols[g]] = (attn * z_ref[:, head_cols[g]].astype(_F32)).astype(o_ref.dtype)


def _moba_attention(q, proj, batch, seq, *, heads=MOBA_HEADS_PER_STEP, cast_jobs=()):
    n_blocks = seq // MOBA_BLOCK
    q3 = q.reshape(batch, seq, ATT_WIDTH)
    proj3 = proj.reshape(batch, seq, PROJ_WIDTH)
    width = heads * ATT_HEAD_DIM
    cols = PROJ_TILE // width
    q_rows = MOBA_Q_BLOCKS_PER_STEP * MOBA_BLOCK
    assert n_blocks % MOBA_Q_BLOCKS_PER_STEP == 0
    grid = (batch, ATT_WIDTH // width, n_blocks // MOBA_Q_BLOCKS_PER_STEP)
    job_specs, job_shapes, job_blocks = _cast_job_plan(
        cast_jobs, grid[0] * grid[1] * grid[2], lambda b, h, i: (b * grid[1] + h) * grid[2] + i)
    return pl.pallas_call(
        functools.partial(_moba_kernel, n_blocks=n_blocks, heads=heads, cast_job_blocks=job_blocks),
        grid=grid,
        in_specs=[pl.BlockSpec((None, q_rows, width), lambda b, h, i: (b, i, h)),
                  pl.BlockSpec((None, seq, width), lambda b, h, i: (b, 0, K_A * cols + h)),
                  pl.BlockSpec((None, seq, width), lambda b, h, i: (b, 0, V_A * cols + h)),
                  pl.BlockSpec((None, q_rows, width), lambda b, h, i: (b, i, Z_A * cols + h)),
                  *job_specs],
        out_specs=[pl.BlockSpec((None, q_rows, width), lambda b, h, i: (b, i, h)), *job_specs],
        out_shape=[jax.ShapeDtypeStruct((batch, seq, ATT_WIDTH), _BF16), *job_shapes],
        scratch_shapes=[pltpu.VMEM((heads, n_blocks, _ACC_ROWS, MOBA_BLOCK), _BF16),
                        pltpu.VMEM((heads, n_blocks, ATT_HEAD_DIM), _F32),
                        pltpu.VMEM((heads, n_blocks, MOBA_BLOCK), _F32),
                        pltpu.VMEM((heads, n_blocks, MOBA_BLOCK), _F32),
                        pltpu.VMEM((heads, ATT_HEAD_DIM, MOBA_BLOCK), _BF16),
                        pltpu.VMEM((3, heads, MOBA_BLOCK, MOBA_BLOCK), _F32),
                        pltpu.VMEM((heads, _ACC_ROWS, MOBA_BLOCK), _F32)],
        compiler_params=pltpu.CompilerParams(
            dimension_semantics=("arbitrary", "arbitrary", "arbitrary"), vmem_limit_bytes=VMEM_LIMIT_BYTES),
        name="moba_attention",
    )(q3, proj3, proj3, proj3, *cast_jobs)


def _sgu_causal_weights(w_ref):
    t_pos = lax.broadcasted_iota(jnp.int32, (SGU_CHUNK, SGU_CHUNK), 0)
    s_pos = lax.broadcasted_iota(jnp.int32, (SGU_CHUNK, SGU_CHUNK), 1)
    return [jnp.where(s_pos <= t_pos, w_ref[g], 0.0).astype(_BF16) for g in range(SGU_GROUPS)]


def _sgu_rows(u_ref, v_ref, z_ref, w_causal, bt_ref, g_ref, b_ref, rows, yg_ref):
    u = u_ref[rows, :].astype(_F32)
    v = v_ref[rows, :].astype(_F32)
    mu = jnp.mean(v, axis=-1, keepdims=True)
    d = v - mu
    var = jnp.mean(d * d, axis=-1, keepdims=True)
    vn = (d * lax.rsqrt(var + LN_EPS) * g_ref[...] + b_ref[...]).astype(_BF16)
    gate = u * z_ref[rows, :].astype(_F32)
    for g in range(SGU_GROUPS):
        bias = bt_ref[:, g:g + 1]
        cols = slice(g * SGU_GROUP_DIM, (g + 1) * SGU_GROUP_DIM)
        for c in range((rows.stop - rows.start) // SGU_CHUNK):
            sub = slice(c * SGU_CHUNK, (c + 1) * SGU_CHUNK)
            mixed = jnp.dot(w_causal[g], vn[sub, cols], preferred_element_type=_F32) + bias
            yg_ref[rows.start + c * SGU_CHUNK:rows.start + (c + 1) * SGU_CHUNK, cols] = (
                gate[sub, cols] * mixed).astype(yg_ref.dtype)


def _mem_attn_kernel(q_ref, z_ref, mem_ref, wk_ref, wv_ref, o_ref, k_ref, v_ref):
    @pl.when(pl.program_id(1) == 0)
    def _project_memory():
        mem_bf = mem_ref[...].astype(_BF16)
        for w_ref, dst in ((wk_ref, k_ref), (wv_ref, v_ref)):
            for c in range(0, MEM_WIDTH, MEM_HEAD_DIM):
                dst[:, c:c + MEM_HEAD_DIM] = jnp.dot(
                    mem_bf, w_ref[:, c:c + MEM_HEAD_DIM].astype(_BF16), preferred_element_type=_F32
                ).astype(_BF16)

    scale_log2e = MEM_HEAD_DIM ** -0.5 * _LOG2E
    for h in range(MEM_HEADS):
        cols = slice(h * MEM_HEAD_DIM, (h + 1) * MEM_HEAD_DIM)
        s = lax.dot_general(q_ref[:, cols], k_ref[:, cols], _NT, preferred_element_type=_F32)
        p = jnp.exp2((s - jnp.max(s, axis=-1, keepdims=True)) * scale_log2e)
        l = jnp.sum(p, axis=-1, keepdims=True)
        o = jnp.dot(p.astype(_BF16), v_ref[:, cols], preferred_element_type=_F32) / l
        o_ref[:, cols] = (o * z_ref[:, cols].astype(_F32)).astype(o_ref.dtype)


def _memory_attention(proj, mem, w_mem_k, w_mem_v, seq, *, tm=1024):
    batch, n_mem, d = mem.shape
    proj3 = proj.reshape(batch, seq, PROJ_WIDTH)
    weight = pl.BlockSpec((d, MEM_WIDTH), lambda b, i: (0, 0), pipeline_mode=pl.Buffered(1))
    return pl.pallas_call(
        _mem_attn_kernel,
        grid=(batch, seq // tm),
        in_specs=[pl.BlockSpec((None, tm, MEM_WIDTH), lambda b, i: (b, i, Q_C)),
                  pl.BlockSpec((None, tm, MEM_WIDTH), lambda b, i: (b, i, Z_C)),
                  pl.BlockSpec((None, n_mem, d), lambda b, i: (b, 0, 0)),
                  weight, weight],
        out_specs=pl.BlockSpec((None, tm, MEM_WIDTH), lambda b, i: (b, i, 0)),
        out_shape=jax.ShapeDtypeStruct((batch, seq, MEM_WIDTH), _BF16),
        scratch_shapes=[pltpu.VMEM((n_mem, MEM_WIDTH), _BF16), pltpu.VMEM((n_mem, MEM_WIDTH), _BF16)],
        compiler_params=pltpu.CompilerParams(
            dimension_semantics=("arbitrary", "arbitrary"), vmem_limit_bytes=VMEM_LIMIT_BYTES),
        name="memory_attention",
    )(proj3, proj3, mem, w_mem_k, w_mem_v)


MERGE_TM = 512
OUT_NORM_TM = 512
OUT_NORM_CHUNKS = (256, 128, 128)
MERGE_ROW_CHUNK = 256


def _branch_merge_kernel(ya_ref, yc_ref, uvz_ref, ws_ref, bt_ref, lvg_ref, lvb_ref, ga_ref, gg_ref, gc_ref,
                         wa_ref, wg_ref, wc_ref, o_ref, yg_ref):
    u_ref, v_ref, z_ref = (uvz_ref.at[:, k * PROJ_TILE:(k + 1) * PROJ_TILE] for k in range(3))
    per_branch = D_MODEL // PROJ_TILE
    col_tiles = [slice(c * PROJ_TILE, (c + 1) * PROJ_TILE) for c in range(per_branch)]
    chunks = [slice(r * MERGE_ROW_CHUNK, (r + 1) * MERGE_ROW_CHUNK)
              for r in range(o_ref.shape[0] // MERGE_ROW_CHUNK)]
    w_causal = _sgu_causal_weights(ws_ref)

    def branch_dots(rows):
        n_sub = len(col_tiles)
        sub_rows = (rows.stop - rows.start) // n_sub
        outer = []
        for c, cols in enumerate(col_tiles):
            outer.append((jnp.dot(ya_ref[rows, :], wa_ref[:, cols], preferred_element_type=_F32),
                          jnp.dot(yc_ref[rows, :], wc_ref[:, cols], preferred_element_type=_F32)))
            sub = slice(rows.start + c * sub_rows, rows.start + (c + 1) * sub_rows)
            _sgu_rows(u_ref, v_ref, z_ref, w_causal, bt_ref, lvg_ref, lvb_ref, sub, yg_ref)
        inner = [jnp.dot(yg_ref[rows, :], wg_ref[:, cols], preferred_element_type=_F32) for cols in col_tiles]
        return outer, inner

    dots = branch_dots(chunks[0])
    for k, rows in enumerate(chunks):
        next_dots = branch_dots(chunks[k + 1]) if k + 1 < len(chunks) else None
        outer, inner = dots
        for c, cols in enumerate(col_tiles):
            merged = (ga_ref[rows, cols].astype(_F32) * outer[c][0]
                      + gg_ref[rows, cols].astype(_F32) * inner[c]
                      + gc_ref[rows, cols].astype(_F32) * outer[c][1])
            o_ref[rows, cols] = merged.astype(o_ref.dtype)
        dots = next_dots


def _out_norm_kernel(m_ref, x_ref, wo_ref, lng_ref, lnb_ref, o_ref):
    bounds = [0]
    for size in OUT_NORM_CHUNKS:
        bounds.append(bounds[-1] + size)
    assert bounds[-1] == o_ref.shape[0]
    chunks = [slice(a, b) for a, b in zip(bounds[:-1], bounds[1:])]

    def out_dot(rows):
        return jnp.dot(m_ref[rows, :], wo_ref[...], preferred_element_type=_F32)

    y = out_dot(chunks[0])
    for k, rows in enumerate(chunks):
        y_next = out_dot(chunks[k + 1]) if k + 1 < len(chunks) else None
        h = DN_ALPHA * x_ref[rows, :] + y
        mu = jnp.mean(h, axis=-1, keepdims=True)
        d = h - mu
        var = jnp.mean(d * d, axis=-1, keepdims=True)
        o_ref[rows, :] = d * lax.rsqrt(var + LN_EPS) * lng_ref[...] + lnb_ref[...]
        y = y_next


def _merge_project_norm(ya, yc, proj, sgu_params, gates, x2, wa, wg, wc, wo, ln_g, ln_b, *, tm=MERGE_TM):
    t = x2.shape[0]
    w_s, b_s, ln_v_g, ln_v_b = sgu_params
    assert t % tm == 0 and tm % MERGE_ROW_CHUNK == 0 and t % OUT_NORM_TM == 0 and OUT_NORM_TM % MERGE_ROW_CHUNK == 0
    branch = pl.BlockSpec((tm, ATT_WIDTH), lambda i: (i, 0))
    gate = lambda br: pl.BlockSpec((tm, D_MODEL), lambda i: (i, br))
    rows = pl.BlockSpec((tm, D_MODEL), lambda i: (i, 0))
    resident = lambda shape: pl.BlockSpec(shape, lambda i: (0, 0), pipeline_mode=pl.Buffered(1))
    params = pltpu.CompilerParams(dimension_semantics=("arbitrary",), vmem_limit_bytes=VMEM_LIMIT_BYTES)
    assert (V_G, Z_G) == (U_G + 1, U_G + 2) and U_G % 3 == 0
    whole = lambda shape: pl.BlockSpec(shape, lambda i: (0,) * len(shape))
    merged = pl.pallas_call(
        _branch_merge_kernel,
        grid=(t // tm,),
        in_specs=[branch, branch, pl.BlockSpec((tm, 3 * PROJ_TILE), lambda i: (i, U_G // 3)),
                  whole((SGU_GROUPS, SGU_CHUNK, SGU_CHUNK)), whole((SGU_CHUNK, SGU_GROUPS)),
                  whole((1, SGU_WIDTH)), whole((1, SGU_WIDTH)),
                  *[gate(br) for br in range(N_BRANCH)],
                  resident((ATT_WIDTH, D_MODEL)), resident((SGU_WIDTH, D_MODEL)), resident((MEM_WIDTH, D_MODEL))],
        out_specs=rows,
        out_shape=jax.ShapeDtypeStruct((t, D_MODEL), _BF16),
        scratch_shapes=[pltpu.VMEM((tm, SGU_WIDTH), _BF16)],
        compiler_params=params,
        name="branch_merge",
    )(ya, yc, proj, w_s, b_s.T, ln_v_g.reshape(1, -1), ln_v_b.reshape(1, -1),
      *([gates] * N_BRANCH), wa, wg, wc)
    out_rows = pl.BlockSpec((OUT_NORM_TM, D_MODEL), lambda i: (i, 0))
    return pl.pallas_call(
        _out_norm_kernel,
        grid=(t // OUT_NORM_TM,),
        in_specs=[out_rows, out_rows,
                  resident((D_MODEL, D_MODEL)), resident((1, D_MODEL)), resident((1, D_MODEL))],
        out_specs=out_rows,
        out_shape=jax.ShapeDtypeStruct((t, D_MODEL), _F32),
        compiler_params=params,
        name="out_proj_norm",
    )(merged, x2, wo, ln_g.reshape(1, -1), ln_b.reshape(1, -1))


def kernel(x, mem, w_in, w_mem_k, w_mem_v, w_s, b_s, ln_v_g, ln_v_b,
           w_branch_attn, w_branch_sgu, w_branch_mem, w_out, ln_g, ln_b):
    batch, seq, d = x.shape
    assert d == D_MODEL and seq % MOBA_BLOCK == 0 and mem.shape[1] == N_MEM
    t = batch * seq
    x2 = x.reshape(t, d)

    assert w_in.shape[1] == GATE_TILE0 * PROJ_TILE + N_BRANCH * D_MODEL and ATT_WIDTH == PROJ_TILE
    q_a, x_bf = _project(x2, w_in, tm=1024, tn=PROJ_TILE, name="in_proj_q", n_col_tiles=1,
                         scale=MOBA_Q_PRESCALE, row_chunks=4, emit_x_bf16=True)
    proj = _project(x_bf, w_in, tm=IN_PROJ_TM, tn=PROJ_TILE, name="in_proj_branches", first_col_tile=PROJ_TILE0,
                    n_col_tiles=GATE_TILE0 - PROJ_TILE0, row_chunks=IN_PROJ_ROW_CHUNKS,
                    activations=PROJ_ACTIVATIONS)
    gates = _project(x_bf, w_in, tm=IN_PROJ_TM, tn=PROJ_TILE, name="in_proj_gates", first_col_tile=GATE_TILE0,
                     sigmoid=True, row_chunks=IN_PROJ_ROW_CHUNKS)

    ya, wa, wg, wc, wo = _moba_attention(q_a, proj, batch, seq,
                                         cast_jobs=(w_branch_attn, w_branch_sgu, w_branch_mem, w_out))
    ya = ya.reshape(t, ATT_WIDTH)
    yc = _memory_attention(proj, mem, w_mem_k, w_mem_v, seq).reshape(t, MEM_WIDTH)

    out = _merge_project_norm(ya, yc, proj, (w_s, b_s, ln_v_g, ln_v_b), gates, x2, wa, wg, wc, wo, ln_g, ln_b)
    return out.reshape(batch, seq, d)
```

```python
import functools

import jax
import jax.numpy as jnp
from jax import lax
from jax.experimental import pallas as pl
from jax.experimental.pallas import tpu as pltpu

D_MODEL = 2048
DEPTH = 1
N_MEM = 256
ATT_HEAD_DIM = 128
ATT_WIDTH = D_MODEL // 2
MOBA_BLOCK = 256
MOBA_TOPK = 3
SGU_WIDTH = D_MODEL // 2
SGU_CHUNK = 128
SGU_GROUP_DIM = 128
SGU_GROUPS = SGU_WIDTH // SGU_GROUP_DIM
MEM_HEADS = 4
MEM_WIDTH = D_MODEL // 2
MEM_HEAD_DIM = MEM_WIDTH // MEM_HEADS
N_BRANCH = 3
DN_ALPHA = (2 * DEPTH) ** 0.25
LN_EPS = 1e-5

PROJ_TILE = 1024
GATE_TILE0 = 9
PLAIN_TILES, SILU_TILES, GELU_TILES = (1, 2, 7), (3, 6, 8), (4, 5)
K_A, V_A, Q_C = range(3)
Z_A, Z_G, Z_C = range(3)
U_G, V_G = range(2)

VMEM_LIMIT_BYTES = 56 * 1024 * 1024
IN_PROJ_TM = 2048
IN_PROJ_ROW_CHUNKS = 8
CAST_JOB_ROWS = 256

_NT = (((1,), (1,)), ((), ()))
_F32 = jnp.float32
_BF16 = jnp.bfloat16


_LOG2E = 1.4426950408889634


def _sigmoid(x):
    return 1.0 / (1.0 + jnp.exp2(x * -_LOG2E))


def _silu(x):
    return x * _sigmoid(x)


def _gelu_tanh(x):
    a = -2.0 * 0.7978845608028654 * _LOG2E
    return x / (1.0 + jnp.exp2(x * (a + (a * 0.044715) * (x * x))))


_ACTIVATIONS = {None: lambda x: x, "sigmoid": _sigmoid, "silu": _silu, "gelu": _gelu_tanh}


def _project_kernel(x_ref, w_ref, o_ref, *rest, scale, activation, row_chunks, emit_x_bf16):
    x_bf_ref = rest[0] if emit_x_bf16 else None
    w_bf_ref = rest[-1]

    @pl.when(pl.program_id(1) == 0)
    def _cast_weight_tile():
        w_bf_ref[...] = w_ref[...].astype(_BF16)

    tc = x_ref.shape[0] // row_chunks
    for c in range(row_chunks):
        rows = slice(c * tc, (c + 1) * tc)
        xc = x_ref[rows, :]
        if emit_x_bf16:
            xc = xc.astype(_BF16)
            x_bf_ref[rows, :] = xc
        acc = _ACTIVATIONS[activation](jnp.dot(xc, w_bf_ref[...], preferred_element_type=_F32))
        if scale is not None:
            acc = acc * scale
        o_ref[rows, :] = acc.astype(o_ref.dtype)


def _project(x, w, *, tm, tn, name, col_tiles, scale=None, activation=None, row_chunks=1, emit_x_bf16=False):
    m, k = x.shape
    n_col_tiles = len(col_tiles)
    assert m % tm == 0 and tm % row_chunks == 0 and max(col_tiles) * tn < w.shape[1]
    assert not emit_x_bf16 or n_col_tiles == 1

    def weight_tile(j):
        tile = col_tiles[-1]
        for idx in range(n_col_tiles - 2, -1, -1):
            tile = jnp.where(j == idx, col_tiles[idx], tile)
        return tile

    out_specs = [pl.BlockSpec((tm, tn), lambda j, i: (i, j))]
    out_shape = [jax.ShapeDtypeStruct((m, n_col_tiles * tn), _BF16)]
    if emit_x_bf16:
        out_specs.append(pl.BlockSpec((tm, k), lambda j, i: (i, 0)))
        out_shape.append(jax.ShapeDtypeStruct((m, k), _BF16))
    outs = pl.pallas_call(
        functools.partial(_project_kernel, scale=scale, activation=activation, row_chunks=row_chunks,
                          emit_x_bf16=emit_x_bf16),
        grid=(n_col_tiles, m // tm),
        in_specs=[pl.BlockSpec((tm, k), lambda j, i: (i, 0)),
                  pl.BlockSpec((k, tn), lambda j, i: (0, weight_tile(j)))],
        out_specs=out_specs,
        out_shape=out_shape,
        scratch_shapes=[pltpu.VMEM((k, tn), _BF16)],
        compiler_params=pltpu.CompilerParams(
            dimension_semantics=("arbitrary", "arbitrary"), vmem_limit_bytes=VMEM_LIMIT_BYTES),
        name=name,
    )(x, w)
    return outs if emit_x_bf16 else outs[0]


def _cast_job_plan(arrays, n_steps, linear_step):
    specs, shapes, blocks, first = [], [], [], 0
    for a in arrays:
        assert a.shape[0] % CAST_JOB_ROWS == 0
        n_blocks = a.shape[0] // CAST_JOB_ROWS
        specs.append(pl.BlockSpec(
            (CAST_JOB_ROWS, a.shape[1]),
            lambda *idx, first=first, n_blocks=n_blocks: (jnp.clip(linear_step(*idx) - first, 0, n_blocks - 1), 0)))
        shapes.append(jax.ShapeDtypeStruct(a.shape, _BF16))
        blocks.append(n_blocks)
        first += n_blocks
    assert first <= n_steps
    return specs, shapes, tuple(blocks)


def _run_cast_jobs(step, job_in, job_out, blocks):
    first = 0
    for src, dst, n_blocks in zip(job_in, job_out, blocks):
        @pl.when((step >= first) & (step < first + n_blocks))
        def _cast_job_block(src=src, dst=dst):
            dst[...] = src[...].astype(_BF16)
        first += n_blocks


MOBA_HEADS_PER_STEP = 4
MOBA_Q_BLOCKS_PER_STEP = 2
_ACC_ROWS = ATT_HEAD_DIM + 16
MOBA_Q_PRESCALE = ATT_HEAD_DIM ** -0.5 * _LOG2E


def _moba_kernel(q_ref, k_ref, v_ref, z_ref, *rest, n_blocks, heads, cast_job_blocks):
    n_jobs = len(cast_job_blocks)
    job_in, o_ref, job_out = rest[:n_jobs], rest[n_jobs], rest[n_jobs + 1:2 * n_jobs + 1]
    vt_ref, kmean_ref, neg_ref, pos_ref, qt_ref, s_ref, acc_ref = rest[2 * n_jobs + 1:]
    blk = MOBA_BLOCK
    hd = ATT_HEAD_DIM
    step = (pl.program_id(0) * pl.num_programs(1) + pl.program_id(1)) * pl.num_programs(2) + pl.program_id(2)
    _run_cast_jobs(step, job_in, job_out, cast_job_blocks)

    @pl.when(pl.program_id(2) == 0)
    def _per_head_setup():
        for g in range(heads):
            cols = slice(g * hd, (g + 1) * hd)
            for j in range(n_blocks):
                rows = slice(j * blk, (j + 1) * blk)
                vt_ref[g, j, :hd, :] = v_ref[rows, cols].T
                vt_ref[g, j, hd:, :] = jnp.ones((_ACC_ROWS - hd, blk), _BF16)
                kmean_ref[g, j:j + 1, :] = (
                    jnp.sum(k_ref[rows, cols].astype(_F32), axis=0, keepdims=True) * (1.0 / blk))

    for sub in range(MOBA_Q_BLOCKS_PER_STEP):
        rows = slice(sub * blk, (sub + 1) * blk)
        _moba_query_block(pl.program_id(2) * MOBA_Q_BLOCKS_PER_STEP + sub, q_ref.at[rows], z_ref.at[rows],
                          o_ref.at[rows], k_ref, vt_ref, kmean_ref, neg_ref, pos_ref, qt_ref, s_ref, acc_ref,
                          n_blocks=n_blocks, heads=heads)


def _moba_query_block(qi, q_ref, z_ref, o_ref, k_ref, vt_ref, kmean_ref, neg_ref, pos_ref, qt_ref, s_ref, acc_ref,
                      *, n_blocks, heads):
    blk = MOBA_BLOCK
    hd = ATT_HEAD_DIM
    head_cols = [slice(g * hd, (g + 1) * hd) for g in range(heads)]
    qs = [q_ref[:, c] for c in head_cols]
    for g in range(heads):
        qt_ref[g] = qs[g].T

    def issue_scores(j, slot):
        start = j * blk if isinstance(j, int) else pl.multiple_of(j * blk, blk)
        for g in range(heads):
            s_ref[slot, g] = jnp.dot(k_ref[pl.ds(start, blk), head_cols[g]], qt_ref[g],
                                     preferred_element_type=_F32)

    sel_scores = []
    for g in range(heads):
        km = kmean_ref[g]
        km_hi = km.astype(_BF16)
        km_lo = (km - km_hi.astype(_F32)).astype(_BF16)
        sel_scores.append(jnp.dot(km_hi, qt_ref[g], preferred_element_type=_F32)
                          + jnp.dot(km_lo, qt_ref[g], preferred_element_type=_F32))
    issue_scores(qi, 2)
    row_id = lax.broadcasted_iota(jnp.int32, (n_blocks, blk), 0)
    past = row_id < qi
    later_rows = [jnp.where(row_id > j, 1.0, 0.0) for j in range(n_blocks)]
    for g in range(heads):
        sb = jnp.where(past, sel_scores[g], -jnp.inf)
        rank = jnp.zeros(sb.shape, _F32)
        for j in range(n_blocks):
            other = sb[j:j + 1, :]
            rank = rank + jnp.where(other > sb, 1.0, 0.0) + jnp.where(other == sb, later_rows[j], 0.0)
        selected = past & (rank < MOBA_TOPK)
        neg_ref[g] = jnp.where(selected, 0.0, -jnp.inf)
        pos_ref[g] = jnp.where(selected, -jnp.inf, jnp.inf)

    def accumulate(j, slot, ms, causal):
        probs, m_news = [], []
        for g in range(heads):
            s = s_ref[slot, g]
            if causal:
                key_pos = lax.broadcasted_iota(jnp.int32, s.shape, 0)
                q_pos = lax.broadcasted_iota(jnp.int32, s.shape, 1)
                s = jnp.where(key_pos <= q_pos, s, -jnp.inf)
                m_new = jnp.maximum(ms[g], jnp.max(s, axis=0, keepdims=True))
                m_sub = m_new
            else:
                m_new = jnp.maximum(ms[g], jnp.max(s, axis=0, keepdims=True) + neg_ref[g, pl.ds(j, 1), :])
                m_sub = jnp.maximum(m_new, pos_ref[g, pl.ds(j, 1), :])
            probs.append(jnp.exp2(s - m_sub).astype(_BF16))
            m_news.append(m_new)
        for g in range(heads):
            pv = jnp.dot(vt_ref[g, j], probs[g], preferred_element_type=_F32)
            if causal:
                acc_ref[g] = pv
            else:
                acc_ref[g] = jnp.exp2(ms[g] - m_news[g]) * acc_ref[g] + pv
        return m_news

    def past_block_pair(t, ms):
        issue_scores(2 * t + 1, 1)
        ms = accumulate(2 * t, 0, ms, causal=False)
        issue_scores(2 * t + 2, 0)
        return accumulate(2 * t + 1, 1, ms, causal=False)

    issue_scores(0, 0)
    ms = accumulate(qi, 2, [jnp.full((1, blk), -jnp.inf, _F32)] * heads, causal=True)
    ms = lax.fori_loop(0, qi // 2, past_block_pair, ms)

    @pl.when(qi % 2 == 1)
    def _last_past_block():
        accumulate(qi - 1, 0, ms, causal=False)

    for g in range(heads):
        attn = (acc_ref[g, :hd, :] / acc_ref[g, hd:hd + 1, :]).T
        o_ref[:, head_cols[g]] = (attn * z_ref[:, head_cols[g]].astype(_F32)).astype(o_ref.dtype)


def _moba_attention(q, plain, gated, batch, seq, *, heads=MOBA_HEADS_PER_STEP, cast_jobs=()):
    n_blocks = seq // MOBA_BLOCK
    q3 = q.reshape(batch, seq, ATT_WIDTH)
    plain3 = plain.reshape(batch, seq, -1)
    gated3 = gated.reshape(batch, seq, -1)
    width = heads * ATT_HEAD_DIM
    cols = PROJ_TILE // width
    q_rows = MOBA_Q_BLOCKS_PER_STEP * MOBA_BLOCK
    assert n_blocks % MOBA_Q_BLOCKS_PER_STEP == 0
    grid = (batch, ATT_WIDTH // width, n_blocks // MOBA_Q_BLOCKS_PER_STEP)
    job_specs, job_shapes, job_blocks = _cast_job_plan(
        cast_jobs, grid[0] * grid[1] * grid[2], lambda b, h, i: (b * grid[1] + h) * grid[2] + i)
    return pl.pallas_call(
        functools.partial(_moba_kernel, n_blocks=n_blocks, heads=heads, cast_job_blocks=job_blocks),
        grid=grid,
        in_specs=[pl.BlockSpec((None, q_rows, width), lambda b, h, i: (b, i, h)),
                  pl.BlockSpec((None, seq, width), lambda b, h, i: (b, 0, K_A * cols + h)),
                  pl.BlockSpec((None, seq, width), lambda b, h, i: (b, 0, V_A * cols + h)),
                  pl.BlockSpec((None, q_rows, width), lambda b, h, i: (b, i, Z_A * cols + h)),
                  *job_specs],
        out_specs=[pl.BlockSpec((None, q_rows, width), lambda b, h, i: (b, i, h)), *job_specs],
        out_shape=[jax.ShapeDtypeStruct((batch, seq, ATT_WIDTH), _BF16), *job_shapes],
        scratch_shapes=[pltpu.VMEM((heads, n_blocks, _ACC_ROWS, MOBA_BLOCK), _BF16),
                        pltpu.VMEM((heads, n_blocks, ATT_HEAD_DIM), _F32),
                        pltpu.VMEM((heads, n_blocks, MOBA_BLOCK), _F32),
                        pltpu.VMEM((heads, n_blocks, MOBA_BLOCK), _F32),
                        pltpu.VMEM((heads, ATT_HEAD_DIM, MOBA_BLOCK), _BF16),
                        pltpu.VMEM((3, heads, MOBA_BLOCK, MOBA_BLOCK), _F32),
                        pltpu.VMEM((heads, _ACC_ROWS, MOBA_BLOCK), _F32)],
        compiler_params=pltpu.CompilerParams(
            dimension_semantics=("arbitrary", "arbitrary", "arbitrary"), vmem_limit_bytes=VMEM_LIMIT_BYTES),
        name="moba_attention",
    )(q3, plain3, plain3, gated3, *cast_jobs)


def _sgu_causal_weights(w_ref):
    t_pos = lax.broadcasted_iota(jnp.int32, (SGU_CHUNK, SGU_CHUNK), 0)
    s_pos = lax.broadcasted_iota(jnp.int32, (SGU_CHUNK, SGU_CHUNK), 1)
    return [jnp.where(s_pos <= t_pos, w_ref[g], 0.0).astype(_BF16) for g in range(SGU_GROUPS)]


def _sgu_rows(u_ref, v_ref, z_ref, w_causal, bt_ref, g_ref, b_ref, rows, yg_ref):
    u = u_ref[rows, :].astype(_F32)
    v = v_ref[rows, :].astype(_F32)
    mu = jnp.mean(v, axis=-1, keepdims=True)
    d = v - mu
    var = jnp.mean(d * d, axis=-1, keepdims=True)
    vn = (d * lax.rsqrt(var + LN_EPS) * g_ref[...] + b_ref[...]).astype(_BF16)
    gate = u * z_ref[rows, :].astype(_F32)
    for g in range(SGU_GROUPS):
        bias = bt_ref[:, g:g + 1]
        cols = slice(g * SGU_GROUP_DIM, (g + 1) * SGU_GROUP_DIM)
        for c in range((rows.stop - rows.start) // SGU_CHUNK):
            sub = slice(c * SGU_CHUNK, (c + 1) * SGU_CHUNK)
            mixed = jnp.dot(w_causal[g], vn[sub, cols], preferred_element_type=_F32) + bias
            yg_ref[rows.start + c * SGU_CHUNK:rows.start + (c + 1) * SGU_CHUNK, cols] = (
                gate[sub, cols] * mixed).astype(yg_ref.dtype)


def _mem_attn_kernel(q_ref, z_ref, mem_ref, wk_ref, wv_ref, o_ref, k_ref, v_ref):
    @pl.when(pl.program_id(1) == 0)
    def _project_memory():
        mem_bf = mem_ref[...].astype(_BF16)
        for w_ref, dst in ((wk_ref, k_ref), (wv_ref, v_ref)):
            for c in range(0, MEM_WIDTH, MEM_HEAD_DIM):
                dst[:, c:c + MEM_HEAD_DIM] = jnp.dot(
                    mem_bf, w_ref[:, c:c + MEM_HEAD_DIM].astype(_BF16), preferred_element_type=_F32
                ).astype(_BF16)

    scale_log2e = MEM_HEAD_DIM ** -0.5 * _LOG2E
    for h in range(MEM_HEADS):
        cols = slice(h * MEM_HEAD_DIM, (h + 1) * MEM_HEAD_DIM)
        s = lax.dot_general(q_ref[:, cols], k_ref[:, cols], _NT, preferred_element_type=_F32)
        p = jnp.exp2((s - jnp.max(s, axis=-1, keepdims=True)) * scale_log2e)
        l = jnp.sum(p, axis=-1, keepdims=True)
        o = jnp.dot(p.astype(_BF16), v_ref[:, cols], preferred_element_type=_F32) / l
        o_ref[:, cols] = (o * z_ref[:, cols].astype(_F32)).astype(o_ref.dtype)


def _memory_attention(plain, gated, mem, w_mem_k, w_mem_v, seq, *, tm=1024):
    batch, n_mem, d = mem.shape
    plain3 = plain.reshape(batch, seq, -1)
    gated3 = gated.reshape(batch, seq, -1)
    weight = pl.BlockSpec((d, MEM_WIDTH), lambda b, i: (0, 0), pipeline_mode=pl.Buffered(1))
    return pl.pallas_call(
        _mem_attn_kernel,
        grid=(batch, seq // tm),
        in_specs=[pl.BlockSpec((None, tm, MEM_WIDTH), lambda b, i: (b, i, Q_C)),
                  pl.BlockSpec((None, tm, MEM_WIDTH), lambda b, i: (b, i, Z_C)),
                  pl.BlockSpec((None, n_mem, d), lambda b, i: (b, 0, 0)),
                  weight, weight],
        out_specs=pl.BlockSpec((None, tm, MEM_WIDTH), lambda b, i: (b, i, 0)),
        out_shape=jax.ShapeDtypeStruct((batch, seq, MEM_WIDTH), _BF16),
        scratch_shapes=[pltpu.VMEM((n_mem, MEM_WIDTH), _BF16), pltpu.VMEM((n_mem, MEM_WIDTH), _BF16)],
        compiler_params=pltpu.CompilerParams(
            dimension_semantics=("arbitrary", "arbitrary"), vmem_limit_bytes=VMEM_LIMIT_BYTES),
        name="memory_attention",
    )(plain3, gated3, mem, w_mem_k, w_mem_v)


MERGE_TM = 512
OUT_NORM_TM = 512
OUT_NORM_CHUNKS = (256, 128, 128)
MERGE_ROW_CHUNK = 256


def _branch_merge_kernel(ya_ref, yc_ref, uv_ref, z_ref, ws_ref, bt_ref, lvg_ref, lvb_ref, ga_ref, gg_ref, gc_ref,
                         wa_ref, wg_ref, wc_ref, o_ref, yg_ref):
    u_ref, v_ref = (uv_ref.at[:, k * PROJ_TILE:(k + 1) * PROJ_TILE] for k in (U_G, V_G))
    per_branch = D_MODEL // PROJ_TILE
    col_tiles = [slice(c * PROJ_TILE, (c + 1) * PROJ_TILE) for c in range(per_branch)]
    chunks = [slice(r * MERGE_ROW_CHUNK, (r + 1) * MERGE_ROW_CHUNK)
              for r in range(o_ref.shape[0] // MERGE_ROW_CHUNK)]
    w_causal = _sgu_causal_weights(ws_ref)

    def branch_dots(rows):
        n_sub = len(col_tiles)
        sub_rows = (rows.stop - rows.start) // n_sub
        outer = []
        for c, cols in enumerate(col_tiles):
            outer.append((jnp.dot(ya_ref[rows, :], wa_ref[:, cols], preferred_element_type=_F32),
                          jnp.dot(yc_ref[rows, :], wc_ref[:, cols], preferred_element_type=_F32)))
            sub = slice(rows.start + c * sub_rows, rows.start + (c + 1) * sub_rows)
            _sgu_rows(u_ref, v_ref, z_ref, w_causal, bt_ref, lvg_ref, lvb_ref, sub, yg_ref)
        inner = [jnp.dot(yg_ref[rows, :], wg_ref[:, cols], preferred_element_type=_F32) for cols in col_tiles]
        return outer, inner

    dots = branch_dots(chunks[0])
    for k, rows in enumerate(chunks):
        next_dots = branch_dots(chunks[k + 1]) if k + 1 < len(chunks) else None
        outer, inner = dots
        for c, cols in enumerate(col_tiles):
            merged = (ga_ref[rows, cols].astype(_F32) * outer[c][0]
                      + gg_ref[rows, cols].astype(_F32) * inner[c]
                      + gc_ref[rows, cols].astype(_F32) * outer[c][1])
            o_ref[rows, cols] = merged.astype(o_ref.dtype)
        dots = next_dots


def _out_norm_kernel(m_ref, x_ref, wo_ref, lng_ref, lnb_ref, o_ref):
    bounds = [0]
    for size in OUT_NORM_CHUNKS:
        bounds.append(bounds[-1] + size)
    assert bounds[-1] == o_ref.shape[0]
    chunks = [slice(a, b) for a, b in zip(bounds[:-1], bounds[1:])]

    def out_dot(rows):
        return jnp.dot(m_ref[rows, :], wo_ref[...], preferred_element_type=_F32)

    y = out_dot(chunks[0])
    for k, rows in enumerate(chunks):
        y_next = out_dot(chunks[k + 1]) if k + 1 < len(chunks) else None
        h = DN_ALPHA * x_ref[rows, :] + y
        mu = jnp.mean(h, axis=-1, keepdims=True)
        d = h - mu
        var = jnp.mean(d * d, axis=-1, keepdims=True)
        o_ref[rows, :] = d * lax.rsqrt(var + LN_EPS) * lng_ref[...] + lnb_ref[...]
        y = y_next


def _merge_project_norm(ya, yc, gelu_uv, gated, sgu_params, gates, x2, wa, wg, wc, wo, ln_g, ln_b, *, tm=MERGE_TM):
    t = x2.shape[0]
    w_s, b_s, ln_v_g, ln_v_b = sgu_params
    assert t % tm == 0 and tm % MERGE_ROW_CHUNK == 0 and t % OUT_NORM_TM == 0 and OUT_NORM_TM % MERGE_ROW_CHUNK == 0
    branch = pl.BlockSpec((tm, ATT_WIDTH), lambda i: (i, 0))
    gate = lambda br: pl.BlockSpec((tm, D_MODEL), lambda i: (i, br))
    rows = pl.BlockSpec((tm, D_MODEL), lambda i: (i, 0))
    resident = lambda shape: pl.BlockSpec(shape, lambda i: (0, 0), pipeline_mode=pl.Buffered(1))
    params = pltpu.CompilerParams(dimension_semantics=("arbitrary",), vmem_limit_bytes=VMEM_LIMIT_BYTES)
    whole = lambda shape: pl.BlockSpec(shape, lambda i: (0,) * len(shape))
    merged = pl.pallas_call(
        _branch_merge_kernel,
        grid=(t // tm,),
        in_specs=[branch, branch, pl.BlockSpec((tm, 2 * PROJ_TILE), lambda i: (i, 0)),
                  pl.BlockSpec((tm, PROJ_TILE), lambda i: (i, Z_G)),
                  whole((SGU_GROUPS, SGU_CHUNK, SGU_CHUNK)), whole((SGU_CHUNK, SGU_GROUPS)),
                  whole((1, SGU_WIDTH)), whole((1, SGU_WIDTH)),
                  *[gate(br) for br in range(N_BRANCH)],
                  resident((ATT_WIDTH, D_MODEL)), resident((SGU_WIDTH, D_MODEL)), resident((MEM_WIDTH, D_MODEL))],
        out_specs=rows,
        out_shape=jax.ShapeDtypeStruct((t, D_MODEL), _BF16),
        scratch_shapes=[pltpu.VMEM((tm, SGU_WIDTH), _BF16)],
        compiler_params=params,
        name="branch_merge",
    )(ya, yc, gelu_uv, gated, w_s, b_s.T, ln_v_g.reshape(1, -1), ln_v_b.reshape(1, -1),
      *([gates] * N_BRANCH), wa, wg, wc)
    out_rows = pl.BlockSpec((OUT_NORM_TM, D_MODEL), lambda i: (i, 0))
    return pl.pallas_call(
        _out_norm_kernel,
        grid=(t // OUT_NORM_TM,),
        in_specs=[out_rows, out_rows,
                  resident((D_MODEL, D_MODEL)), resident((1, D_MODEL)), resident((1, D_MODEL))],
        out_specs=out_rows,
        out_shape=jax.ShapeDtypeStruct((t, D_MODEL), _F32),
        compiler_params=params,
        name="out_proj_norm",
    )(merged, x2, wo, ln_g.reshape(1, -1), ln_b.reshape(1, -1))


def kernel(x, mem, w_in, w_mem_k, w_mem_v, w_s, b_s, ln_v_g, ln_v_b,
           w_branch_attn, w_branch_sgu, w_branch_mem, w_out, ln_g, ln_b):
    batch, seq, d = x.shape
    assert d == D_MODEL and seq % MOBA_BLOCK == 0 and mem.shape[1] == N_MEM
    t = batch * seq
    x2 = x.reshape(t, d)

    assert w_in.shape[1] == GATE_TILE0 * PROJ_TILE + N_BRANCH * D_MODEL and ATT_WIDTH == PROJ_TILE
    q_a, x_bf = _project(x2, w_in, tm=1024, tn=PROJ_TILE, name="in_proj_q", col_tiles=(0,),
                         scale=MOBA_Q_PRESCALE, row_chunks=4, emit_x_bf16=True)
    in_proj = functools.partial(_project, x_bf, w_in, tm=IN_PROJ_TM, tn=PROJ_TILE, row_chunks=IN_PROJ_ROW_CHUNKS)
    plain = in_proj(name="in_proj_plain", col_tiles=PLAIN_TILES)
    gated = in_proj(name="in_proj_silu", col_tiles=SILU_TILES, activation="silu")
    gelu_uv = in_proj(name="in_proj_gelu", col_tiles=GELU_TILES, activation="gelu")
    gates = in_proj(name="in_proj_gates", col_tiles=tuple(range(GATE_TILE0, w_in.shape[1] // PROJ_TILE)),
                    activation="sigmoid")

    ya, wa, wg, wc, wo = _moba_attention(q_a, plain, gated, batch, seq,
                                         cast_jobs=(w_branch_attn, w_branch_sgu, w_branch_mem, w_out))
    ya = ya.reshape(t, ATT_WIDTH)
    yc = _memory_attention(plain, gated, mem, w_mem_k, w_mem_v, seq).reshape(t, MEM_WIDTH)

    out = _merge_project_norm(ya, yc, gelu_uv, gated, (w_s, b_s, ln_v_g, ln_v_b), gates, x2,
                              wa, wg, wc, wo, ln_g, ln_b)
    return out.reshape(batch, seq, d)
```

```python
import functools

import jax
import jax.numpy as jnp
from jax import lax
from jax.experimental import pallas as pl
from jax.experimental.pallas import tpu as pltpu

D_MODEL = 2048
DEPTH = 1
N_MEM = 256
ATT_HEAD_DIM = 128
ATT_WIDTH = D_MODEL // 2
MOBA_BLOCK = 256
MOBA_TOPK = 3
SGU_WIDTH = D_MODEL // 2
SGU_CHUNK = 128
SGU_GROUP_DIM = 128
SGU_GROUPS = SGU_WIDTH // SGU_GROUP_DIM
MEM_HEADS = 4
MEM_WIDTH = D_MODEL // 2
MEM_HEAD_DIM = MEM_WIDTH // MEM_HEADS
N_BRANCH = 3
DN_ALPHA = (2 * DEPTH) ** 0.25
LN_EPS = 1e-5

PROJ_TILE = 1024
PROJ_TILE0 = 1
GATE_TILE0 = 9
PROJ_WIDTH = (GATE_TILE0 - PROJ_TILE0) * PROJ_TILE
K_A, V_A, Z_A, U_G, V_G, Z_G, Q_C, Z_C = range(8)

VMEM_LIMIT_BYTES = 56 * 1024 * 1024
IN_PROJ_TM = 2048
IN_PROJ_ROW_CHUNKS = 8
CAST_JOB_ROWS = 256

_NT = (((1,), (1,)), ((), ()))
_F32 = jnp.float32
_BF16 = jnp.bfloat16


_LOG2E = 1.4426950408889634


def _sigmoid(x):
    return 1.0 / (1.0 + jnp.exp2(x * -_LOG2E))


def _silu(x):
    return x * _sigmoid(x)


def _gelu_tanh(x):
    a = -2.0 * 0.7978845608028654 * _LOG2E
    return x / (1.0 + jnp.exp2(x * (a + (a * 0.044715) * (x * x))))


def _project_kernel(x_ref, w_ref, o_ref, *rest, scale, sigmoid, row_chunks, emit_x_bf16):
    x_bf_ref = rest[0] if emit_x_bf16 else None
    w_bf_ref = rest[-1]

    @pl.when(pl.program_id(1) == 0)
    def _cast_weight_tile():
        w_bf_ref[...] = w_ref[...].astype(_BF16)

    tc = x_ref.shape[0] // row_chunks
    for c in range(row_chunks):
        rows = slice(c * tc, (c + 1) * tc)
        xc = x_ref[rows, :]
        if emit_x_bf16:
            xc = xc.astype(_BF16)
            x_bf_ref[rows, :] = xc
        acc = jnp.dot(xc, w_bf_ref[...], preferred_element_type=_F32)
        if sigmoid:
            acc = _sigmoid(acc)
        if scale is not None:
            acc = acc * scale
        o_ref[rows, :] = acc.astype(o_ref.dtype)


def _project(x, w, *, tm, tn, name, first_col_tile=0, n_col_tiles=None,
             scale=None, sigmoid=False, row_chunks=1, emit_x_bf16=False):
    m, k = x.shape
    if n_col_tiles is None:
        n_col_tiles = w.shape[1] // tn - first_col_tile
    assert m % tm == 0 and tm % row_chunks == 0 and (first_col_tile + n_col_tiles) * tn <= w.shape[1]
    assert not emit_x_bf16 or n_col_tiles == 1
    out_specs = [pl.BlockSpec((tm, tn), lambda j, i: (i, j))]
    out_shape = [jax.ShapeDtypeStruct((m, n_col_tiles * tn), _BF16)]
    if emit_x_bf16:
        out_specs.append(pl.BlockSpec((tm, k), lambda j, i: (i, 0)))
        out_shape.append(jax.ShapeDtypeStruct((m, k), _BF16))
    outs = pl.pallas_call(
        functools.partial(_project_kernel, scale=scale, sigmoid=sigmoid, row_chunks=row_chunks,
                          emit_x_bf16=emit_x_bf16),
        grid=(n_col_tiles, m // tm),
        in_specs=[pl.BlockSpec((tm, k), lambda j, i: (i, 0)),
                  pl.BlockSpec((k, tn), lambda j, i: (0, first_col_tile + j))],
        out_specs=out_specs,
        out_shape=out_shape,
        scratch_shapes=[pltpu.VMEM((k, tn), _BF16)],
        compiler_params=pltpu.CompilerParams(
            dimension_semantics=("arbitrary", "arbitrary"), vmem_limit_bytes=VMEM_LIMIT_BYTES),
        name=name,
    )(x, w)
    return outs if emit_x_bf16 else outs[0]


def _cast_job_plan(arrays, n_steps, linear_step):
    specs, shapes, blocks, first = [], [], [], 0
    for a in arrays:
        assert a.shape[0] % CAST_JOB_ROWS == 0
        n_blocks = a.shape[0] // CAST_JOB_ROWS
        specs.append(pl.BlockSpec(
            (CAST_JOB_ROWS, a.shape[1]),
            lambda *idx, first=first, n_blocks=n_blocks: (jnp.clip(linear_step(*idx) - first, 0, n_blocks - 1), 0)))
        shapes.append(jax.ShapeDtypeStruct(a.shape, _BF16))
        blocks.append(n_blocks)
        first += n_blocks
    assert first <= n_steps
    return specs, shapes, tuple(blocks)


def _run_cast_jobs(step, job_in, job_out, blocks):
    first = 0
    for src, dst, n_blocks in zip(job_in, job_out, blocks):
        @pl.when((step >= first) & (step < first + n_blocks))
        def _cast_job_block(src=src, dst=dst):
            dst[...] = src[...].astype(_BF16)
        first += n_blocks


MOBA_HEADS_PER_STEP = 4
MOBA_Q_BLOCKS_PER_STEP = 2
_ACC_ROWS = ATT_HEAD_DIM + 16
MOBA_Q_PRESCALE = ATT_HEAD_DIM ** -0.5 * _LOG2E


def _moba_kernel(q_ref, k_ref, v_ref, z_ref, *rest, n_blocks, heads, cast_job_blocks):
    n_jobs = len(cast_job_blocks)
    job_in, o_ref, job_out = rest[:n_jobs], rest[n_jobs], rest[n_jobs + 1:2 * n_jobs + 1]
    vt_ref, kmean_ref, neg_ref, pos_ref, qt_ref, s_ref, acc_ref, smax_ref = rest[2 * n_jobs + 1:]
    blk = MOBA_BLOCK
    hd = ATT_HEAD_DIM
    step = (pl.program_id(0) * pl.num_programs(1) + pl.program_id(1)) * pl.num_programs(2) + pl.program_id(2)
    _run_cast_jobs(step, job_in, job_out, cast_job_blocks)

    @pl.when(pl.program_id(2) == 0)
    def _per_head_setup():
        for g in range(heads):
            cols = slice(g * hd, (g + 1) * hd)
            for j in range(n_blocks):
                rows = slice(j * blk, (j + 1) * blk)
                vt_ref[g, j, :hd, :] = v_ref[rows, cols].T
                vt_ref[g, j, hd:, :] = jnp.ones((_ACC_ROWS - hd, blk), _BF16)
                kmean_ref[g, j:j + 1, :] = (
                    jnp.sum(k_ref[rows, cols].astype(_F32), axis=0, keepdims=True) * (1.0 / blk))

    for sub in range(MOBA_Q_BLOCKS_PER_STEP):
        rows = slice(sub * blk, (sub + 1) * blk)
        _moba_query_block(pl.program_id(2) * MOBA_Q_BLOCKS_PER_STEP + sub, q_ref.at[rows], z_ref.at[rows],
                          o_ref.at[rows], k_ref, vt_ref, kmean_ref, neg_ref, pos_ref, qt_ref, s_ref, acc_ref, smax_ref,
                          n_blocks=n_blocks, heads=heads)


def _moba_query_block(qi, q_ref, z_ref, o_ref, k_ref, vt_ref, kmean_ref, neg_ref, pos_ref, qt_ref, s_ref, acc_ref, smax_ref,
                      *, n_blocks, heads):
    blk = MOBA_BLOCK
    hd = ATT_HEAD_DIM
    head_cols = [slice(g * hd, (g + 1) * hd) for g in range(heads)]
    qs = [q_ref[:, c] for c in head_cols]
    for g in range(heads):
        qt_ref[g] = qs[g].T

    def issue_scores(j, slot):
        start = j * blk if isinstance(j, int) else pl.multiple_of(j * blk, blk)
        for g in range(heads):
            scores = jnp.dot(k_ref[pl.ds(start, blk), head_cols[g]], qt_ref[g],
                             preferred_element_type=_F32)
            s_ref[slot, g] = scores
            smax_ref[slot * heads + g:slot * heads + g + 1, :] = jnp.max(scores, axis=0, keepdims=True)

    sel_scores = []
    for g in range(heads):
        km = kmean_ref[g]
        km_hi = km.astype(_BF16)
        km_lo = (km - km_hi.astype(_F32)).astype(_BF16)
        sel_scores.append(jnp.dot(km_hi, qt_ref[g], preferred_element_type=_F32)
                          + jnp.dot(km_lo, qt_ref[g], preferred_element_type=_F32))
    issue_scores(qi, 2)
    row_id = lax.broadcasted_iota(jnp.int32, (n_blocks, blk), 0)
    past = row_id < qi
    later_rows = [jnp.where(row_id > j, 1.0, 0.0) for j in range(n_blocks)]
    for g in range(heads):
        sb = jnp.where(past, sel_scores[g], -jnp.inf)
        rank = jnp.zeros(sb.shape, _F32)
        for j in range(n_blocks):
            other = sb[j:j + 1, :]
            rank = rank + jnp.where(other > sb, 1.0, 0.0) + jnp.where(other == sb, later_rows[j], 0.0)
        selected = past & (rank < MOBA_TOPK)
        neg_ref[g] = jnp.where(selected, 0.0, -jnp.inf)
        pos_ref[g] = jnp.where(selected, -jnp.inf, jnp.inf)

    def accumulate(j, slot, ms, causal):
        probs, m_news = [], []
        for g in range(heads):
            s = s_ref[slot, g]
            if causal:
                key_pos = lax.broadcasted_iota(jnp.int32, s.shape, 0)
                q_pos = lax.broadcasted_iota(jnp.int32, s.shape, 1)
                s = jnp.where(key_pos <= q_pos, s, -jnp.inf)
                m_new = jnp.maximum(ms[g], jnp.max(s, axis=0, keepdims=True))
                m_sub = m_new
            else:
                block_max = smax_ref[slot * heads + g:slot * heads + g + 1, :]
                m_new = jnp.maximum(ms[g], block_max + neg_ref[g, pl.ds(j, 1), :])
                m_sub = jnp.maximum(m_new, pos_ref[g, pl.ds(j, 1), :])
            probs.append(jnp.exp2(s - m_sub).astype(_BF16))
            m_news.append(m_new)
        for g in range(heads):
            pv = jnp.dot(vt_ref[g, j], probs[g], preferred_element_type=_F32)
            if causal:
                acc_ref[g] = pv
            else:
                acc_ref[g] = jnp.exp2(ms[g] - m_news[g]) * acc_ref[g] + pv
        return m_news

    def past_block_pair(t, ms):
        issue_scores(2 * t + 1, 1)
        ms = accumulate(2 * t, 0, ms, causal=False)
        issue_scores(2 * t + 2, 0)
        return accumulate(2 * t + 1, 1, ms, causal=False)

    issue_scores(0, 0)
    ms = accumulate(qi, 2, [jnp.full((1, blk), -jnp.inf, _F32)] * heads, causal=True)
    ms = lax.fori_loop(0, qi // 2, past_block_pair, ms)

    @pl.when(qi % 2 == 1)
    def _last_past_block():
        accumulate(qi - 1, 0, ms, causal=False)

    for g in range(heads):
        attn = (acc_ref[g, :hd, :] / acc_ref[g, hd:hd + 1, :]).T
        o_ref[:, head_cols[g]] = (attn * _silu(z_ref[:, head_cols[g]].astype(_F32))).astype(o_ref.dtype)


def _moba_attention(q, proj, batch, seq, *, heads=MOBA_HEADS_PER_STEP, cast_jobs=()):
    n_blocks = seq // MOBA_BLOCK
    q3 = q.reshape(batch, seq, ATT_WIDTH)
    proj3 = proj.reshape(batch, seq, PROJ_WIDTH)
    width = heads * ATT_HEAD_DIM
    cols = PROJ_TILE // width
    q_rows = MOBA_Q_BLOCKS_PER_STEP * MOBA_BLOCK
    assert n_blocks % MOBA_Q_BLOCKS_PER_STEP == 0
    grid = (batch, ATT_WIDTH // width, n_blocks // MOBA_Q_BLOCKS_PER_STEP)
    job_specs, job_shapes, job_blocks = _cast_job_plan(
        cast_jobs, grid[0] * grid[1] * grid[2], lambda b, h, i: (b * grid[1] + h) * grid[2] + i)
    return pl.pallas_call(
        functools.partial(_moba_kernel, n_blocks=n_blocks, heads=heads, cast_job_blocks=job_blocks),
        grid=grid,
        in_specs=[pl.BlockSpec((None, q_rows, width), lambda b, h, i: (b, i, h)),
                  pl.BlockSpec((None, seq, width), lambda b, h, i: (b, 0, K_A * cols + h)),
                  pl.BlockSpec((None, seq, width), lambda b, h, i: (b, 0, V_A * cols + h)),
                  pl.BlockSpec((None, q_rows, width), lambda b, h, i: (b, i, Z_A * cols + h)),
                  *job_specs],
        out_specs=[pl.BlockSpec((None, q_rows, width), lambda b, h, i: (b, i, h)), *job_specs],
        out_shape=[jax.ShapeDtypeStruct((batch, seq, ATT_WIDTH), _BF16), *job_shapes],
        scratch_shapes=[pltpu.VMEM((heads, n_blocks, _ACC_ROWS, MOBA_BLOCK), _BF16),
                        pltpu.VMEM((heads, n_blocks, ATT_HEAD_DIM), _F32),
                        pltpu.VMEM((heads, n_blocks, MOBA_BLOCK), _F32),
                        pltpu.VMEM((heads, n_blocks, MOBA_BLOCK), _F32),
                        pltpu.VMEM((heads, ATT_HEAD_DIM, MOBA_BLOCK), _BF16),
                        pltpu.VMEM((3, heads, MOBA_BLOCK, MOBA_BLOCK), _F32),
                        pltpu.VMEM((heads, _ACC_ROWS, MOBA_BLOCK), _F32),
                        pltpu.VMEM((3 * heads, MOBA_BLOCK), _F32)],
        compiler_params=pltpu.CompilerParams(
            dimension_semantics=("arbitrary", "arbitrary", "arbitrary"), vmem_limit_bytes=VMEM_LIMIT_BYTES),
        name="moba_attention",
    )(q3, proj3, proj3, proj3, *cast_jobs)


def _sgu_causal_weights(w_ref):
    t_pos = lax.broadcasted_iota(jnp.int32, (SGU_CHUNK, SGU_CHUNK), 0)
    s_pos = lax.broadcasted_iota(jnp.int32, (SGU_CHUNK, SGU_CHUNK), 1)
    return [jnp.where(s_pos <= t_pos, w_ref[g], 0.0).astype(_BF16) for g in range(SGU_GROUPS)]


def _sgu_rows(u_ref, v_ref, z_ref, w_causal, bt_ref, g_ref, b_ref, rows, yg_ref):
    u = _gelu_tanh(u_ref[rows, :].astype(_F32))
    v = _gelu_tanh(v_ref[rows, :].astype(_F32))
    mu = jnp.mean(v, axis=-1, keepdims=True)
    d = v - mu
    var = jnp.mean(d * d, axis=-1, keepdims=True)
    vn = (d * lax.rsqrt(var + LN_EPS) * g_ref[...] + b_ref[...]).astype(_BF16)
    gate = u * _silu(z_ref[rows, :].astype(_F32))
    for g in range(SGU_GROUPS):
        bias = bt_ref[:, g:g + 1]
        cols = slice(g * SGU_GROUP_DIM, (g + 1) * SGU_GROUP_DIM)
        for c in range((rows.stop - rows.start) // SGU_CHUNK):
            sub = slice(c * SGU_CHUNK, (c + 1) * SGU_CHUNK)
            mixed = jnp.dot(w_causal[g], vn[sub, cols], preferred_element_type=_F32) + bias
            yg_ref[rows.start + c * SGU_CHUNK:rows.start + (c + 1) * SGU_CHUNK, cols] = (
                gate[sub, cols] * mixed).astype(yg_ref.dtype)


def _mem_attn_kernel(q_ref, z_ref, mem_ref, wk_ref, wv_ref, o_ref, k_ref, v_ref):
    @pl.when(pl.program_id(1) == 0)
    def _project_memory():
        mem_bf = mem_ref[...].astype(_BF16)
        for w_ref, dst in ((wk_ref, k_ref), (wv_ref, v_ref)):
            for c in range(0, MEM_WIDTH, MEM_HEAD_DIM):
                dst[:, c:c + MEM_HEAD_DIM] = jnp.dot(
                    mem_bf, w_ref[:, c:c + MEM_HEAD_DIM].astype(_BF16), preferred_element_type=_F32
                ).astype(_BF16)

    scale_log2e = MEM_HEAD_DIM ** -0.5 * _LOG2E
    for h in range(MEM_HEADS):
        cols = slice(h * MEM_HEAD_DIM, (h + 1) * MEM_HEAD_DIM)
        s = lax.dot_general(q_ref[:, cols], k_ref[:, cols], _NT, preferred_element_type=_F32)
        p = jnp.exp2((s - jnp.max(s, axis=-1, keepdims=True)) * scale_log2e)
        l = jnp.sum(p, axis=-1, keepdims=True)
        o = jnp.dot(p.astype(_BF16), v_ref[:, cols], preferred_element_type=_F32) / l
        o_ref[:, cols] = (o * _silu(z_ref[:, cols].astype(_F32))).astype(o_ref.dtype)


def _memory_attention(proj, mem, w_mem_k, w_mem_v, seq, *, tm=1024):
    batch, n_mem, d = mem.shape
    proj3 = proj.reshape(batch, seq, PROJ_WIDTH)
    weight = pl.BlockSpec((d, MEM_WIDTH), lambda b, i: (0, 0), pipeline_mode=pl.Buffered(1))
    return pl.pallas_call(
        _mem_attn_kernel,
        grid=(batch, seq // tm),
        in_specs=[pl.BlockSpec((None, tm, MEM_WIDTH), lambda b, i: (b, i, Q_C)),
                  pl.BlockSpec((None, tm, MEM_WIDTH), lambda b, i: (b, i, Z_C)),
                  pl.BlockSpec((None, n_mem, d), lambda b, i: (b, 0, 0)),
                  weight, weight],
        out_specs=pl.BlockSpec((None, tm, MEM_WIDTH), lambda b, i: (b, i, 0)),
        out_shape=jax.ShapeDtypeStruct((batch, seq, MEM_WIDTH), _BF16),
        scratch_shapes=[pltpu.VMEM((n_mem, MEM_WIDTH), _BF16), pltpu.VMEM((n_mem, MEM_WIDTH), _BF16)],
        compiler_params=pltpu.CompilerParams(
            dimension_semantics=("arbitrary", "arbitrary"), vmem_limit_bytes=VMEM_LIMIT_BYTES),
        name="memory_attention",
    )(proj3, proj3, mem, w_mem_k, w_mem_v)


MERGE_TM = 512
OUT_NORM_TM = 512
OUT_NORM_CHUNKS = (256, 128, 128)
MERGE_ROW_CHUNK = 256


def _branch_merge_kernel(ya_ref, yc_ref, uvz_ref, ws_ref, bt_ref, lvg_ref, lvb_ref, ga_ref, gg_ref, gc_ref,
                         wa_ref, wg_ref, wc_ref, o_ref, yg_ref):
    u_ref, v_ref, z_ref = (uvz_ref.at[:, k * PROJ_TILE:(k + 1) * PROJ_TILE] for k in range(3))
    per_branch = D_MODEL // PROJ_TILE
    col_tiles = [slice(c * PROJ_TILE, (c + 1) * PROJ_TILE) for c in range(per_branch)]
    chunks = [slice(r * MERGE_ROW_CHUNK, (r + 1) * MERGE_ROW_CHUNK)
              for r in range(o_ref.shape[0] // MERGE_ROW_CHUNK)]
    w_causal = _sgu_causal_weights(ws_ref)

    def branch_dots(rows):
        n_sub = len(col_tiles)
        sub_rows = (rows.stop - rows.start) // n_sub
        outer = []
        for c, cols in enumerate(col_tiles):
            outer.append((jnp.dot(ya_ref[rows, :], wa_ref[:, cols], preferred_element_type=_F32),
                          jnp.dot(yc_ref[rows, :], wc_ref[:, cols], preferred_element_type=_F32)))
            sub = slice(rows.start + c * sub_rows, rows.start + (c + 1) * sub_rows)
            _sgu_rows(u_ref, v_ref, z_ref, w_causal, bt_ref, lvg_ref, lvb_ref, sub, yg_ref)
        inner = [jnp.dot(yg_ref[rows, :], wg_ref[:, cols], preferred_element_type=_F32) for cols in col_tiles]
        return outer, inner

    dots = branch_dots(chunks[0])
    for k, rows in enumerate(chunks):
        next_dots = branch_dots(chunks[k + 1]) if k + 1 < len(chunks) else None
        outer, inner = dots
        for c, cols in enumerate(col_tiles):
            merged = (ga_ref[rows, cols].astype(_F32) * outer[c][0]
                      + gg_ref[rows, cols].astype(_F32) * inner[c]
                      + gc_ref[rows, cols].astype(_F32) * outer[c][1])
            o_ref[rows, cols] = merged.astype(o_ref.dtype)
        dots = next_dots


def _out_norm_kernel(m_ref, x_ref, wo_ref, lng_ref, lnb_ref, o_ref):
    bounds = [0]
    for size in OUT_NORM_CHUNKS:
        bounds.append(bounds[-1] + size)
    assert bounds[-1] == o_ref.shape[0]
    chunks = [slice(a, b) for a, b in zip(bounds[:-1], bounds[1:])]

    def out_dot(rows):
        return jnp.dot(m_ref[rows, :], wo_ref[...], preferred_element_type=_F32)

    y = out_dot(chunks[0])
    for k, rows in enumerate(chunks):
        y_next = out_dot(chunks[k + 1]) if k + 1 < len(chunks) else None
        h = DN_ALPHA * x_ref[rows, :] + y
        mu = jnp.mean(h, axis=-1, keepdims=True)
        d = h - mu
        var = jnp.mean(d * d, axis=-1, keepdims=True)
        o_ref[rows, :] = d * lax.rsqrt(var + LN_EPS) * lng_ref[...] + lnb_ref[...]
        y = y_next


def _merge_project_norm(ya, yc, proj, sgu_params, gates, x2, wa, wg, wc, wo, ln_g, ln_b, *, tm=MERGE_TM):
    t = x2.shape[0]
    w_s, b_s, ln_v_g, ln_v_b = sgu_params
    assert t % tm == 0 and tm % MERGE_ROW_CHUNK == 0 and t % OUT_NORM_TM == 0 and OUT_NORM_TM % MERGE_ROW_CHUNK == 0
    branch = pl.BlockSpec((tm, ATT_WIDTH), lambda i: (i, 0))
    gate = lambda br: pl.BlockSpec((tm, D_MODEL), lambda i: (i, br))
    rows = pl.BlockSpec((tm, D_MODEL), lambda i: (i, 0))
    resident = lambda shape: pl.BlockSpec(shape, lambda i: (0, 0), pipeline_mode=pl.Buffered(1))
    params = pltpu.CompilerParams(dimension_semantics=("arbitrary",), vmem_limit_bytes=VMEM_LIMIT_BYTES)
    assert (V_G, Z_G) == (U_G + 1, U_G + 2) and U_G % 3 == 0
    whole = lambda shape: pl.BlockSpec(shape, lambda i: (0,) * len(shape))
    merged = pl.pallas_call(
        _branch_merge_kernel,
        grid=(t // tm,),
        in_specs=[branch, branch, pl.BlockSpec((tm, 3 * PROJ_TILE), lambda i: (i, U_G // 3)),
                  whole((SGU_GROUPS, SGU_CHUNK, SGU_CHUNK)), whole((SGU_CHUNK, SGU_GROUPS)),
                  whole((1, SGU_WIDTH)), whole((1, SGU_WIDTH)),
                  *[gate(br) for br in range(N_BRANCH)],
                  resident((ATT_WIDTH, D_MODEL)), resident((SGU_WIDTH, D_MODEL)), resident((MEM_WIDTH, D_MODEL))],
        out_specs=rows,
        out_shape=jax.ShapeDtypeStruct((t, D_MODEL), _BF16),
        scratch_shapes=[pltpu.VMEM((tm, SGU_WIDTH), _BF16)],
        compiler_params=params,
        name="branch_merge",
    )(ya, yc, proj, w_s, b_s.T, ln_v_g.reshape(1, -1), ln_v_b.reshape(1, -1),
      *([gates] * N_BRANCH), wa, wg, wc)
    out_rows = pl.BlockSpec((OUT_NORM_TM, D_MODEL), lambda i: (i, 0))
    return pl.pallas_call(
        _out_norm_kernel,
        grid=(t // OUT_NORM_TM,),
        in_specs=[out_rows, out_rows,
                  resident((D_MODEL, D_MODEL)), resident((1, D_MODEL)), resident((1, D_MODEL))],
        out_specs=out_rows,
        out_shape=jax.ShapeDtypeStruct((t, D_MODEL), _F32),
        compiler_params=params,
        name="out_proj_norm",
    )(merged, x2, wo, ln_g.reshape(1, -1), ln_b.reshape(1, -1))


def kernel(x, mem, w_in, w_mem_k, w_mem_v, w_s, b_s, ln_v_g, ln_v_b,
           w_branch_attn, w_branch_sgu, w_branch_mem, w_out, ln_g, ln_b):
    batch, seq, d = x.shape
    assert d == D_MODEL and seq % MOBA_BLOCK == 0 and mem.shape[1] == N_MEM
    t = batch * seq
    x2 = x.reshape(t, d)

    assert w_in.shape[1] == GATE_TILE0 * PROJ_TILE + N_BRANCH * D_MODEL and ATT_WIDTH == PROJ_TILE
    q_a, x_bf = _project(x2, w_in, tm=1024, tn=PROJ_TILE, name="in_proj_q", n_col_tiles=1,
                         scale=MOBA_Q_PRESCALE, row_chunks=4, emit_x_bf16=True)
    proj = _project(x_bf, w_in, tm=IN_PROJ_TM, tn=PROJ_TILE, name="in_proj_branches", first_col_tile=PROJ_TILE0,
                    n_col_tiles=GATE_TILE0 - PROJ_TILE0, row_chunks=IN_PROJ_ROW_CHUNKS)
    gates = _project(x_bf, w_in, tm=IN_PROJ_TM, tn=PROJ_TILE, name="in_proj_gates", first_col_tile=GATE_TILE0,
                     sigmoid=True, row_chunks=IN_PROJ_ROW_CHUNKS)

    ya, wa, wg, wc, wo = _moba_attention(q_a, proj, batch, seq,
                                         cast_jobs=(w_branch_attn, w_branch_sgu, w_branch_mem, w_out))
    ya = ya.reshape(t, ATT_WIDTH)
    yc = _memory_attention(proj, mem, w_mem_k, w_mem_v, seq).reshape(t, MEM_WIDTH)

    out = _merge_project_norm(ya, yc, proj, (w_s, b_s, ln_v_g, ln_v_b), gates, x2, wa, wg, wc, wo, ln_g, ln_b)
    return out.reshape(batch, seq, d)
```

```python
import functools

import jax
import jax.numpy as jnp
from jax import lax
from jax.experimental import pallas as pl
from jax.experimental.pallas import tpu as pltpu

D_MODEL = 2048
DEPTH = 1
N_MEM = 256
ATT_HEAD_DIM = 128
ATT_WIDTH = D_MODEL // 2
MOBA_BLOCK = 256
MOBA_TOPK = 3
SGU_WIDTH = D_MODEL // 2
SGU_CHUNK = 128
SGU_GROUP_DIM = 128
SGU_GROUPS = SGU_WIDTH // SGU_GROUP_DIM
MEM_HEADS = 4
MEM_WIDTH = D_MODEL // 2
MEM_HEAD_DIM = MEM_WIDTH // MEM_HEADS
N_BRANCH = 3
DN_ALPHA = (2 * DEPTH) ** 0.25
LN_EPS = 1e-5

PROJ_TILE = 1024
PROJ_TILE0 = 1
GATE_TILE0 = 9
PROJ_WIDTH = (GATE_TILE0 - PROJ_TILE0) * PROJ_TILE
K_A, V_A, Z_A, U_G, V_G, Z_G, Q_C, Z_C = range(8)

VMEM_LIMIT_BYTES = 56 * 1024 * 1024
IN_PROJ_TM = 2048
IN_PROJ_ROW_CHUNKS = 8
CAST_JOB_ROWS = 256

_NT = (((1,), (1,)), ((), ()))
_F32 = jnp.float32
_BF16 = jnp.bfloat16


_LOG2E = 1.4426950408889634


def _sigmoid(x):
    return 1.0 / (1.0 + jnp.exp2(x * -_LOG2E))


def _silu(x):
    return x * _sigmoid(x)


def _gelu_tanh(x):
    a = -2.0 * 0.7978845608028654 * _LOG2E
    return x / (1.0 + jnp.exp2(x * (a + (a * 0.044715) * (x * x))))


def _project_kernel(x_ref, w_ref, o_ref, *rest, scale, sigmoid, row_chunks, emit_x_bf16):
    x_bf_ref = rest[0] if emit_x_bf16 else None
    w_bf_ref = rest[-1]

    @pl.when(pl.program_id(1) == 0)
    def _cast_weight_tile():
        w_bf_ref[...] = w_ref[...].astype(_BF16)

    tc = x_ref.shape[0] // row_chunks
    for c in range(row_chunks):
        rows = slice(c * tc, (c + 1) * tc)
        xc = x_ref[rows, :]
        if emit_x_bf16:
            xc = xc.astype(_BF16)
            x_bf_ref[rows, :] = xc
        acc = jnp.dot(xc, w_bf_ref[...], preferred_element_type=_F32)
        if sigmoid:
            acc = _sigmoid(acc)
        if scale is not None:
            acc = acc * scale
        o_ref[rows, :] = acc.astype(o_ref.dtype)


def _project(x, w, *, tm, tn, name, first_col_tile=0, n_col_tiles=None,
             scale=None, sigmoid=False, row_chunks=1, emit_x_bf16=False):
    m, k = x.shape
    if n_col_tiles is None:
        n_col_tiles = w.shape[1] // tn - first_col_tile
    assert m % tm == 0 and tm % row_chunks == 0 and (first_col_tile + n_col_tiles) * tn <= w.shape[1]
    assert not emit_x_bf16 or n_col_tiles == 1
    out_specs = [pl.BlockSpec((tm, tn), lambda j, i: (i, j))]
    out_shape = [jax.ShapeDtypeStruct((m, n_col_tiles * tn), _BF16)]
    if emit_x_bf16:
        out_specs.append(pl.BlockSpec((tm, k), lambda j, i: (i, 0)))
        out_shape.append(jax.ShapeDtypeStruct((m, k), _BF16))
    outs = pl.pallas_call(
        functools.partial(_project_kernel, scale=scale, sigmoid=sigmoid, row_chunks=row_chunks,
                          emit_x_bf16=emit_x_bf16),
        grid=(n_col_tiles, m // tm),
        in_specs=[pl.BlockSpec((tm, k), lambda j, i: (i, 0)),
                  pl.BlockSpec((k, tn), lambda j, i: (0, first_col_tile + j))],
        out_specs=out_specs,
        out_shape=out_shape,
        scratch_shapes=[pltpu.VMEM((k, tn), _BF16)],
        compiler_params=pltpu.CompilerParams(
            dimension_semantics=("arbitrary", "arbitrary"), vmem_limit_bytes=VMEM_LIMIT_BYTES),
        name=name,
    )(x, w)
    return outs if emit_x_bf16 else outs[0]


def _cast_job_plan(arrays, n_steps, linear_step):
    specs, shapes, blocks, first = [], [], [], 0
    for a in arrays:
        assert a.shape[0] % CAST_JOB_ROWS == 0
        n_blocks = a.shape[0] // CAST_JOB_ROWS
        specs.append(pl.BlockSpec(
            (CAST_JOB_ROWS, a.shape[1]),
            lambda *idx, first=first, n_blocks=n_blocks: (jnp.clip(linear_step(*idx) - first, 0, n_blocks - 1), 0)))
        shapes.append(jax.ShapeDtypeStruct(a.shape, _BF16))
        blocks.append(n_blocks)
        first += n_blocks
    assert first <= n_steps
    return specs, shapes, tuple(blocks)


def _run_cast_jobs(step, job_in, job_out, blocks):
    first = 0
    for src, dst, n_blocks in zip(job_in, job_out, blocks):
        @pl.when((step >= first) & (step < first + n_blocks))
        def _cast_job_block(src=src, dst=dst):
            dst[...] = src[...].astype(_BF16)
        first += n_blocks


MOBA_HEADS_PER_STEP = 4
MOBA_Q_BLOCKS_PER_STEP = 2
_ACC_ROWS = ATT_HEAD_DIM + 16
MOBA_Q_PRESCALE = ATT_HEAD_DIM ** -0.5 * _LOG2E


def _moba_kernel(q_ref, k_ref, v_ref, z_ref, *rest, n_blocks, heads, cast_job_blocks):
    n_jobs = len(cast_job_blocks)
    job_in, o_ref, job_out = rest[:n_jobs], rest[n_jobs], rest[n_jobs + 1:2 * n_jobs + 1]
    vt_ref, kmean_ref, neg_ref, pos_ref, qt_ref, s_ref, acc_ref, smax_ref = rest[2 * n_jobs + 1:]
    blk = MOBA_BLOCK
    hd = ATT_HEAD_DIM
    step = (pl.program_id(0) * pl.num_programs(1) + pl.program_id(1)) * pl.num_programs(2) + pl.program_id(2)
    _run_cast_jobs(step, job_in, job_out, cast_job_blocks)

    @pl.when(pl.program_id(2) == 0)
    def _per_head_setup():
        for g in range(heads):
            cols = slice(g * hd, (g + 1) * hd)
            for j in range(n_blocks):
                rows = slice(j * blk, (j + 1) * blk)
                vt_ref[g, j, :hd, :] = v_ref[rows, cols].T
                vt_ref[g, j, hd:, :] = jnp.ones((_ACC_ROWS - hd, blk), _BF16)
                kmean_ref[g, j:j + 1, :] = (
                    jnp.sum(k_ref[rows, cols].astype(_F32), axis=0, keepdims=True) * (1.0 / blk))

    for sub in range(MOBA_Q_BLOCKS_PER_STEP):
        rows = slice(sub * blk, (sub + 1) * blk)
        _moba_query_block(pl.program_id(2) * MOBA_Q_BLOCKS_PER_STEP + sub, q_ref.at[rows], z_ref.at[rows],
                          o_ref.at[rows], k_ref, vt_ref, kmean_ref, neg_ref, pos_ref, qt_ref, s_ref, acc_ref, smax_ref,
                          n_blocks=n_blocks, heads=heads)


def _moba_query_block(qi, q_ref, z_ref, o_ref, k_ref, vt_ref, kmean_ref, neg_ref, pos_ref, qt_ref, s_ref, acc_ref, smax_ref,
                      *, n_blocks, heads):
    blk = MOBA_BLOCK
    hd = ATT_HEAD_DIM
    head_cols = [slice(g * hd, (g + 1) * hd) for g in range(heads)]
    qs = [q_ref[:, c] for c in head_cols]
    for g in range(heads):
        qt_ref[g] = qs[g].T

    def issue_scores(j, slot):
        start = j * blk if isinstance(j, int) else pl.multiple_of(j * blk, blk)
        for g in range(heads):
            scores = jnp.dot(k_ref[pl.ds(start, blk), head_cols[g]], qt_ref[g],
                             preferred_element_type=_F32)
            s_ref[slot, g] = scores
            smax_ref[slot * heads + g:slot * heads + g + 1, :] = jnp.max(scores, axis=0, keepdims=True)

    sel_scores = []
    for g in range(heads):
        km = kmean_ref[g]
        km_hi = km.astype(_BF16)
        km_lo = (km - km_hi.astype(_F32)).astype(_BF16)
        sel_scores.append(jnp.dot(km_hi, qt_ref[g], preferred_element_type=_F32)
                          + jnp.dot(km_lo, qt_ref[g], preferred_element_type=_F32))
    issue_scores(qi, 4)
    row_id = lax.broadcasted_iota(jnp.int32, (n_blocks, blk), 0)
    past = row_id < qi
    later_rows = [jnp.where(row_id > j, 1.0, 0.0) for j in range(n_blocks)]
    for g in range(heads):
        sb = jnp.where(past, sel_scores[g], -jnp.inf)
        rank = jnp.zeros(sb.shape, _F32)
        for j in range(n_blocks):
            other = sb[j:j + 1, :]
            rank = rank + jnp.where(other > sb, 1.0, 0.0) + jnp.where(other == sb, later_rows[j], 0.0)
        selected = past & (rank < MOBA_TOPK)
        neg_ref[g] = jnp.where(selected, 0.0, -jnp.inf)
        pos_ref[g] = jnp.where(selected, -jnp.inf, jnp.inf)

    def accumulate(j, slot, ms, causal):
        probs, m_news = [], []
        for g in range(heads):
            s = s_ref[slot, g]
            if causal:
                key_pos = lax.broadcasted_iota(jnp.int32, s.shape, 0)
                q_pos = lax.broadcasted_iota(jnp.int32, s.shape, 1)
                s = jnp.where(key_pos <= q_pos, s, -jnp.inf)
                m_new = jnp.maximum(ms[g], jnp.max(s, axis=0, keepdims=True))
                m_sub = m_new
            else:
                block_max = smax_ref[slot * heads + g:slot * heads + g + 1, :]
                m_new = jnp.maximum(ms[g], block_max + neg_ref[g, pl.ds(j, 1), :])
                m_sub = jnp.maximum(m_new, pos_ref[g, pl.ds(j, 1), :])
            probs.append(jnp.exp2(s - m_sub).astype(_BF16))
            m_news.append(m_new)
        for g in range(heads):
            pv = jnp.dot(vt_ref[g, j], probs[g], preferred_element_type=_F32)
            if causal:
                acc_ref[g] = pv
            else:
                acc_ref[g] = jnp.exp2(ms[g] - m_news[g]) * acc_ref[g] + pv
        return m_news

    def past_block_quad(u, ms):
        for k in range(4):
            issue_scores(4 * u + k + 1, (k + 1) % 4)
            ms = accumulate(4 * u + k, k, ms, causal=False)
        return ms

    issue_scores(0, 0)
    ms = accumulate(qi, 4, [jnp.full((1, blk), -jnp.inf, _F32)] * heads, causal=True)
    ms = lax.fori_loop(0, qi // 4, past_block_quad, ms)

    base, rest = (qi // 4) * 4, qi % 4

    def remaining_pair(ms):
        issue_scores(base + 1, 1)
        ms = accumulate(base, 0, ms, causal=False)
        issue_scores(base + 2, 2)
        return accumulate(base + 1, 1, ms, causal=False)

    ms = lax.cond(rest >= 2, remaining_pair, lambda ms: ms, ms)

    @pl.when(rest == 1)
    def _last_of_one():
        accumulate(base, 0, ms, causal=False)

    @pl.when(rest == 3)
    def _last_of_three():
        accumulate(base + 2, 2, ms, causal=False)

    for g in range(heads):
        attn = (acc_ref[g, :hd, :] / acc_ref[g, hd:hd + 1, :]).T
        o_ref[:, head_cols[g]] = (attn * _silu(z_ref[:, head_cols[g]].astype(_F32))).astype(o_ref.dtype)


def _moba_attention(q, proj, batch, seq, *, heads=MOBA_HEADS_PER_STEP, cast_jobs=()):
    n_blocks = seq // MOBA_BLOCK
    q3 = q.reshape(batch, seq, ATT_WIDTH)
    proj3 = proj.reshape(batch, seq, PROJ_WIDTH)
    width = heads * ATT_HEAD_DIM
    cols = PROJ_TILE // width
    q_rows = MOBA_Q_BLOCKS_PER_STEP * MOBA_BLOCK
    assert n_blocks % MOBA_Q_BLOCKS_PER_STEP == 0
    grid = (batch, ATT_WIDTH // width, n_blocks // MOBA_Q_BLOCKS_PER_STEP)
    job_specs, job_shapes, job_blocks = _cast_job_plan(
        cast_jobs, grid[0] * grid[1] * grid[2], lambda b, h, i: (b * grid[1] + h) * grid[2] + i)
    return pl.pallas_call(
        functools.partial(_moba_kernel, n_blocks=n_blocks, heads=heads, cast_job_blocks=job_blocks),
        grid=grid,
        in_specs=[pl.BlockSpec((None, q_rows, width), lambda b, h, i: (b, i, h)),
                  pl.BlockSpec((None, seq, width), lambda b, h, i: (b, 0, K_A * cols + h)),
                  pl.BlockSpec((None, seq, width), lambda b, h, i: (b, 0, V_A * cols + h)),
                  pl.BlockSpec((None, q_rows, width), lambda b, h, i: (b, i, Z_A * cols + h)),
                  *job_specs],
        out_specs=[pl.BlockSpec((None, q_rows, width), lambda b, h, i: (b, i, h)), *job_specs],
        out_shape=[jax.ShapeDtypeStruct((batch, seq, ATT_WIDTH), _BF16), *job_shapes],
        scratch_shapes=[pltpu.VMEM((heads, n_blocks, _ACC_ROWS, MOBA_BLOCK), _BF16),
                        pltpu.VMEM((heads, n_blocks, ATT_HEAD_DIM), _F32),
                        pltpu.VMEM((heads, n_blocks, MOBA_BLOCK), _F32),
                        pltpu.VMEM((heads, n_blocks, MOBA_BLOCK), _F32),
                        pltpu.VMEM((heads, ATT_HEAD_DIM, MOBA_BLOCK), _BF16),
                        pltpu.VMEM((5, heads, MOBA_BLOCK, MOBA_BLOCK), _F32),
                        pltpu.VMEM((heads, _ACC_ROWS, MOBA_BLOCK), _F32),
                        pltpu.VMEM((5 * heads, MOBA_BLOCK), _F32)],
        compiler_params=pltpu.CompilerParams(
            dimension_semantics=("arbitrary", "arbitrary", "arbitrary"), vmem_limit_bytes=VMEM_LIMIT_BYTES),
        name="moba_attention",
    )(q3, proj3, proj3, proj3, *cast_jobs)


def _sgu_causal_weights(w_ref):
    t_pos = lax.broadcasted_iota(jnp.int32, (SGU_CHUNK, SGU_CHUNK), 0)
    s_pos = lax.broadcasted_iota(jnp.int32, (SGU_CHUNK, SGU_CHUNK), 1)
    return [jnp.where(s_pos <= t_pos, w_ref[g], 0.0).astype(_BF16) for g in range(SGU_GROUPS)]


def _sgu_rows(u_ref, v_ref, z_ref, w_causal, bt_ref, g_ref, b_ref, rows, yg_ref):
    u = _gelu_tanh(u_ref[rows, :].astype(_F32))
    v = _gelu_tanh(v_ref[rows, :].astype(_F32))
    mu = jnp.mean(v, axis=-1, keepdims=True)
    d = v - mu
    var = jnp.mean(d * d, axis=-1, keepdims=True)
    vn = (d * lax.rsqrt(var + LN_EPS) * g_ref[...] + b_ref[...]).astype(_BF16)
    gate = u * _silu(z_ref[rows, :].astype(_F32))
    for g in range(SGU_GROUPS):
        bias = bt_ref[:, g:g + 1]
        cols = slice(g * SGU_GROUP_DIM, (g + 1) * SGU_GROUP_DIM)
        for c in range((rows.stop - rows.start) // SGU_CHUNK):
            sub = slice(c * SGU_CHUNK, (c + 1) * SGU_CHUNK)
            mixed = jnp.dot(w_causal[g], vn[sub, cols], preferred_element_type=_F32) + bias
            yg_ref[rows.start + c * SGU_CHUNK:rows.start + (c + 1) * SGU_CHUNK, cols] = (
                gate[sub, cols] * mixed).astype(yg_ref.dtype)


def _mem_attn_kernel(q_ref, z_ref, mem_ref, wk_ref, wv_ref, o_ref, k_ref, v_ref):
    @pl.when(pl.program_id(1) == 0)
    def _project_memory():
        mem_bf = mem_ref[...].astype(_BF16)
        for w_ref, dst in ((wk_ref, k_ref), (wv_ref, v_ref)):
            for c in range(0, MEM_WIDTH, MEM_HEAD_DIM):
                dst[:, c:c + MEM_HEAD_DIM] = jnp.dot(
                    mem_bf, w_ref[:, c:c + MEM_HEAD_DIM].astype(_BF16), preferred_element_type=_F32
                ).astype(_BF16)

    scale_log2e = MEM_HEAD_DIM ** -0.5 * _LOG2E
    for h in range(MEM_HEADS):
        cols = slice(h * MEM_HEAD_DIM, (h + 1) * MEM_HEAD_DIM)
        s = lax.dot_general(q_ref[:, cols], k_ref[:, cols], _NT, preferred_element_type=_F32)
        p = jnp.exp2((s - jnp.max(s, axis=-1, keepdims=True)) * scale_log2e)
        l = jnp.sum(p, axis=-1, keepdims=True)
        o = jnp.dot(p.astype(_BF16), v_ref[:, cols], preferred_element_type=_F32) / l
        o_ref[:, cols] = (o * _silu(z_ref[:, cols].astype(_F32))).astype(o_ref.dtype)


def _memory_attention(proj, mem, w_mem_k, w_mem_v, seq, *, tm=1024):
    batch, n_mem, d = mem.shape
    proj3 = proj.reshape(batch, seq, PROJ_WIDTH)
    weight = pl.BlockSpec((d, MEM_WIDTH), lambda b, i: (0, 0), pipeline_mode=pl.Buffered(1))
    return pl.pallas_call(
        _mem_attn_kernel,
        grid=(batch, seq // tm),
        in_specs=[pl.BlockSpec((None, tm, MEM_WIDTH), lambda b, i: (b, i, Q_C)),
                  pl.BlockSpec((None, tm, MEM_WIDTH), lambda b, i: (b, i, Z_C)),
                  pl.BlockSpec((None, n_mem, d), lambda b, i: (b, 0, 0)),
                  weight, weight],
        out_specs=pl.BlockSpec((None, tm, MEM_WIDTH), lambda b, i: (b, i, 0)),
        out_shape=jax.ShapeDtypeStruct((batch, seq, MEM_WIDTH), _BF16),
        scratch_shapes=[pltpu.VMEM((n_mem, MEM_WIDTH), _BF16), pltpu.VMEM((n_mem, MEM_WIDTH), _BF16)],
        compiler_params=pltpu.CompilerParams(
            dimension_semantics=("arbitrary", "arbitrary"), vmem_limit_bytes=VMEM_LIMIT_BYTES),
        name="memory_attention",
    )(proj3, proj3, mem, w_mem_k, w_mem_v)


MERGE_TM = 512
OUT_NORM_TM = 512
OUT_NORM_CHUNKS = (256, 128, 128)
MERGE_ROW_CHUNK = 256


def _branch_merge_kernel(ya_ref, yc_ref, uvz_ref, ws_ref, bt_ref, lvg_ref, lvb_ref, ga_ref, gg_ref, gc_ref,
                         wa_ref, wg_ref, wc_ref, o_ref, yg_ref):
    u_ref, v_ref, z_ref = (uvz_ref.at[:, k * PROJ_TILE:(k + 1) * PROJ_TILE] for k in range(3))
    per_branch = D_MODEL // PROJ_TILE
    col_tiles = [slice(c * PROJ_TILE, (c + 1) * PROJ_TILE) for c in range(per_branch)]
    chunks = [slice(r * MERGE_ROW_CHUNK, (r + 1) * MERGE_ROW_CHUNK)
              for r in range(o_ref.shape[0] // MERGE_ROW_CHUNK)]
    w_causal = _sgu_causal_weights(ws_ref)

    def branch_dots(rows):
        n_sub = len(col_tiles)
        sub_rows = (rows.stop - rows.start) // n_sub
        outer = []
        for c, cols in enumerate(col_tiles):
            outer.append((jnp.dot(ya_ref[rows, :], wa_ref[:, cols], preferred_element_type=_F32),
                          jnp.dot(yc_ref[rows, :], wc_ref[:, cols], preferred_element_type=_F32)))
            sub = slice(rows.start + c * sub_rows, rows.start + (c + 1) * sub_rows)
            _sgu_rows(u_ref, v_ref, z_ref, w_causal, bt_ref, lvg_ref, lvb_ref, sub, yg_ref)
        inner = [jnp.dot(yg_ref[rows, :], wg_ref[:, cols], preferred_element_type=_F32) for cols in col_tiles]
        return outer, inner

    dots = branch_dots(chunks[0])
    for k, rows in enumerate(chunks):
        next_dots = branch_dots(chunks[k + 1]) if k + 1 < len(chunks) else None
        outer, inner = dots
        for c, cols in enumerate(col_tiles):
            merged = (ga_ref[rows, cols].astype(_F32) * outer[c][0]
                      + gg_ref[rows, cols].astype(_F32) * inner[c]
                      + gc_ref[rows, cols].astype(_F32) * outer[c][1])
            o_ref[rows, cols] = merged.astype(o_ref.dtype)
        dots = next_dots


def _out_norm_kernel(m_ref, x_ref, wo_ref, lng_ref, lnb_ref, o_ref):
    bounds = [0]
    for size in OUT_NORM_CHUNKS:
        bounds.append(bounds[-1] + size)
    assert bounds[-1] == o_ref.shape[0]
    chunks = [slice(a, b) for a, b in zip(bounds[:-1], bounds[1:])]

    def out_dot(rows):
        return jnp.dot(m_ref[rows, :], wo_ref[...], preferred_element_type=_F32)

    y = out_dot(chunks[0])
    for k, rows in enumerate(chunks):
        y_next = out_dot(chunks[k + 1]) if k + 1 < len(chunks) else None
        h = DN_ALPHA * x_ref[rows, :] + y
        mu = jnp.mean(h, axis=-1, keepdims=True)
        d = h - mu
        var = jnp.mean(d * d, axis=-1, keepdims=True)
        o_ref[rows, :] = d * lax.rsqrt(var + LN_EPS) * lng_ref[...] + lnb_ref[...]
        y = y_next


def _merge_project_norm(ya, yc, proj, sgu_params, gates, x2, wa, wg, wc, wo, ln_g, ln_b, *, tm=MERGE_TM):
    t = x2.shape[0]
    w_s, b_s, ln_v_g, ln_v_b = sgu_params
    assert t % tm == 0 and tm % MERGE_ROW_CHUNK == 0 and t % OUT_NORM_TM == 0 and OUT_NORM_TM % MERGE_ROW_CHUNK == 0
    branch = pl.BlockSpec((tm, ATT_WIDTH), lambda i: (i, 0))
    gate = lambda br: pl.BlockSpec((tm, D_MODEL), lambda i: (i, br))
    rows = pl.BlockSpec((tm, D_MODEL), lambda i: (i, 0))
    resident = lambda shape: pl.BlockSpec(shape, lambda i: (0, 0), pipeline_mode=pl.Buffered(1))
    params = pltpu.CompilerParams(dimension_semantics=("arbitrary",), vmem_limit_bytes=VMEM_LIMIT_BYTES)
    assert (V_G, Z_G) == (U_G + 1, U_G + 2) and U_G % 3 == 0
    whole = lambda shape: pl.BlockSpec(shape, lambda i: (0,) * len(shape))
    merged = pl.pallas_call(
        _branch_merge_kernel,
        grid=(t // tm,),
        in_specs=[branch, branch, pl.BlockSpec((tm, 3 * PROJ_TILE), lambda i: (i, U_G // 3)),
                  whole((SGU_GROUPS, SGU_CHUNK, SGU_CHUNK)), whole((SGU_CHUNK, SGU_GROUPS)),
                  whole((1, SGU_WIDTH)), whole((1, SGU_WIDTH)),
                  *[gate(br) for br in range(N_BRANCH)],
                  resident((ATT_WIDTH, D_MODEL)), resident((SGU_WIDTH, D_MODEL)), resident((MEM_WIDTH, D_MODEL))],
        out_specs=rows,
        out_shape=jax.ShapeDtypeStruct((t, D_MODEL), _BF16),
        scratch_shapes=[pltpu.VMEM((tm, SGU_WIDTH), _BF16)],
        compiler_params=params,
        name="branch_merge",
    )(ya, yc, proj, w_s, b_s.T, ln_v_g.reshape(1, -1), ln_v_b.reshape(1, -1),
      *([gates] * N_BRANCH), wa, wg, wc)
    out_rows = pl.BlockSpec((OUT_NORM_TM, D_MODEL), lambda i: (i, 0))
    return pl.pallas_call(
        _out_norm_kernel,
        grid=(t // OUT_NORM_TM,),
        in_specs=[out_rows, out_rows,
                  resident((D_MODEL, D_MODEL)), resident((1, D_MODEL)), resident((1, D_MODEL))],
        out_specs=out_rows,
        out_shape=jax.ShapeDtypeStruct((t, D_MODEL), _F32),
        compiler_params=params,
        name="out_proj_norm",
    )(merged, x2, wo, ln_g.reshape(1, -1), ln_b.reshape(1, -1))


def kernel(x, mem, w_in, w_mem_k, w_mem_v, w_s, b_s, ln_v_g, ln_v_b,
           w_branch_attn, w_branch_sgu, w_branch_mem, w_out, ln_g, ln_b):
    batch, seq, d = x.shape
    assert d == D_MODEL and seq % MOBA_BLOCK == 0 and mem.shape[1] == N_MEM
    t = batch * seq
    x2 = x.reshape(t, d)

    assert w_in.shape[1] == GATE_TILE0 * PROJ_TILE + N_BRANCH * D_MODEL and ATT_WIDTH == PROJ_TILE
    q_a, x_bf = _project(x2, w_in, tm=1024, tn=PROJ_TILE, name="in_proj_q", n_col_tiles=1,
                         scale=MOBA_Q_PRESCALE, row_chunks=4, emit_x_bf16=True)
    proj = _project(x_bf, w_in, tm=IN_PROJ_TM, tn=PROJ_TILE, name="in_proj_branches", first_col_tile=PROJ_TILE0,
                    n_col_tiles=GATE_TILE0 - PROJ_TILE0, row_chunks=IN_PROJ_ROW_CHUNKS)
    gates = _project(x_bf, w_in, tm=IN_PROJ_TM, tn=PROJ_TILE, name="in_proj_gates", first_col_tile=GATE_TILE0,
                     sigmoid=True, row_chunks=IN_PROJ_ROW_CHUNKS)

    ya, wa, wg, wc, wo = _moba_attention(q_a, proj, batch, seq,
                                         cast_jobs=(w_branch_attn, w_branch_sgu, w_branch_mem, w_out))
    ya = ya.reshape(t, ATT_WIDTH)
    yc = _memory_attention(proj, mem, w_mem_k, w_mem_v, seq).reshape(t, MEM_WIDTH)

    out = _merge_project_norm(ya, yc, proj, (w_s, b_s, ln_v_g, ln_v_b), gates, x2, wa, wg, wc, wo, ln_g, ln_b)
    return out.reshape(batch, seq, d)
```

```python
import functools

import jax
import jax.numpy as jnp
from jax import lax
from jax.experimental import pallas as pl
from jax.experimental.pallas import tpu as pltpu

D_MODEL = 2048
DEPTH = 1
N_MEM = 256
ATT_HEAD_DIM = 128
ATT_WIDTH = D_MODEL // 2
MOBA_BLOCK = 256
MOBA_TOPK = 3
SGU_WIDTH = D_MODEL // 2
SGU_CHUNK = 128
SGU_GROUP_DIM = 128
SGU_GROUPS = SGU_WIDTH // SGU_GROUP_DIM
MEM_HEADS = 4
MEM_WIDTH = D_MODEL // 2
MEM_HEAD_DIM = MEM_WIDTH // MEM_HEADS
N_BRANCH = 3
DN_ALPHA = (2 * DEPTH) ** 0.25
LN_EPS = 1e-5

PROJ_TILE = 1024
PROJ_TILE0 = 1
GATE_TILE0 = 9
PROJ_WIDTH = (GATE_TILE0 - PROJ_TILE0) * PROJ_TILE
K_A, V_A, Z_A, U_G, V_G, Z_G, Q_C, Z_C = range(8)

VMEM_LIMIT_BYTES = 56 * 1024 * 1024
IN_PROJ_TM = 2048
IN_PROJ_ROW_CHUNKS = 8
CAST_JOB_ROWS = 256

_NT = (((1,), (1,)), ((), ()))
_F32 = jnp.float32
_BF16 = jnp.bfloat16


_LOG2E = 1.4426950408889634


def _sigmoid(x):
    return 1.0 / (1.0 + jnp.exp2(x * -_LOG2E))


def _silu(x):
    return x * _sigmoid(x)


def _gelu_tanh(x):
    a = -2.0 * 0.7978845608028654 * _LOG2E
    return x / (1.0 + jnp.exp2(x * (a + (a * 0.044715) * (x * x))))


def _project_kernel(x_ref, w_ref, o_ref, *rest, scale, sigmoid, row_chunks, emit_x_bf16):
    x_bf_ref = rest[0] if emit_x_bf16 else None
    w_bf_ref = rest[-1]

    @pl.when(pl.program_id(1) == 0)
    def _cast_weight_tile():
        w_bf_ref[...] = w_ref[...].astype(_BF16)

    tc = x_ref.shape[0] // row_chunks
    for c in range(row_chunks):
        rows = slice(c * tc, (c + 1) * tc)
        xc = x_ref[rows, :]
        if emit_x_bf16:
            xc = xc.astype(_BF16)
            x_bf_ref[rows, :] = xc
        acc = jnp.dot(xc, w_bf_ref[...], preferred_element_type=_F32)
        if sigmoid:
            acc = _sigmoid(acc)
        if scale is not None:
            acc = acc * scale
        o_ref[rows, :] = acc.astype(o_ref.dtype)


def _project(x, w, *, tm, tn, name, first_col_tile=0, n_col_tiles=None,
             scale=None, sigmoid=False, row_chunks=1, emit_x_bf16=False):
    m, k = x.shape
    if n_col_tiles is None:
        n_col_tiles = w.shape[1] // tn - first_col_tile
    assert m % tm == 0 and tm % row_chunks == 0 and (first_col_tile + n_col_tiles) * tn <= w.shape[1]
    assert not emit_x_bf16 or n_col_tiles == 1
    out_specs = [pl.BlockSpec((tm, tn), lambda j, i: (i, j))]
    out_shape = [jax.ShapeDtypeStruct((m, n_col_tiles * tn), _BF16)]
    if emit_x_bf16:
        out_specs.append(pl.BlockSpec((tm, k), lambda j, i: (i, 0)))
        out_shape.append(jax.ShapeDtypeStruct((m, k), _BF16))
    outs = pl.pallas_call(
        functools.partial(_project_kernel, scale=scale, sigmoid=sigmoid, row_chunks=row_chunks,
                          emit_x_bf16=emit_x_bf16),
        grid=(n_col_tiles, m // tm),
        in_specs=[pl.BlockSpec((tm, k), lambda j, i: (i, 0)),
                  pl.BlockSpec((k, tn), lambda j, i: (0, first_col_tile + j))],
        out_specs=out_specs,
        out_shape=out_shape,
        scratch_shapes=[pltpu.VMEM((k, tn), _BF16)],
        compiler_params=pltpu.CompilerParams(
            dimension_semantics=("arbitrary", "arbitrary"), vmem_limit_bytes=VMEM_LIMIT_BYTES),
        name=name,
    )(x, w)
    return outs if emit_x_bf16 else outs[0]


def _cast_job_plan(arrays, n_steps, linear_step):
    specs, shapes, blocks, first = [], [], [], 0
    for a in arrays:
        assert a.shape[0] % CAST_JOB_ROWS == 0
        n_blocks = a.shape[0] // CAST_JOB_ROWS
        specs.append(pl.BlockSpec(
            (CAST_JOB_ROWS, a.shape[1]),
            lambda *idx, first=first, n_blocks=n_blocks: (jnp.clip(linear_step(*idx) - first, 0, n_blocks - 1), 0)))
        shapes.append(jax.ShapeDtypeStruct(a.shape, _BF16))
        blocks.append(n_blocks)
        first += n_blocks
    assert first <= n_steps
    return specs, shapes, tuple(blocks)


def _run_cast_jobs(step, job_in, job_out, blocks):
    first = 0
    for src, dst, n_blocks in zip(job_in, job_out, blocks):
        @pl.when((step >= first) & (step < first + n_blocks))
        def _cast_job_block(src=src, dst=dst):
            dst[...] = src[...].astype(_BF16)
        first += n_blocks


MOBA_HEADS_PER_STEP = 4
MOBA_Q_BLOCKS_PER_STEP = 2
MOBA_UNROLL = 8
_ACC_ROWS = ATT_HEAD_DIM + 16
MOBA_Q_PRESCALE = ATT_HEAD_DIM ** -0.5 * _LOG2E


def _moba_kernel(q_ref, k_ref, v_ref, z_ref, *rest, n_blocks, heads, cast_job_blocks):
    n_jobs = len(cast_job_blocks)
    job_in, o_ref, job_out = rest[:n_jobs], rest[n_jobs], rest[n_jobs + 1:2 * n_jobs + 1]
    vt_ref, kmean_ref, neg_ref, pos_ref, qt_ref, s_ref, acc_ref, smax_ref = rest[2 * n_jobs + 1:]
    blk = MOBA_BLOCK
    hd = ATT_HEAD_DIM
    step = (pl.program_id(0) * pl.num_programs(1) + pl.program_id(1)) * pl.num_programs(2) + pl.program_id(2)
    _run_cast_jobs(step, job_in, job_out, cast_job_blocks)

    @pl.when(pl.program_id(2) == 0)
    def _per_head_setup():
        for g in range(heads):
            cols = slice(g * hd, (g + 1) * hd)
            for j in range(n_blocks):
                rows = slice(j * blk, (j + 1) * blk)
                vt_ref[g, j, :hd, :] = v_ref[rows, cols].T
                vt_ref[g, j, hd:, :] = jnp.ones((_ACC_ROWS - hd, blk), _BF16)
                kmean_ref[g, j:j + 1, :] = (
                    jnp.sum(k_ref[rows, cols].astype(_F32), axis=0, keepdims=True) * (1.0 / blk))

    for sub in range(MOBA_Q_BLOCKS_PER_STEP):
        rows = slice(sub * blk, (sub + 1) * blk)
        _moba_query_block(pl.program_id(2) * MOBA_Q_BLOCKS_PER_STEP + sub, q_ref.at[rows], z_ref.at[rows],
                          o_ref.at[rows], k_ref, vt_ref, kmean_ref, neg_ref, pos_ref, qt_ref, s_ref, acc_ref, smax_ref,
                          n_blocks=n_blocks, heads=heads)


def _moba_query_block(qi, q_ref, z_ref, o_ref, k_ref, vt_ref, kmean_ref, neg_ref, pos_ref, qt_ref, s_ref, acc_ref, smax_ref,
                      *, n_blocks, heads):
    blk = MOBA_BLOCK
    hd = ATT_HEAD_DIM
    head_cols = [slice(g * hd, (g + 1) * hd) for g in range(heads)]
    qs = [q_ref[:, c] for c in head_cols]
    for g in range(heads):
        qt_ref[g] = qs[g].T

    def issue_scores(j, slot):
        start = j * blk if isinstance(j, int) else pl.multiple_of(j * blk, blk)
        for g in range(heads):
            scores = jnp.dot(k_ref[pl.ds(start, blk), head_cols[g]], qt_ref[g],
                             preferred_element_type=_F32)
            s_ref[slot, g] = scores
            smax_ref[slot * heads + g:slot * heads + g + 1, :] = jnp.max(scores, axis=0, keepdims=True)

    sel_scores = []
    for g in range(heads):
        km = kmean_ref[g]
        km_hi = km.astype(_BF16)
        km_lo = (km - km_hi.astype(_F32)).astype(_BF16)
        sel_scores.append(jnp.dot(km_hi, qt_ref[g], preferred_element_type=_F32)
                          + jnp.dot(km_lo, qt_ref[g], preferred_element_type=_F32))
    issue_scores(qi, 2)
    row_id = lax.broadcasted_iota(jnp.int32, (n_blocks, blk), 0)
    past = row_id < qi
    later_rows = [jnp.where(row_id > j, 1.0, 0.0) for j in range(n_blocks)]
    for g in range(heads):
        sb = jnp.where(past, sel_scores[g], -jnp.inf)
        rank = jnp.zeros(sb.shape, _F32)
        for j in range(n_blocks):
            other = sb[j:j + 1, :]
            rank = rank + jnp.where(other > sb, 1.0, 0.0) + jnp.where(other == sb, later_rows[j], 0.0)
        selected = past & (rank < MOBA_TOPK)
        neg_ref[g] = jnp.where(selected, 0.0, -jnp.inf)
        pos_ref[g] = jnp.where(selected, -jnp.inf, jnp.inf)

    def accumulate(j, slot, ms, causal):
        probs, m_news = [], []
        for g in range(heads):
            s = s_ref[slot, g]
            if causal:
                key_pos = lax.broadcasted_iota(jnp.int32, s.shape, 0)
                q_pos = lax.broadcasted_iota(jnp.int32, s.shape, 1)
                s = jnp.where(key_pos <= q_pos, s, -jnp.inf)
                m_new = jnp.maximum(ms[g], jnp.max(s, axis=0, keepdims=True))
                m_sub = m_new
            else:
                block_max = smax_ref[slot * heads + g:slot * heads + g + 1, :]
                m_new = jnp.maximum(ms[g], block_max + neg_ref[g, pl.ds(j, 1), :])
                m_sub = jnp.maximum(m_new, pos_ref[g, pl.ds(j, 1), :])
            probs.append(jnp.exp2(s - m_sub).astype(_BF16))
            m_news.append(m_new)
        for g in range(heads):
            pv = jnp.dot(vt_ref[g, j], probs[g], preferred_element_type=_F32)
            if causal:
                acc_ref[g] = pv
            else:
                acc_ref[g] = jnp.exp2(ms[g] - m_news[g]) * acc_ref[g] + pv
        return m_news

    def run_blocks(first, n, ms, prefetch_after=True):
        for k in range(n):
            if prefetch_after or k + 1 < n:
                issue_scores(first + k + 1, (k + 1) % 2)
            ms = accumulate(first + k, k % 2, ms, causal=False)
        return ms

    issue_scores(0, 0)
    ms = accumulate(qi, 2, [jnp.full((1, blk), -jnp.inf, _F32)] * heads, causal=True)
    ms = lax.fori_loop(0, qi // MOBA_UNROLL, lambda u, ms: run_blocks(MOBA_UNROLL * u, MOBA_UNROLL, ms), ms)

    first, rest = (qi // MOBA_UNROLL) * MOBA_UNROLL, qi % MOBA_UNROLL
    arm = MOBA_UNROLL // 2
    while arm >= 2:
        ms = lax.cond((rest & arm) != 0, functools.partial(run_blocks, first, arm), lambda ms: ms, ms)
        first = first + (rest & arm)
        arm //= 2
    last_block = first

    @pl.when((rest & 1) != 0)
    def _last_past_block():
        run_blocks(last_block, 1, ms, prefetch_after=False)

    for g in range(heads):
        attn = (acc_ref[g, :hd, :] / acc_ref[g, hd:hd + 1, :]).T
        o_ref[:, head_cols[g]] = (attn * _silu(z_ref[:, head_cols[g]].astype(_F32))).astype(o_ref.dtype)


def _moba_attention(q, proj, batch, seq, *, heads=MOBA_HEADS_PER_STEP, cast_jobs=()):
    n_blocks = seq // MOBA_BLOCK
    q3 = q.reshape(batch, seq, ATT_WIDTH)
    proj3 = proj.reshape(batch, seq, PROJ_WIDTH)
    width = heads * ATT_HEAD_DIM
    cols = PROJ_TILE // width
    q_rows = MOBA_Q_BLOCKS_PER_STEP * MOBA_BLOCK
    assert n_blocks % MOBA_Q_BLOCKS_PER_STEP == 0
    grid = (batch, ATT_WIDTH // width, n_blocks // MOBA_Q_BLOCKS_PER_STEP)
    job_specs, job_shapes, job_blocks = _cast_job_plan(
        cast_jobs, grid[0] * grid[1] * grid[2], lambda b, h, i: (b * grid[1] + h) * grid[2] + i)
    return pl.pallas_call(
        functools.partial(_moba_kernel, n_blocks=n_blocks, heads=heads, cast_job_blocks=job_blocks),
        grid=grid,
        in_specs=[pl.BlockSpec((None, q_rows, width), lambda b, h, i: (b, i, h)),
                  pl.BlockSpec((None, seq, width), lambda b, h, i: (b, 0, K_A * cols + h)),
                  pl.BlockSpec((None, seq, width), lambda b, h, i: (b, 0, V_A * cols + h)),
                  pl.BlockSpec((None, q_rows, width), lambda b, h, i: (b, i, Z_A * cols + h)),
                  *job_specs],
        out_specs=[pl.BlockSpec((None, q_rows, width), lambda b, h, i: (b, i, h)), *job_specs],
        out_shape=[jax.ShapeDtypeStruct((batch, seq, ATT_WIDTH), _BF16), *job_shapes],
        scratch_shapes=[pltpu.VMEM((heads, n_blocks, _ACC_ROWS, MOBA_BLOCK), _BF16),
                        pltpu.VMEM((heads, n_blocks, ATT_HEAD_DIM), _F32),
                        pltpu.VMEM((heads, n_blocks, MOBA_BLOCK), _F32),
                        pltpu.VMEM((heads, n_blocks, MOBA_BLOCK), _F32),
                        pltpu.VMEM((heads, ATT_HEAD_DIM, MOBA_BLOCK), _BF16),
                        pltpu.VMEM((3, heads, MOBA_BLOCK, MOBA_BLOCK), _F32),
                        pltpu.VMEM((heads, _ACC_ROWS, MOBA_BLOCK), _F32),
                        pltpu.VMEM((3 * heads, MOBA_BLOCK), _F32)],
        compiler_params=pltpu.CompilerParams(
            dimension_semantics=("arbitrary", "arbitrary", "arbitrary"), vmem_limit_bytes=VMEM_LIMIT_BYTES),
        name="moba_attention",
    )(q3, proj3, proj3, proj3, *cast_jobs)


def _sgu_causal_weights(w_ref):
    t_pos = lax.broadcasted_iota(jnp.int32, (SGU_CHUNK, SGU_CHUNK), 0)
    s_pos = lax.broadcasted_iota(jnp.int32, (SGU_CHUNK, SGU_CHUNK), 1)
    return [jnp.where(s_pos <= t_pos, w_ref[g], 0.0).astype(_BF16) for g in range(SGU_GROUPS)]


def _sgu_rows(u_ref, v_ref, z_ref, w_causal, bt_ref, g_ref, b_ref, rows, yg_ref):
    u = _gelu_tanh(u_ref[rows, :].astype(_F32))
    v = _gelu_tanh(v_ref[rows, :].astype(_F32))
    mu = jnp.mean(v, axis=-1, keepdims=True)
    d = v - mu
    var = jnp.mean(d * d, axis=-1, keepdims=True)
    vn = (d * lax.rsqrt(var + LN_EPS) * g_ref[...] + b_ref[...]).astype(_BF16)
    gate = u * _silu(z_ref[rows, :].astype(_F32))
    for g in range(SGU_GROUPS):
        bias = bt_ref[:, g:g + 1]
        cols = slice(g * SGU_GROUP_DIM, (g + 1) * SGU_GROUP_DIM)
        for c in range((rows.stop - rows.start) // SGU_CHUNK):
            sub = slice(c * SGU_CHUNK, (c + 1) * SGU_CHUNK)
            mixed = jnp.dot(w_causal[g], vn[sub, cols], preferred_element_type=_F32) + bias
            yg_ref[rows.start + c * SGU_CHUNK:rows.start + (c + 1) * SGU_CHUNK, cols] = (
                gate[sub, cols] * mixed).astype(yg_ref.dtype)


def _mem_attn_kernel(q_ref, z_ref, mem_ref, wk_ref, wv_ref, o_ref, k_ref, v_ref):
    @pl.when(pl.program_id(1) == 0)
    def _project_memory():
        mem_bf = mem_ref[...].astype(_BF16)
        for w_ref, dst in ((wk_ref, k_ref), (wv_ref, v_ref)):
            for c in range(0, MEM_WIDTH, MEM_HEAD_DIM):
                dst[:, c:c + MEM_HEAD_DIM] = jnp.dot(
                    mem_bf, w_ref[:, c:c + MEM_HEAD_DIM].astype(_BF16), preferred_element_type=_F32
                ).astype(_BF16)

    scale_log2e = MEM_HEAD_DIM ** -0.5 * _LOG2E
    for h in range(MEM_HEADS):
        cols = slice(h * MEM_HEAD_DIM, (h + 1) * MEM_HEAD_DIM)
        s = lax.dot_general(q_ref[:, cols], k_ref[:, cols], _NT, preferred_element_type=_F32)
        p = jnp.exp2((s - jnp.max(s, axis=-1, keepdims=True)) * scale_log2e)
        l = jnp.sum(p, axis=-1, keepdims=True)
        o = jnp.dot(p.astype(_BF16), v_ref[:, cols], preferred_element_type=_F32) / l
        o_ref[:, cols] = (o * _silu(z_ref[:, cols].astype(_F32))).astype(o_ref.dtype)


def _memory_attention(proj, mem, w_mem_k, w_mem_v, seq, *, tm=1024):
    batch, n_mem, d = mem.shape
    proj3 = proj.reshape(batch, seq, PROJ_WIDTH)
    weight = pl.BlockSpec((d, MEM_WIDTH), lambda b, i: (0, 0), pipeline_mode=pl.Buffered(1))
    return pl.pallas_call(
        _mem_attn_kernel,
        grid=(batch, seq // tm),
        in_specs=[pl.BlockSpec((None, tm, MEM_WIDTH), lambda b, i: (b, i, Q_C)),
                  pl.BlockSpec((None, tm, MEM_WIDTH), lambda b, i: (b, i, Z_C)),
                  pl.BlockSpec((None, n_mem, d), lambda b, i: (b, 0, 0)),
                  weight, weight],
        out_specs=pl.BlockSpec((None, tm, MEM_WIDTH), lambda b, i: (b, i, 0)),
        out_shape=jax.ShapeDtypeStruct((batch, seq, MEM_WIDTH), _BF16),
        scratch_shapes=[pltpu.VMEM((n_mem, MEM_WIDTH), _BF16), pltpu.VMEM((n_mem, MEM_WIDTH), _BF16)],
        compiler_params=pltpu.CompilerParams(
            dimension_semantics=("arbitrary", "arbitrary"), vmem_limit_bytes=VMEM_LIMIT_BYTES),
        name="memory_attention",
    )(proj3, proj3, mem, w_mem_k, w_mem_v)


MERGE_TM = 512
OUT_NORM_TM = 512
OUT_NORM_CHUNKS = (256, 128, 128)
MERGE_ROW_CHUNK = 256


def _branch_merge_kernel(ya_ref, yc_ref, uvz_ref, ws_ref, bt_ref, lvg_ref, lvb_ref, ga_ref, gg_ref, gc_ref,
                         wa_ref, wg_ref, wc_ref, o_ref, yg_ref):
    u_ref, v_ref, z_ref = (uvz_ref.at[:, k * PROJ_TILE:(k + 1) * PROJ_TILE] for k in range(3))
    per_branch = D_MODEL // PROJ_TILE
    col_tiles = [slice(c * PROJ_TILE, (c + 1) * PROJ_TILE) for c in range(per_branch)]
    chunks = [slice(r * MERGE_ROW_CHUNK, (r + 1) * MERGE_ROW_CHUNK)
              for r in range(o_ref.shape[0] // MERGE_ROW_CHUNK)]
    w_causal = _sgu_causal_weights(ws_ref)

    def branch_dots(rows):
        n_sub = len(col_tiles)
        sub_rows = (rows.stop - rows.start) // n_sub
        outer = []
        for c, cols in enumerate(col_tiles):
            outer.append((jnp.dot(ya_ref[rows, :], wa_ref[:, cols], preferred_element_type=_F32),
                          jnp.dot(yc_ref[rows, :], wc_ref[:, cols], preferred_element_type=_F32)))
            sub = slice(rows.start + c * sub_rows, rows.start + (c + 1) * sub_rows)
            _sgu_rows(u_ref, v_ref, z_ref, w_causal, bt_ref, lvg_ref, lvb_ref, sub, yg_ref)
        inner = [jnp.dot(yg_ref[rows, :], wg_ref[:, cols], preferred_element_type=_F32) for cols in col_tiles]
        return outer, inner

    dots = branch_dots(chunks[0])
    for k, rows in enumerate(chunks):
        next_dots = branch_dots(chunks[k + 1]) if k + 1 < len(chunks) else None
        outer, inner = dots
        for c, cols in enumerate(col_tiles):
            merged = (ga_ref[rows, cols].astype(_F32) * outer[c][0]
                      + gg_ref[rows, cols].astype(_F32) * inner[c]
                      + gc_ref[rows, cols].astype(_F32) * outer[c][1])
            o_ref[rows, cols] = merged.astype(o_ref.dtype)
        dots = next_dots


def _out_norm_kernel(m_ref, x_ref, wo_ref, lng_ref, lnb_ref, o_ref):
    bounds = [0]
    for size in OUT_NORM_CHUNKS:
        bounds.append(bounds[-1] + size)
    assert bounds[-1] == o_ref.shape[0]
    chunks = [slice(a, b) for a, b in zip(bounds[:-1], bounds[1:])]

    def out_dot(rows):
        return jnp.dot(m_ref[rows, :], wo_ref[...], preferred_element_type=_F32)

    y = out_dot(chunks[0])
    for k, rows in enumerate(chunks):
        y_next = out_dot(chunks[k + 1]) if k + 1 < len(chunks) else None
        h = DN_ALPHA * x_ref[rows, :] + y
        mu = jnp.mean(h, axis=-1, keepdims=True)
        d = h - mu
        var = jnp.mean(d * d, axis=-1, keepdims=True)
        o_ref[rows, :] = d * lax.rsqrt(var + LN_EPS) * lng_ref[...] + lnb_ref[...]
        y = y_next


def _merge_project_norm(ya, yc, proj, sgu_params, gates, x2, wa, wg, wc, wo, ln_g, ln_b, *, tm=MERGE_TM):
    t = x2.shape[0]
    w_s, b_s, ln_v_g, ln_v_b = sgu_params
    assert t % tm == 0 and tm % MERGE_ROW_CHUNK == 0 and t % OUT_NORM_TM == 0 and OUT_NORM_TM % MERGE_ROW_CHUNK == 0
    branch = pl.BlockSpec((tm, ATT_WIDTH), lambda i: (i, 0))
    gate = lambda br: pl.BlockSpec((tm, D_MODEL), lambda i: (i, br))
    rows = pl.BlockSpec((tm, D_MODEL), lambda i: (i, 0))
    resident = lambda shape: pl.BlockSpec(shape, lambda i: (0, 0), pipeline_mode=pl.Buffered(1))
    params = pltpu.CompilerParams(dimension_semantics=("arbitrary",), vmem_limit_bytes=VMEM_LIMIT_BYTES)
    assert (V_G, Z_G) == (U_G + 1, U_G + 2) and U_G % 3 == 0
    whole = lambda shape: pl.BlockSpec(shape, lambda i: (0,) * len(shape))
    merged = pl.pallas_call(
        _branch_merge_kernel,
        grid=(t // tm,),
        in_specs=[branch, branch, pl.BlockSpec((tm, 3 * PROJ_TILE), lambda i: (i, U_G // 3)),
                  whole((SGU_GROUPS, SGU_CHUNK, SGU_CHUNK)), whole((SGU_CHUNK, SGU_GROUPS)),
                  whole((1, SGU_WIDTH)), whole((1, SGU_WIDTH)),
                  *[gate(br) for br in range(N_BRANCH)],
                  resident((ATT_WIDTH, D_MODEL)), resident((SGU_WIDTH, D_MODEL)), resident((MEM_WIDTH, D_MODEL))],
        out_specs=rows,
        out_shape=jax.ShapeDtypeStruct((t, D_MODEL), _BF16),
        scratch_shapes=[pltpu.VMEM((tm, SGU_WIDTH), _BF16)],
        compiler_params=params,
        name="branch_merge",
    )(ya, yc, proj, w_s, b_s.T, ln_v_g.reshape(1, -1), ln_v_b.reshape(1, -1),
      *([gates] * N_BRANCH), wa, wg, wc)
    out_rows = pl.BlockSpec((OUT_NORM_TM, D_MODEL), lambda i: (i, 0))
    return pl.pallas_call(
        _out_norm_kernel,
        grid=(t // OUT_NORM_TM,),
        in_specs=[out_rows, out_rows,
                  resident((D_MODEL, D_MODEL)), resident((1, D_MODEL)), resident((1, D_MODEL))],
        out_specs=out_rows,
        out_shape=jax.ShapeDtypeStruct((t, D_MODEL), _F32),
        compiler_params=params,
        name="out_proj_norm",
    )(merged, x2, wo, ln_g.reshape(1, -1), ln_b.reshape(1, -1))


def kernel(x, mem, w_in, w_mem_k, w_mem_v, w_s, b_s, ln_v_g, ln_v_b,
           w_branch_attn, w_branch_sgu, w_branch_mem, w_out, ln_g, ln_b):
    batch, seq, d = x.shape
    assert d == D_MODEL and seq % MOBA_BLOCK == 0 and mem.shape[1] == N_MEM
    t = batch * seq
    x2 = x.reshape(t, d)

    assert w_in.shape[1] == GATE_TILE0 * PROJ_TILE + N_BRANCH * D_MODEL and ATT_WIDTH == PROJ_TILE
    q_a, x_bf = _project(x2, w_in, tm=1024, tn=PROJ_TILE, name="in_proj_q", n_col_tiles=1,
                         scale=MOBA_Q_PRESCALE, row_chunks=4, emit_x_bf16=True)
    proj = _project(x_bf, w_in, tm=IN_PROJ_TM, tn=PROJ_TILE, name="in_proj_branches", first_col_tile=PROJ_TILE0,
                    n_col_tiles=GATE_TILE0 - PROJ_TILE0, row_chunks=IN_PROJ_ROW_CHUNKS)
    gates = _project(x_bf, w_in, tm=IN_PROJ_TM, tn=PROJ_TILE, name="in_proj_gates", first_col_tile=GATE_TILE0,
                     sigmoid=True, row_chunks=IN_PROJ_ROW_CHUNKS)

    ya, wa, wg, wc, wo = _moba_attention(q_a, proj, batch, seq,
                                         cast_jobs=(w_branch_attn, w_branch_sgu, w_branch_mem, w_out))
    ya = ya.reshape(t, ATT_WIDTH)
    yc = _memory_attention(proj, mem, w_mem_k, w_mem_v, seq).reshape(t, MEM_WIDTH)

    out = _merge_project_norm(ya, yc, proj, (w_s, b_s, ln_v_g, ln_v_b), gates, x2, wa, wg, wc, wo, ln_g, ln_b)
    return out.reshape(batch, seq, d)
```

```python
import functools

import jax
import jax.numpy as jnp
from jax import lax
from jax.experimental import pallas as pl
from jax.experimental.pallas import tpu as pltpu

D_MODEL = 2048
DEPTH = 1
N_MEM = 256
ATT_HEAD_DIM = 128
ATT_WIDTH = D_MODEL // 2
MOBA_BLOCK = 256
MOBA_TOPK = 3
SGU_WIDTH = D_MODEL // 2
SGU_CHUNK = 128
SGU_GROUP_DIM = 128
SGU_GROUPS = SGU_WIDTH // SGU_GROUP_DIM
MEM_HEADS = 4
MEM_WIDTH = D_MODEL // 2
MEM_HEAD_DIM = MEM_WIDTH // MEM_HEADS
N_BRANCH = 3
DN_ALPHA = (2 * DEPTH) ** 0.25
LN_EPS = 1e-5

PROJ_TILE = 1024
PROJ_TILE0 = 1
GATE_TILE0 = 9
PROJ_WIDTH = (GATE_TILE0 - PROJ_TILE0) * PROJ_TILE
K_A, V_A, Z_A, U_G, V_G, Z_G, Q_C, Z_C = range(8)

VMEM_LIMIT_BYTES = 56 * 1024 * 1024
IN_PROJ_TM = 2048
IN_PROJ_ROW_CHUNKS = 8
CAST_JOB_ROWS = 256

_NT = (((1,), (1,)), ((), ()))
_F32 = jnp.float32
_BF16 = jnp.bfloat16


_LOG2E = 1.4426950408889634


def _sigmoid(x):
    return 1.0 / (1.0 + jnp.exp2(x * -_LOG2E))


def _silu(x):
    return x * _sigmoid(x)


def _gelu_tanh(x):
    a = -2.0 * 0.7978845608028654 * _LOG2E
    return x / (1.0 + jnp.exp2(x * (a + (a * 0.044715) * (x * x))))


def _project_kernel(x_ref, w_ref, o_ref, *rest, scale, sigmoid, row_chunks, emit_x_bf16):
    x_bf_ref = rest[0] if emit_x_bf16 else None
    w_bf_ref = rest[-1]

    @pl.when(pl.program_id(1) == 0)
    def _cast_weight_tile():
        w_bf_ref[...] = w_ref[...].astype(_BF16)

    tc = x_ref.shape[0] // row_chunks
    for c in range(row_chunks):
        rows = slice(c * tc, (c + 1) * tc)
        xc = x_ref[rows, :]
        if emit_x_bf16:
            xc = xc.astype(_BF16)
            x_bf_ref[rows, :] = xc
        acc = jnp.dot(xc, w_bf_ref[...], preferred_element_type=_F32)
        if sigmoid:
            acc = _sigmoid(acc)
        if scale is not None:
            acc = acc * scale
        o_ref[rows, :] = acc.astype(o_ref.dtype)


def _project(x, w, *, tm, tn, name, first_col_tile=0, n_col_tiles=None,
             scale=None, sigmoid=False, row_chunks=1, emit_x_bf16=False):
    m, k = x.shape
    if n_col_tiles is None:
        n_col_tiles = w.shape[1] // tn - first_col_tile
    assert m % tm == 0 and tm % row_chunks == 0 and (first_col_tile + n_col_tiles) * tn <= w.shape[1]
    assert not emit_x_bf16 or n_col_tiles == 1
    out_specs = [pl.BlockSpec((tm, tn), lambda j, i: (i, j))]
    out_shape = [jax.ShapeDtypeStruct((m, n_col_tiles * tn), _BF16)]
    if emit_x_bf16:
        out_specs.append(pl.BlockSpec((tm, k), lambda j, i: (i, 0)))
        out_shape.append(jax.ShapeDtypeStruct((m, k), _BF16))
    outs = pl.pallas_call(
        functools.partial(_project_kernel, scale=scale, sigmoid=sigmoid, row_chunks=row_chunks,
                          emit_x_bf16=emit_x_bf16),
        grid=(n_col_tiles, m // tm),
        in_specs=[pl.BlockSpec((tm, k), lambda j, i: (i, 0)),
                  pl.BlockSpec((k, tn), lambda j, i: (0, first_col_tile + j))],
        out_specs=out_specs,
        out_shape=out_shape,
        scratch_shapes=[pltpu.VMEM((k, tn), _BF16)],
        compiler_params=pltpu.CompilerParams(
            dimension_semantics=("arbitrary", "arbitrary"), vmem_limit_bytes=VMEM_LIMIT_BYTES),
        name=name,
    )(x, w)
    return outs if emit_x_bf16 else outs[0]


def _cast_job_plan(arrays, n_steps, linear_step):
    specs, shapes, blocks, first = [], [], [], 0
    for a in arrays:
        assert a.shape[0] % CAST_JOB_ROWS == 0
        n_blocks = a.shape[0] // CAST_JOB_ROWS
        specs.append(pl.BlockSpec(
            (CAST_JOB_ROWS, a.shape[1]),
            lambda *idx, first=first, n_blocks=n_blocks: (jnp.clip(linear_step(*idx) - first, 0, n_blocks - 1), 0)))
        shapes.append(jax.ShapeDtypeStruct(a.shape, _BF16))
        blocks.append(n_blocks)
        first += n_blocks
    assert first <= n_steps
    return specs, shapes, tuple(blocks)


def _run_cast_jobs(step, job_in, job_out, blocks):
    first = 0
    for src, dst, n_blocks in zip(job_in, job_out, blocks):
        @pl.when((step >= first) & (step < first + n_blocks))
        def _cast_job_block(src=src, dst=dst):
            dst[...] = src[...].astype(_BF16)
        first += n_blocks


MOBA_HEADS_PER_STEP = 4
MOBA_Q_BLOCKS_PER_STEP = 2
MOBA_UNROLL = 8
_ACC_ROWS = ATT_HEAD_DIM + 16
MOBA_Q_PRESCALE = ATT_HEAD_DIM ** -0.5 * _LOG2E


def _moba_kernel(q_ref, k_ref, v_ref, z_ref, *rest, n_blocks, heads, cast_job_blocks):
    n_jobs = len(cast_job_blocks)
    job_in, o_ref, job_out = rest[:n_jobs], rest[n_jobs], rest[n_jobs + 1:2 * n_jobs + 1]
    vt_ref, kmean_ref, neg_ref, pos_ref, qt_ref, s_ref, acc_ref, smax_ref = rest[2 * n_jobs + 1:]
    blk = MOBA_BLOCK
    hd = ATT_HEAD_DIM
    step = (pl.program_id(0) * pl.num_programs(1) + pl.program_id(1)) * pl.num_programs(2) + pl.program_id(2)
    _run_cast_jobs(step, job_in, job_out, cast_job_blocks)

    @pl.when(pl.program_id(2) == 0)
    def _per_head_setup():
        for g in range(heads):
            cols = slice(g * hd, (g + 1) * hd)
            for j in range(n_blocks):
                rows = slice(j * blk, (j + 1) * blk)
                vt_ref[g, j, :hd, :] = v_ref[rows, cols].T
                vt_ref[g, j, hd:, :] = jnp.ones((_ACC_ROWS - hd, blk), _BF16)
                kmean_ref[g, j:j + 1, :] = (
                    jnp.sum(k_ref[rows, cols].astype(_F32), axis=0, keepdims=True) * (1.0 / blk))

    for sub in range(MOBA_Q_BLOCKS_PER_STEP):
        rows = slice(sub * blk, (sub + 1) * blk)
        _moba_query_block(pl.program_id(2) * MOBA_Q_BLOCKS_PER_STEP + sub, q_ref.at[rows], z_ref.at[rows],
                          o_ref.at[rows], k_ref, vt_ref, kmean_ref, neg_ref, pos_ref, qt_ref, s_ref, acc_ref, smax_ref,
                          n_blocks=n_blocks, heads=heads)


def _moba_query_block(qi, q_ref, z_ref, o_ref, k_ref, vt_ref, kmean_ref, neg_ref, pos_ref, qt_ref, s_ref, acc_ref, smax_ref,
                      *, n_blocks, heads):
    blk = MOBA_BLOCK
    hd = ATT_HEAD_DIM
    head_cols = [slice(g * hd, (g + 1) * hd) for g in range(heads)]
    qs = [q_ref[:, c] for c in head_cols]
    for g in range(heads):
        qt_ref[g] = qs[g].T

    def issue_scores(j, slot, causal=False):
        start = j * blk if isinstance(j, int) else pl.multiple_of(j * blk, blk)
        for g in range(heads):
            scores = jnp.dot(k_ref[pl.ds(start, blk), head_cols[g]], qt_ref[g],
                             preferred_element_type=_F32)
            if causal:
                key_pos = lax.broadcasted_iota(jnp.int32, scores.shape, 0)
                q_pos = lax.broadcasted_iota(jnp.int32, scores.shape, 1)
                scores = jnp.where(key_pos <= q_pos, scores, -jnp.inf)
            s_ref[slot, g] = scores
            smax_ref[slot * heads + g:slot * heads + g + 1, :] = jnp.max(scores, axis=0, keepdims=True)

    sel_scores = []
    for g in range(heads):
        km = kmean_ref[g]
        km_hi = km.astype(_BF16)
        km_lo = (km - km_hi.astype(_F32)).astype(_BF16)
        sel_scores.append(jnp.dot(km_hi, qt_ref[g], preferred_element_type=_F32)
                          + jnp.dot(km_lo, qt_ref[g], preferred_element_type=_F32))
    issue_scores(qi, 2, causal=True)
    row_id = lax.broadcasted_iota(jnp.int32, (n_blocks, blk), 0)
    past = row_id < qi
    later_rows = [jnp.where(row_id > j, 1.0, 0.0) for j in range(n_blocks)]
    for g in range(heads):
        sb = jnp.where(past, sel_scores[g], -jnp.inf)
        rank = jnp.zeros(sb.shape, _F32)
        for j in range(n_blocks):
            other = sb[j:j + 1, :]
            rank = rank + jnp.where(other > sb, 1.0, 0.0) + jnp.where(other == sb, later_rows[j], 0.0)
        selected = past & (rank < MOBA_TOPK)
        neg_ref[g] = jnp.where(selected, 0.0, -jnp.inf)
        pos_ref[g] = jnp.where(selected, -jnp.inf, jnp.inf)

    def accumulate(j, slot, ms, causal):
        probs, m_news = [], []
        for g in range(heads):
            s = s_ref[slot, g]
            block_max = smax_ref[slot * heads + g:slot * heads + g + 1, :]
            if causal:
                m_new = m_sub = block_max
            else:
                m_new = jnp.maximum(ms[g], block_max + neg_ref[g, pl.ds(j, 1), :])
                m_sub = jnp.maximum(m_new, pos_ref[g, pl.ds(j, 1), :])
            probs.append(jnp.exp2(s - m_sub).astype(_BF16))
            m_news.append(m_new)
        for g in range(heads):
            pv = jnp.dot(vt_ref[g, j], probs[g], preferred_element_type=_F32)
            if causal:
                acc_ref[g] = pv
            else:
                acc_ref[g] = jnp.exp2(ms[g] - m_news[g]) * acc_ref[g] + pv
        return m_news

    def run_blocks(first, n, ms, prefetch_after=True):
        for k in range(n):
            if prefetch_after or k + 1 < n:
                issue_scores(first + k + 1, (k + 1) % 2)
            ms = accumulate(first + k, k % 2, ms, causal=False)
        return ms

    issue_scores(0, 0)
    ms = accumulate(qi, 2, None, causal=True)
    ms = lax.fori_loop(0, qi // MOBA_UNROLL, lambda u, ms: run_blocks(MOBA_UNROLL * u, MOBA_UNROLL, ms), ms)

    first, rest = (qi // MOBA_UNROLL) * MOBA_UNROLL, qi % MOBA_UNROLL
    arm = MOBA_UNROLL // 2
    while arm >= 2:
        ms = lax.cond((rest & arm) != 0, functools.partial(run_blocks, first, arm), lambda ms: ms, ms)
        first = first + (rest & arm)
        arm //= 2
    last_block = first

    @pl.when((rest & 1) != 0)
    def _last_past_block():
        run_blocks(last_block, 1, ms, prefetch_after=False)

    for g in range(heads):
        attn = (acc_ref[g, :hd, :] * (1.0 / acc_ref[g, hd:hd + 1, :])).T
        o_ref[:, head_cols[g]] = (attn * _silu(z_ref[:, head_cols[g]].astype(_F32))).astype(o_ref.dtype)


def _moba_attention(q, proj, batch, seq, *, heads=MOBA_HEADS_PER_STEP, cast_jobs=()):
    n_blocks = seq // MOBA_BLOCK
    q3 = q.reshape(batch, seq, ATT_WIDTH)
    proj3 = proj.reshape(batch, seq, PROJ_WIDTH)
    width = heads * ATT_HEAD_DIM
    cols = PROJ_TILE // width
    q_rows = MOBA_Q_BLOCKS_PER_STEP * MOBA_BLOCK
    assert n_blocks % MOBA_Q_BLOCKS_PER_STEP == 0
    grid = (batch, ATT_WIDTH // width, n_blocks // MOBA_Q_BLOCKS_PER_STEP)
    job_specs, job_shapes, job_blocks = _cast_job_plan(
        cast_jobs, grid[0] * grid[1] * grid[2], lambda b, h, i: (b * grid[1] + h) * grid[2] + i)
    return pl.pallas_call(
        functools.partial(_moba_kernel, n_blocks=n_blocks, heads=heads, cast_job_blocks=job_blocks),
        grid=grid,
        in_specs=[pl.BlockSpec((None, q_rows, width), lambda b, h, i: (b, i, h)),
                  pl.BlockSpec((None, seq, width), lambda b, h, i: (b, 0, K_A * cols + h)),
                  pl.BlockSpec((None, seq, width), lambda b, h, i: (b, 0, V_A * cols + h)),
                  pl.BlockSpec((None, q_rows, width), lambda b, h, i: (b, i, Z_A * cols + h)),
                  *job_specs],
        out_specs=[pl.BlockSpec((None, q_rows, width), lambda b, h, i: (b, i, h)), *job_specs],
        out_shape=[jax.ShapeDtypeStruct((batch, seq, ATT_WIDTH), _BF16), *job_shapes],
        scratch_shapes=[pltpu.VMEM((heads, n_blocks, _ACC_ROWS, MOBA_BLOCK), _BF16),
                        pltpu.VMEM((heads, n_blocks, ATT_HEAD_DIM), _F32),
                        pltpu.VMEM((heads, n_blocks, MOBA_BLOCK), _F32),
                        pltpu.VMEM((heads, n_blocks, MOBA_BLOCK), _F32),
                        pltpu.VMEM((heads, ATT_HEAD_DIM, MOBA_BLOCK), _BF16),
                        pltpu.VMEM((3, heads, MOBA_BLOCK, MOBA_BLOCK), _F32),
                        pltpu.VMEM((heads, _ACC_ROWS, MOBA_BLOCK), _F32),
                        pltpu.VMEM((3 * heads, MOBA_BLOCK), _F32)],
        compiler_params=pltpu.CompilerParams(
            dimension_semantics=("arbitrary", "arbitrary", "arbitrary"), vmem_limit_bytes=VMEM_LIMIT_BYTES),
        name="moba_attention",
    )(q3, proj3, proj3, proj3, *cast_jobs)


def _sgu_causal_weights(w_ref):
    t_pos = lax.broadcasted_iota(jnp.int32, (SGU_CHUNK, SGU_CHUNK), 0)
    s_pos = lax.broadcasted_iota(jnp.int32, (SGU_CHUNK, SGU_CHUNK), 1)
    return [jnp.where(s_pos <= t_pos, w_ref[g], 0.0).astype(_BF16) for g in range(SGU_GROUPS)]


def _sgu_rows(u_ref, v_ref, z_ref, w_causal, bt_ref, g_ref, b_ref, rows, yg_ref):
    u = _gelu_tanh(u_ref[rows, :].astype(_F32))
    v = _gelu_tanh(v_ref[rows, :].astype(_F32))
    mu = jnp.mean(v, axis=-1, keepdims=True)
    d = v - mu
    var = jnp.mean(d * d, axis=-1, keepdims=True)
    vn = (d * lax.rsqrt(var + LN_EPS) * g_ref[...] + b_ref[...]).astype(_BF16)
    gate = u * _silu(z_ref[rows, :].astype(_F32))
    for g in range(SGU_GROUPS):
        bias = bt_ref[:, g:g + 1]
        cols = slice(g * SGU_GROUP_DIM, (g + 1) * SGU_GROUP_DIM)
        for c in range((rows.stop - rows.start) // SGU_CHUNK):
            sub = slice(c * SGU_CHUNK, (c + 1) * SGU_CHUNK)
            mixed = jnp.dot(w_causal[g], vn[sub, cols], preferred_element_type=_F32) + bias
            yg_ref[rows.start + c * SGU_CHUNK:rows.start + (c + 1) * SGU_CHUNK, cols] = (
                gate[sub, cols] * mixed).astype(yg_ref.dtype)


def _mem_attn_kernel(q_ref, z_ref, mem_ref, wk_ref, wv_ref, o_ref, k_ref, v_ref):
    @pl.when(pl.program_id(1) == 0)
    def _project_memory():
        mem_bf = mem_ref[...].astype(_BF16)
        for w_ref, dst in ((wk_ref, k_ref), (wv_ref, v_ref)):
            for c in range(0, MEM_WIDTH, MEM_HEAD_DIM):
                dst[:, c:c + MEM_HEAD_DIM] = jnp.dot(
                    mem_bf, w_ref[:, c:c + MEM_HEAD_DIM].astype(_BF16), preferred_element_type=_F32
                ).astype(_BF16)

    scale_log2e = MEM_HEAD_DIM ** -0.5 * _LOG2E
    for h in range(MEM_HEADS):
        cols = slice(h * MEM_HEAD_DIM, (h + 1) * MEM_HEAD_DIM)
        s = lax.dot_general(q_ref[:, cols], k_ref[:, cols], _NT, preferred_element_type=_F32)
        p = jnp.exp2((s - jnp.max(s, axis=-1, keepdims=True)) * scale_log2e)
        l = jnp.sum(p, axis=-1, keepdims=True)
        o = jnp.dot(p.astype(_BF16), v_ref[:, cols], preferred_element_type=_F32) / l
        o_ref[:, cols] = (o * _silu(z_ref[:, cols].astype(_F32))).astype(o_ref.dtype)


def _memory_attention(proj, mem, w_mem_k, w_mem_v, seq, *, tm=1024):
    batch, n_mem, d = mem.shape
    proj3 = proj.reshape(batch, seq, PROJ_WIDTH)
    weight = pl.BlockSpec((d, MEM_WIDTH), lambda b, i: (0, 0), pipeline_mode=pl.Buffered(1))
    return pl.pallas_call(
        _mem_attn_kernel,
        grid=(batch, seq // tm),
        in_specs=[pl.BlockSpec((None, tm, MEM_WIDTH), lambda b, i: (b, i, Q_C)),
                  pl.BlockSpec((None, tm, MEM_WIDTH), lambda b, i: (b, i, Z_C)),
                  pl.BlockSpec((None, n_mem, d), lambda b, i: (b, 0, 0)),
                  weight, weight],
        out_specs=pl.BlockSpec((None, tm, MEM_WIDTH), lambda b, i: (b, i, 0)),
        out_shape=jax.ShapeDtypeStruct((batch, seq, MEM_WIDTH), _BF16),
        scratch_shapes=[pltpu.VMEM((n_mem, MEM_WIDTH), _BF16), pltpu.VMEM((n_mem, MEM_WIDTH), _BF16)],
        compiler_params=pltpu.CompilerParams(
            dimension_semantics=("arbitrary", "arbitrary"), vmem_limit_bytes=VMEM_LIMIT_BYTES),
        name="memory_attention",
    )(proj3, proj3, mem, w_mem_k, w_mem_v)


MERGE_TM = 512
OUT_NORM_TM = 512
OUT_NORM_CHUNKS = (256, 128, 128)
MERGE_ROW_CHUNK = 256


def _branch_merge_kernel(ya_ref, yc_ref, uvz_ref, ws_ref, bt_ref, lvg_ref, lvb_ref, ga_ref, gg_ref, gc_ref,
                         wa_ref, wg_ref, wc_ref, o_ref, yg_ref):
    u_ref, v_ref, z_ref = (uvz_ref.at[:, k * PROJ_TILE:(k + 1) * PROJ_TILE] for k in range(3))
    per_branch = D_MODEL // PROJ_TILE
    col_tiles = [slice(c * PROJ_TILE, (c + 1) * PROJ_TILE) for c in range(per_branch)]
    chunks = [slice(r * MERGE_ROW_CHUNK, (r + 1) * MERGE_ROW_CHUNK)
              for r in range(o_ref.shape[0] // MERGE_ROW_CHUNK)]
    w_causal = _sgu_causal_weights(ws_ref)

    def branch_dots(rows):
        n_sub = len(col_tiles)
        sub_rows = (rows.stop - rows.start) // n_sub
        outer = []
        for c, cols in enumerate(col_tiles):
            outer.append((jnp.dot(ya_ref[rows, :], wa_ref[:, cols], preferred_element_type=_F32),
                          jnp.dot(yc_ref[rows, :], wc_ref[:, cols], preferred_element_type=_F32)))
            sub = slice(rows.start + c * sub_rows, rows.start + (c + 1) * sub_rows)
            _sgu_rows(u_ref, v_ref, z_ref, w_causal, bt_ref, lvg_ref, lvb_ref, sub, yg_ref)
        inner = [jnp.dot(yg_ref[rows, :], wg_ref[:, cols], preferred_element_type=_F32) for cols in col_tiles]
        return outer, inner

    dots = branch_dots(chunks[0])
    for k, rows in enumerate(chunks):
        next_dots = branch_dots(chunks[k + 1]) if k + 1 < len(chunks) else None
        outer, inner = dots
        for c, cols in enumerate(col_tiles):
            merged = (ga_ref[rows, cols].astype(_F32) * outer[c][0]
                      + gg_ref[rows, cols].astype(_F32) * inner[c]
                      + gc_ref[rows, cols].astype(_F32) * outer[c][1])
            o_ref[rows, cols] = merged.astype(o_ref.dtype)
        dots = next_dots


def _out_norm_kernel(m_ref, x_ref, wo_ref, lng_ref, lnb_ref, o_ref):
    bounds = [0]
    for size in OUT_NORM_CHUNKS:
        bounds.append(bounds[-1] + size)
    assert bounds[-1] == o_ref.shape[0]
    chunks = [slice(a, b) for a, b in zip(bounds[:-1], bounds[1:])]

    def out_dot(rows):
        return jnp.dot(m_ref[rows, :], wo_ref[...], preferred_element_type=_F32)

    y = out_dot(chunks[0])
    for k, rows in enumerate(chunks):
        y_next = out_dot(chunks[k + 1]) if k + 1 < len(chunks) else None
        h = DN_ALPHA * x_ref[rows, :] + y
        mu = jnp.mean(h, axis=-1, keepdims=True)
        d = h - mu
        var = jnp.mean(d * d, axis=-1, keepdims=True)
        o_ref[rows, :] = d * lax.rsqrt(var + LN_EPS) * lng_ref[...] + lnb_ref[...]
        y = y_next


def _merge_project_norm(ya, yc, proj, sgu_params, gates, x2, wa, wg, wc, wo, ln_g, ln_b, *, tm=MERGE_TM):
    t = x2.shape[0]
    w_s, b_s, ln_v_g, ln_v_b = sgu_params
    assert t % tm == 0 and tm % MERGE_ROW_CHUNK == 0 and t % OUT_NORM_TM == 0 and OUT_NORM_TM % MERGE_ROW_CHUNK == 0
    branch = pl.BlockSpec((tm, ATT_WIDTH), lambda i: (i, 0))
    gate = lambda br: pl.BlockSpec((tm, D_MODEL), lambda i: (i, br))
    rows = pl.BlockSpec((tm, D_MODEL), lambda i: (i, 0))
    resident = lambda shape: pl.BlockSpec(shape, lambda i: (0, 0), pipeline_mode=pl.Buffered(1))
    params = pltpu.CompilerParams(dimension_semantics=("arbitrary",), vmem_limit_bytes=VMEM_LIMIT_BYTES)
    assert (V_G, Z_G) == (U_G + 1, U_G + 2) and U_G % 3 == 0
    whole = lambda shape: pl.BlockSpec(shape, lambda i: (0,) * len(shape))
    merged = pl.pallas_call(
        _branch_merge_kernel,
        grid=(t // tm,),
        in_specs=[branch, branch, pl.BlockSpec((tm, 3 * PROJ_TILE), lambda i: (i, U_G // 3)),
                  whole((SGU_GROUPS, SGU_CHUNK, SGU_CHUNK)), whole((SGU_CHUNK, SGU_GROUPS)),
                  whole((1, SGU_WIDTH)), whole((1, SGU_WIDTH)),
                  *[gate(br) for br in range(N_BRANCH)],
                  resident((ATT_WIDTH, D_MODEL)), resident((SGU_WIDTH, D_MODEL)), resident((MEM_WIDTH, D_MODEL))],
        out_specs=rows,
        out_shape=jax.ShapeDtypeStruct((t, D_MODEL), _BF16),
        scratch_shapes=[pltpu.VMEM((tm, SGU_WIDTH), _BF16)],
        compiler_params=params,
        name="branch_merge",
    )(ya, yc, proj, w_s, b_s.T, ln_v_g.reshape(1, -1), ln_v_b.reshape(1, -1),
      *([gates] * N_BRANCH), wa, wg, wc)
    out_rows = pl.BlockSpec((OUT_NORM_TM, D_MODEL), lambda i: (i, 0))
    return pl.pallas_call(
        _out_norm_kernel,
        grid=(t // OUT_NORM_TM,),
        in_specs=[out_rows, out_rows,
                  resident((D_MODEL, D_MODEL)), resident((1, D_MODEL)), resident((1, D_MODEL))],
        out_specs=out_rows,
        out_shape=jax.ShapeDtypeStruct((t, D_MODEL), _F32),
        compiler_params=params,
        name="out_proj_norm",
    )(merged, x2, wo, ln_g.reshape(1, -1), ln_b.reshape(1, -1))


def kernel(x, mem, w_in, w_mem_k, w_mem_v, w_s, b_s, ln_v_g, ln_v_b,
           w_branch_attn, w_branch_sgu, w_branch_mem, w_out, ln_g, ln_b):
    batch, seq, d = x.shape
    assert d == D_MODEL and seq % MOBA_BLOCK == 0 and mem.shape[1] == N_MEM
    t = batch * seq
    x2 = x.reshape(t, d)

    assert w_in.shape[1] == GATE_TILE0 * PROJ_TILE + N_BRANCH * D_MODEL and ATT_WIDTH == PROJ_TILE
    q_a, x_bf = _project(x2, w_in, tm=1024, tn=PROJ_TILE, name="in_proj_q", n_col_tiles=1,
                         scale=MOBA_Q_PRESCALE, row_chunks=4, emit_x_bf16=True)
    proj = _project(x_bf, w_in, tm=IN_PROJ_TM, tn=PROJ_TILE, name="in_proj_branches", first_col_tile=PROJ_TILE0,
                    n_col_tiles=GATE_TILE0 - PROJ_TILE0, row_chunks=IN_PROJ_ROW_CHUNKS)
    gates = _project(x_bf, w_in, tm=IN_PROJ_TM, tn=PROJ_TILE, name="in_proj_gates", first_col_tile=GATE_TILE0,
                     sigmoid=True, row_chunks=IN_PROJ_ROW_CHUNKS)

    ya, wa, wg, wc, wo = _moba_attention(q_a, proj, batch, seq,
                                         cast_jobs=(w_branch_attn, w_branch_sgu, w_branch_mem, w_out))
    ya = ya.reshape(t, ATT_WIDTH)
    yc = _memory_attention(proj, mem, w_mem_k, w_mem_v, seq).reshape(t, MEM_WIDTH)

    out = _merge_project_norm(ya, yc, proj, (w_s, b_s, ln_v_g, ln_v_b), gates, x2, wa, wg, wc, wo, ln_g, ln_b)
    return out.reshape(batch, seq, d)
```

```python
import functools

import jax
import jax.numpy as jnp
from jax import lax
from jax.experimental import pallas as pl
from jax.experimental.pallas import tpu as pltpu

D_MODEL = 2048
DEPTH = 1
N_MEM = 256
ATT_HEAD_DIM = 128
ATT_WIDTH = D_MODEL // 2
MOBA_BLOCK = 256
MOBA_TOPK = 3
SGU_WIDTH = D_MODEL // 2
SGU_CHUNK = 128
SGU_GROUP_DIM = 128
SGU_GROUPS = SGU_WIDTH // SGU_GROUP_DIM
MEM_HEADS = 4
MEM_WIDTH = D_MODEL // 2
MEM_HEAD_DIM = MEM_WIDTH // MEM_HEADS
N_BRANCH = 3
DN_ALPHA = (2 * DEPTH) ** 0.25
LN_EPS = 1e-5

PROJ_TILE = 1024
PROJ_TILE0 = 1
GATE_TILE0 = 9
PROJ_WIDTH = (GATE_TILE0 - PROJ_TILE0) * PROJ_TILE
K_A, V_A, Z_A, U_G, V_G, Z_G, Q_C, Z_C = range(8)

VMEM_LIMIT_BYTES = 56 * 1024 * 1024
IN_PROJ_TM = 2048
IN_PROJ_ROW_CHUNKS = 8
CAST_JOB_ROWS = 256

_NT = (((1,), (1,)), ((), ()))
_F32 = jnp.float32
_BF16 = jnp.bfloat16


_LOG2E = 1.4426950408889634


def _sigmoid(x):
    return 1.0 / (1.0 + jnp.exp2(x * -_LOG2E))


def _silu(x):
    return x * _sigmoid(x)


def _gelu_tanh(x):
    a = -2.0 * 0.7978845608028654 * _LOG2E
    return x / (1.0 + jnp.exp2(x * (a + (a * 0.044715) * (x * x))))


def _project_kernel(x_ref, w_ref, o_ref, *rest, scale, sigmoid, row_chunks, emit_x_bf16):
    x_bf_ref = rest[0] if emit_x_bf16 else None
    w_bf_ref = rest[-1]

    @pl.when(pl.program_id(1) == 0)
    def _cast_weight_tile():
        w_bf_ref[...] = w_ref[...].astype(_BF16)

    tc = x_ref.shape[0] // row_chunks
    for c in range(row_chunks):
        rows = slice(c * tc, (c + 1) * tc)
        xc = x_ref[rows, :]
        if emit_x_bf16:
            xc = xc.astype(_BF16)
            x_bf_ref[rows, :] = xc
        acc = jnp.dot(xc, w_bf_ref[...], preferred_element_type=_F32)
        if sigmoid:
            acc = _sigmoid(acc)
        if scale is not None:
            acc = acc * scale
        o_ref[rows, :] = acc.astype(o_ref.dtype)


def _project(x, w, *, tm, tn, name, first_col_tile=0, n_col_tiles=None,
             scale=None, sigmoid=False, row_chunks=1, emit_x_bf16=False):
    m, k = x.shape
    if n_col_tiles is None:
        n_col_tiles = w.shape[1] // tn - first_col_tile
    assert m % tm == 0 and tm % row_chunks == 0 and (first_col_tile + n_col_tiles) * tn <= w.shape[1]
    assert not emit_x_bf16 or n_col_tiles == 1
    out_specs = [pl.BlockSpec((tm, tn), lambda j, i: (i, j))]
    out_shape = [jax.ShapeDtypeStruct((m, n_col_tiles * tn), _BF16)]
    if emit_x_bf16:
        out_specs.append(pl.BlockSpec((tm, k), lambda j, i: (i, 0)))
        out_shape.append(jax.ShapeDtypeStruct((m, k), _BF16))
    outs = pl.pallas_call(
        functools.partial(_project_kernel, scale=scale, sigmoid=sigmoid, row_chunks=row_chunks,
                          emit_x_bf16=emit_x_bf16),
        grid=(n_col_tiles, m // tm),
        in_specs=[pl.BlockSpec((tm, k), lambda j, i: (i, 0)),
                  pl.BlockSpec((k, tn), lambda j, i: (0, first_col_tile + j))],
        out_specs=out_specs,
        out_shape=out_shape,
        scratch_shapes=[pltpu.VMEM((k, tn), _BF16)],
        compiler_params=pltpu.CompilerParams(
            dimension_semantics=("arbitrary", "arbitrary"), vmem_limit_bytes=VMEM_LIMIT_BYTES),
        name=name,
    )(x, w)
    return outs if emit_x_bf16 else outs[0]


def _cast_job_plan(arrays, n_steps, linear_step):
    specs, shapes, blocks, first = [], [], [], 0
    for a in arrays:
        assert a.shape[0] % CAST_JOB_ROWS == 0
        n_blocks = a.shape[0] // CAST_JOB_ROWS
        specs.append(pl.BlockSpec(
            (CAST_JOB_ROWS, a.shape[1]),
            lambda *idx, first=first, n_blocks=n_blocks: (jnp.clip(linear_step(*idx) - first, 0, n_blocks - 1), 0)))
        shapes.append(jax.ShapeDtypeStruct(a.shape, _BF16))
        blocks.append(n_blocks)
        first += n_blocks
    assert first <= n_steps
    return specs, shapes, tuple(blocks)


def _run_cast_jobs(step, job_in, job_out, blocks):
    first = 0
    for src, dst, n_blocks in zip(job_in, job_out, blocks):
        @pl.when((step >= first) & (step < first + n_blocks))
        def _cast_job_block(src=src, dst=dst):
            dst[...] = src[...].astype(_BF16)
        first += n_blocks


MOBA_HEADS_PER_STEP = 4
MOBA_Q_BLOCKS_PER_STEP = 2
MOBA_UNROLL = 8
_ACC_ROWS = ATT_HEAD_DIM + 16
MOBA_Q_PRESCALE = ATT_HEAD_DIM ** -0.5 * _LOG2E


def _moba_kernel(q_ref, k_ref, v_ref, z_ref, *rest, n_blocks, heads, cast_job_blocks):
    n_jobs = len(cast_job_blocks)
    job_in, o_ref, job_out = rest[:n_jobs], rest[n_jobs], rest[n_jobs + 1:2 * n_jobs + 1]
    vt_ref, kmean_ref, neg_ref, pos_ref, qt_ref, s_ref, acc_ref, smax_ref = rest[2 * n_jobs + 1:]
    blk = MOBA_BLOCK
    hd = ATT_HEAD_DIM
    step = (pl.program_id(0) * pl.num_programs(1) + pl.program_id(1)) * pl.num_programs(2) + pl.program_id(2)
    _run_cast_jobs(step, job_in, job_out, cast_job_blocks)

    @pl.when(pl.program_id(2) == 0)
    def _per_head_setup():
        for g in range(heads):
            cols = slice(g * hd, (g + 1) * hd)
            for j in range(n_blocks):
                rows = slice(j * blk, (j + 1) * blk)
                vt_ref[g, j, :hd, :] = v_ref[rows, cols].T
                vt_ref[g, j, hd:, :] = jnp.ones((_ACC_ROWS - hd, blk), _BF16)
                kmean_ref[g, j:j + 1, :] = (
                    jnp.sum(k_ref[rows, cols].astype(_F32), axis=0, keepdims=True) * (1.0 / blk))

    for sub in range(MOBA_Q_BLOCKS_PER_STEP):
        rows = slice(sub * blk, (sub + 1) * blk)
        _moba_query_block(pl.program_id(2) * MOBA_Q_BLOCKS_PER_STEP + sub, q_ref.at[rows], z_ref.at[rows],
                          o_ref.at[rows], k_ref, vt_ref, kmean_ref, neg_ref, pos_ref, qt_ref, s_ref, acc_ref, smax_ref,
                          n_blocks=n_blocks, heads=heads)


def _moba_query_block(qi, q_ref, z_ref, o_ref, k_ref, vt_ref, kmean_ref, neg_ref, pos_ref, qt_ref, s_ref, acc_ref, smax_ref,
                      *, n_blocks, heads):
    blk = MOBA_BLOCK
    hd = ATT_HEAD_DIM
    head_cols = [slice(g * hd, (g + 1) * hd) for g in range(heads)]
    qs = [q_ref[:, c] for c in head_cols]
    for g in range(heads):
        qt_ref[g] = qs[g].T

    def issue_scores(j, slot, causal=False):
        start = j * blk if isinstance(j, int) else pl.multiple_of(j * blk, blk)
        for g in range(heads):
            scores = jnp.dot(k_ref[pl.ds(start, blk), head_cols[g]], qt_ref[g],
                             preferred_element_type=_F32)
            if causal:
                key_pos = lax.broadcasted_iota(jnp.int32, scores.shape, 0)
                q_pos = lax.broadcasted_iota(jnp.int32, scores.shape, 1)
                scores = jnp.where(key_pos <= q_pos, scores, -jnp.inf)
            s_ref[slot, g] = scores
            smax_ref[slot * heads + g:slot * heads + g + 1, :] = jnp.max(scores, axis=0, keepdims=True)

    sel_scores = []
    for g in range(heads):
        km = kmean_ref[g]
        km_hi = km.astype(_BF16)
        km_lo = (km - km_hi.astype(_F32)).astype(_BF16)
        sel_scores.append(jnp.dot(km_hi, qt_ref[g], preferred_element_type=_F32)
                          + jnp.dot(km_lo, qt_ref[g], preferred_element_type=_F32))
    issue_scores(qi, 2, causal=True)
    row_id = lax.broadcasted_iota(jnp.int32, (n_blocks, blk), 0)
    row_f = row_id.astype(_F32)
    past = row_id < qi
    for g in range(heads):
        sb = jnp.where(past, sel_scores[g], -jnp.inf)
        taken = jnp.zeros(sb.shape, _F32)
        for _ in range(MOBA_TOPK):
            top = jnp.max(sb, axis=0, keepdims=True)
            first = jnp.min(jnp.where(sb == top, row_f, float(n_blocks)), axis=0, keepdims=True)
            hit = row_f == first
            taken = jnp.where(hit, 1.0, taken)
            sb = jnp.where(hit, -jnp.inf, sb)
        selected = past & (taken > 0.0)
        neg_ref[g] = jnp.where(selected, 0.0, -jnp.inf)
        pos_ref[g] = jnp.where(selected, -jnp.inf, jnp.inf)

    def accumulate(j, slot, ms, causal):
        probs, m_news = [], []
        for g in range(heads):
            s = s_ref[slot, g]
            block_max = smax_ref[slot * heads + g:slot * heads + g + 1, :]
            if causal:
                m_new = m_sub = block_max
            else:
                m_new = jnp.maximum(ms[g], block_max + neg_ref[g, pl.ds(j, 1), :])
                m_sub = jnp.maximum(m_new, pos_ref[g, pl.ds(j, 1), :])
            probs.append(jnp.exp2(s - m_sub).astype(_BF16))
            m_news.append(m_new)
        for g in range(heads):
            pv = jnp.dot(vt_ref[g, j], probs[g], preferred_element_type=_F32)
            if causal:
                acc_ref[g] = pv
            else:
                acc_ref[g] = jnp.exp2(ms[g] - m_news[g]) * acc_ref[g] + pv
        return m_news

    def run_blocks(first, n, ms, prefetch_after=True):
        for k in range(n):
            if prefetch_after or k + 1 < n:
                issue_scores(first + k + 1, (k + 1) % 2)
            ms = accumulate(first + k, k % 2, ms, causal=False)
        return ms

    issue_scores(0, 0)
    ms = accumulate(qi, 2, None, causal=True)
    ms = lax.fori_loop(0, qi // MOBA_UNROLL, lambda u, ms: run_blocks(MOBA_UNROLL * u, MOBA_UNROLL, ms), ms)

    first, rest = (qi // MOBA_UNROLL) * MOBA_UNROLL, qi % MOBA_UNROLL
    arm = MOBA_UNROLL // 2
    while arm >= 2:
        ms = lax.cond((rest & arm) != 0, functools.partial(run_blocks, first, arm), lambda ms: ms, ms)
        first = first + (rest & arm)
        arm //= 2
    last_block = first

    @pl.when((rest & 1) != 0)
    def _last_past_block():
        run_blocks(last_block, 1, ms, prefetch_after=False)

    for g in range(heads):
        attn = (acc_ref[g, :hd, :] * (1.0 / acc_ref[g, hd:hd + 1, :])).T
        o_ref[:, head_cols[g]] = (attn * _silu(z_ref[:, head_cols[g]].astype(_F32))).astype(o_ref.dtype)


def _moba_attention(q, proj, batch, seq, *, heads=MOBA_HEADS_PER_STEP, cast_jobs=()):
    n_blocks = seq // MOBA_BLOCK
    q3 = q.reshape(batch, seq, ATT_WIDTH)
    proj3 = proj.reshape(batch, seq, PROJ_WIDTH)
    width = heads * ATT_HEAD_DIM
    cols = PROJ_TILE // width
    q_rows = MOBA_Q_BLOCKS_PER_STEP * MOBA_BLOCK
    assert n_blocks % MOBA_Q_BLOCKS_PER_STEP == 0
    grid = (batch, ATT_WIDTH // width, n_blocks // MOBA_Q_BLOCKS_PER_STEP)
    job_specs, job_shapes, job_blocks = _cast_job_plan(
        cast_jobs, grid[0] * grid[1] * grid[2], lambda b, h, i: (b * grid[1] + h) * grid[2] + i)
    return pl.pallas_call(
        functools.partial(_moba_kernel, n_blocks=n_blocks, heads=heads, cast_job_blocks=job_blocks),
        grid=grid,
        in_specs=[pl.BlockSpec((None, q_rows, width), lambda b, h, i: (b, i, h)),
                  pl.BlockSpec((None, seq, width), lambda b, h, i: (b, 0, K_A * cols + h)),
                  pl.BlockSpec((None, seq, width), lambda b, h, i: (b, 0, V_A * cols + h)),
                  pl.BlockSpec((None, q_rows, width), lambda b, h, i: (b, i, Z_A * cols + h)),
                  *job_specs],
        out_specs=[pl.BlockSpec((None, q_rows, width), lambda b, h, i: (b, i, h)), *job_specs],
        out_shape=[jax.ShapeDtypeStruct((batch, seq, ATT_WIDTH), _BF16), *job_shapes],
        scratch_shapes=[pltpu.VMEM((heads, n_blocks, _ACC_ROWS, MOBA_BLOCK), _BF16),
                        pltpu.VMEM((heads, n_blocks, ATT_HEAD_DIM), _F32),
                        pltpu.VMEM((heads, n_blocks, MOBA_BLOCK), _F32),
                        pltpu.VMEM((heads, n_blocks, MOBA_BLOCK), _F32),
                        pltpu.VMEM((heads, ATT_HEAD_DIM, MOBA_BLOCK), _BF16),
                        pltpu.VMEM((3, heads, MOBA_BLOCK, MOBA_BLOCK), _F32),
                        pltpu.VMEM((heads, _ACC_ROWS, MOBA_BLOCK), _F32),
                        pltpu.VMEM((3 * heads, MOBA_BLOCK), _F32)],
        compiler_params=pltpu.CompilerParams(
            dimension_semantics=("arbitrary", "arbitrary", "arbitrary"), vmem_limit_bytes=VMEM_LIMIT_BYTES),
        name="moba_attention",
    )(q3, proj3, proj3, proj3, *cast_jobs)


def _sgu_causal_weights(w_ref):
    t_pos = lax.broadcasted_iota(jnp.int32, (SGU_CHUNK, SGU_CHUNK), 0)
    s_pos = lax.broadcasted_iota(jnp.int32, (SGU_CHUNK, SGU_CHUNK), 1)
    return [jnp.where(s_pos <= t_pos, w_ref[g], 0.0).astype(_BF16) for g in range(SGU_GROUPS)]


def _sgu_rows(u_ref, v_ref, z_ref, w_causal, bt_ref, g_ref, b_ref, rows, yg_ref):
    u = _gelu_tanh(u_ref[rows, :].astype(_F32))
    v = _gelu_tanh(v_ref[rows, :].astype(_F32))
    mu = jnp.mean(v, axis=-1, keepdims=True)
    d = v - mu
    var = jnp.mean(d * d, axis=-1, keepdims=True)
    vn = (d * lax.rsqrt(var + LN_EPS) * g_ref[...] + b_ref[...]).astype(_BF16)
    gate = u * _silu(z_ref[rows, :].astype(_F32))
    for g in range(SGU_GROUPS):
        bias = bt_ref[:, g:g + 1]
        cols = slice(g * SGU_GROUP_DIM, (g + 1) * SGU_GROUP_DIM)
        for c in range((rows.stop - rows.start) // SGU_CHUNK):
            sub = slice(c * SGU_CHUNK, (c + 1) * SGU_CHUNK)
            mixed = jnp.dot(w_causal[g], vn[sub, cols], preferred_element_type=_F32) + bias
            yg_ref[rows.start + c * SGU_CHUNK:rows.start + (c + 1) * SGU_CHUNK, cols] = (
                gate[sub, cols] * mixed).astype(yg_ref.dtype)


def _mem_attn_kernel(q_ref, z_ref, mem_ref, wk_ref, wv_ref, o_ref, k_ref, v_ref):
    @pl.when(pl.program_id(1) == 0)
    def _project_memory():
        mem_bf = mem_ref[...].astype(_BF16)
        for w_ref, dst in ((wk_ref, k_ref), (wv_ref, v_ref)):
            for c in range(0, MEM_WIDTH, MEM_HEAD_DIM):
                dst[:, c:c + MEM_HEAD_DIM] = jnp.dot(
                    mem_bf, w_ref[:, c:c + MEM_HEAD_DIM].astype(_BF16), preferred_element_type=_F32
                ).astype(_BF16)

    scale_log2e = MEM_HEAD_DIM ** -0.5 * _LOG2E
    for h in range(MEM_HEADS):
        cols = slice(h * MEM_HEAD_DIM, (h + 1) * MEM_HEAD_DIM)
        s = lax.dot_general(q_ref[:, cols], k_ref[:, cols], _NT, preferred_element_type=_F32)
        p = jnp.exp2((s - jnp.max(s, axis=-1, keepdims=True)) * scale_log2e)
        l = jnp.sum(p, axis=-1, keepdims=True)
        o = jnp.dot(p.astype(_BF16), v_ref[:, cols], preferred_element_type=_F32) / l
        o_ref[:, cols] = (o * _silu(z_ref[:, cols].astype(_F32))).astype(o_ref.dtype)


def _memory_attention(proj, mem, w_mem_k, w_mem_v, seq, *, tm=1024):
    batch, n_mem, d = mem.shape
    proj3 = proj.reshape(batch, seq, PROJ_WIDTH)
    weight = pl.BlockSpec((d, MEM_WIDTH), lambda b, i: (0, 0), pipeline_mode=pl.Buffered(1))
    return pl.pallas_call(
        _mem_attn_kernel,
        grid=(batch, seq // tm),
        in_specs=[pl.BlockSpec((None, tm, MEM_WIDTH), lambda b, i: (b, i, Q_C)),
                  pl.BlockSpec((None, tm, MEM_WIDTH), lambda b, i: (b, i, Z_C)),
                  pl.BlockSpec((None, n_mem, d), lambda b, i: (b, 0, 0)),
                  weight, weight],
        out_specs=pl.BlockSpec((None, tm, MEM_WIDTH), lambda b, i: (b, i, 0)),
        out_shape=jax.ShapeDtypeStruct((batch, seq, MEM_WIDTH), _BF16),
        scratch_shapes=[pltpu.VMEM((n_mem, MEM_WIDTH), _BF16), pltpu.VMEM((n_mem, MEM_WIDTH), _BF16)],
        compiler_params=pltpu.CompilerParams(
            dimension_semantics=("arbitrary", "arbitrary"), vmem_limit_bytes=VMEM_LIMIT_BYTES),
        name="memory_attention",
    )(proj3, proj3, mem, w_mem_k, w_mem_v)


MERGE_TM = 512
OUT_NORM_TM = 512
OUT_NORM_CHUNKS = (256, 128, 128)
MERGE_ROW_CHUNK = 256


def _branch_merge_kernel(ya_ref, yc_ref, uvz_ref, ws_ref, bt_ref, lvg_ref, lvb_ref, ga_ref, gg_ref, gc_ref,
                         wa_ref, wg_ref, wc_ref, o_ref, yg_ref):
    u_ref, v_ref, z_ref = (uvz_ref.at[:, k * PROJ_TILE:(k + 1) * PROJ_TILE] for k in range(3))
    per_branch = D_MODEL // PROJ_TILE
    col_tiles = [slice(c * PROJ_TILE, (c + 1) * PROJ_TILE) for c in range(per_branch)]
    chunks = [slice(r * MERGE_ROW_CHUNK, (r + 1) * MERGE_ROW_CHUNK)
              for r in range(o_ref.shape[0] // MERGE_ROW_CHUNK)]
    w_causal = _sgu_causal_weights(ws_ref)

    def branch_dots(rows):
        n_sub = len(col_tiles)
        sub_rows = (rows.stop - rows.start) // n_sub
        outer = []
        for c, cols in enumerate(col_tiles):
            outer.append((jnp.dot(ya_ref[rows, :], wa_ref[:, cols], preferred_element_type=_F32),
                          jnp.dot(yc_ref[rows, :], wc_ref[:, cols], preferred_element_type=_F32)))
            sub = slice(rows.start + c * sub_rows, rows.start + (c + 1) * sub_rows)
            _sgu_rows(u_ref, v_ref, z_ref, w_causal, bt_ref, lvg_ref, lvb_ref, sub, yg_ref)
        inner = [jnp.dot(yg_ref[rows, :], wg_ref[:, cols], preferred_element_type=_F32) for cols in col_tiles]
        return outer, inner

    dots = branch_dots(chunks[0])
    for k, rows in enumerate(chunks):
        next_dots = branch_dots(chunks[k + 1]) if k + 1 < len(chunks) else None
        outer, inner = dots
        for c, cols in enumerate(col_tiles):
            merged = (ga_ref[rows, cols].astype(_F32) * outer[c][0]
                      + gg_ref[rows, cols].astype(_F32) * inner[c]
                      + gc_ref[rows, cols].astype(_F32) * outer[c][1])
            o_ref[rows, cols] = merged.astype(o_ref.dtype)
        dots = next_dots


def _out_norm_kernel(m_ref, x_ref, wo_ref, lng_ref, lnb_ref, o_ref):
    bounds = [0]
    for size in OUT_NORM_CHUNKS:
        bounds.append(bounds[-1] + size)
    assert bounds[-1] == o_ref.shape[0]
    chunks = [slice(a, b) for a, b in zip(bounds[:-1], bounds[1:])]

    def out_dot(rows):
        return jnp.dot(m_ref[rows, :], wo_ref[...], preferred_element_type=_F32)

    y = out_dot(chunks[0])
    for k, rows in enumerate(chunks):
        y_next = out_dot(chunks[k + 1]) if k + 1 < len(chunks) else None
        h = DN_ALPHA * x_ref[rows, :] + y
        mu = jnp.mean(h, axis=-1, keepdims=True)
        d = h - mu
        var = jnp.mean(d * d, axis=-1, keepdims=True)
        o_ref[rows, :] = d * lax.rsqrt(var + LN_EPS) * lng_ref[...] + lnb_ref[...]
        y = y_next


def _merge_project_norm(ya, yc, proj, sgu_params, gates, x2, wa, wg, wc, wo, ln_g, ln_b, *, tm=MERGE_TM):
    t = x2.shape[0]
    w_s, b_s, ln_v_g, ln_v_b = sgu_params
    assert t % tm == 0 and tm % MERGE_ROW_CHUNK == 0 and t % OUT_NORM_TM == 0 and OUT_NORM_TM % MERGE_ROW_CHUNK == 0
    branch = pl.BlockSpec((tm, ATT_WIDTH), lambda i: (i, 0))
    gate = lambda br: pl.BlockSpec((tm, D_MODEL), lambda i: (i, br))
    rows = pl.BlockSpec((tm, D_MODEL), lambda i: (i, 0))
    resident = lambda shape: pl.BlockSpec(shape, lambda i: (0, 0), pipeline_mode=pl.Buffered(1))
    params = pltpu.CompilerParams(dimension_semantics=("arbitrary",), vmem_limit_bytes=VMEM_LIMIT_BYTES)
    assert (V_G, Z_G) == (U_G + 1, U_G + 2) and U_G % 3 == 0
    whole = lambda shape: pl.BlockSpec(shape, lambda i: (0,) * len(shape))
    merged = pl.pallas_call(
        _branch_merge_kernel,
        grid=(t // tm,),
        in_specs=[branch, branch, pl.BlockSpec((tm, 3 * PROJ_TILE), lambda i: (i, U_G // 3)),
                  whole((SGU_GROUPS, SGU_CHUNK, SGU_CHUNK)), whole((SGU_CHUNK, SGU_GROUPS)),
                  whole((1, SGU_WIDTH)), whole((1, SGU_WIDTH)),
                  *[gate(br) for br in range(N_BRANCH)],
                  resident((ATT_WIDTH, D_MODEL)), resident((SGU_WIDTH, D_MODEL)), resident((MEM_WIDTH, D_MODEL))],
        out_specs=rows,
        out_shape=jax.ShapeDtypeStruct((t, D_MODEL), _BF16),
        scratch_shapes=[pltpu.VMEM((tm, SGU_WIDTH), _BF16)],
        compiler_params=params,
        name="branch_merge",
    )(ya, yc, proj, w_s, b_s.T, ln_v_g.reshape(1, -1), ln_v_b.reshape(1, -1),
      *([gates] * N_BRANCH), wa, wg, wc)
    out_rows = pl.BlockSpec((OUT_NORM_TM, D_MODEL), lambda i: (i, 0))
    return pl.pallas_call(
        _out_norm_kernel,
        grid=(t // OUT_NORM_TM,),
        in_specs=[out_rows, out_rows,
                  resident((D_MODEL, D_MODEL)), resident((1, D_MODEL)), resident((1, D_MODEL))],
        out_specs=out_rows,
        out_shape=jax.ShapeDtypeStruct((t, D_MODEL), _F32),
        compiler_params=params,
        name="out_proj_norm",
    )(merged, x2, wo, ln_g.reshape(1, -1), ln_b.reshape(1, -1))


def kernel(x, mem, w_in, w_mem_k, w_mem_v, w_s, b_s, ln_v_g, ln_v_b,
           w_branch_attn, w_branch_sgu, w_branch_mem, w_out, ln_g, ln_b):
    batch, seq, d = x.shape
    assert d == D_MODEL and seq % MOBA_BLOCK == 0 and mem.shape[1] == N_MEM
    t = batch * seq
    x2 = x.reshape(t, d)

    assert w_in.shape[1] == GATE_TILE0 * PROJ_TILE + N_BRANCH * D_MODEL and ATT_WIDTH == PROJ_TILE
    q_a, x_bf = _project(x2, w_in, tm=1024, tn=PROJ_TILE, name="in_proj_q", n_col_tiles=1,
                         scale=MOBA_Q_PRESCALE, row_chunks=4, emit_x_bf16=True)
    proj = _project(x_bf, w_in, tm=IN_PROJ_TM, tn=PROJ_TILE, name="in_proj_branches", first_col_tile=PROJ_TILE0,
                    n_col_tiles=GATE_TILE0 - PROJ_TILE0, row_chunks=IN_PROJ_ROW_CHUNKS)
    gates = _project(x_bf, w_in, tm=IN_PROJ_TM, tn=PROJ_TILE, name="in_proj_gates", first_col_tile=GATE_TILE0,
                     sigmoid=True, row_chunks=IN_PROJ_ROW_CHUNKS)

    ya, wa, wg, wc, wo = _moba_attention(q_a, proj, batch, seq,
                                         cast_jobs=(w_branch_attn, w_branch_sgu, w_branch_mem, w_out))
    ya = ya.reshape(t, ATT_WIDTH)
    yc = _memory_attention(proj, mem, w_mem_k, w_mem_v, seq).reshape(t, MEM_WIDTH)

    out = _merge_project_norm(ya, yc, proj, (w_s, b_s, ln_v_g, ln_v_b), gates, x2, wa, wg, wc, wo, ln_g, ln_b)
    return out.reshape(batch, seq, d)
```

```python
import functools

import jax
import jax.numpy as jnp
from jax import lax
from jax.experimental import pallas as pl
from jax.experimental.pallas import tpu as pltpu

D_MODEL = 2048
DEPTH = 1
N_MEM = 256
ATT_HEAD_DIM = 128
ATT_WIDTH = D_MODEL // 2
MOBA_BLOCK = 256
MOBA_TOPK = 3
SGU_WIDTH = D_MODEL // 2
SGU_CHUNK = 128
SGU_GROUP_DIM = 128
SGU_GROUPS = SGU_WIDTH // SGU_GROUP_DIM
MEM_HEADS = 4
MEM_WIDTH = D_MODEL // 2
MEM_HEAD_DIM = MEM_WIDTH // MEM_HEADS
N_BRANCH = 3
DN_ALPHA = (2 * DEPTH) ** 0.25
LN_EPS = 1e-5

PROJ_TILE = 1024
PROJ_TILE0 = 1
GATE_TILE0 = 9
PROJ_WIDTH = (GATE_TILE0 - PROJ_TILE0) * PROJ_TILE
K_A, V_A, Z_A, U_G, V_G, Z_G, Q_C, Z_C = range(8)

VMEM_LIMIT_BYTES = 56 * 1024 * 1024
IN_PROJ_TM = 2048
IN_PROJ_ROW_CHUNKS = 8
CAST_JOB_ROWS = 256

_NT = (((1,), (1,)), ((), ()))
_F32 = jnp.float32
_BF16 = jnp.bfloat16


_LOG2E = 1.4426950408889634


def _sigmoid(x):
    return 1.0 / (1.0 + jnp.exp2(x * -_LOG2E))


def _silu(x):
    return x * _sigmoid(x)


def _gelu_tanh(x):
    a = -2.0 * 0.7978845608028654 * _LOG2E
    return x / (1.0 + jnp.exp2(x * (a + (a * 0.044715) * (x * x))))


def _project_kernel(x_ref, w_ref, o_ref, *rest, scale, sigmoid, row_chunks, emit_x_bf16):
    x_bf_ref = rest[0] if emit_x_bf16 else None
    w_bf_ref = rest[-1]

    @pl.when(pl.program_id(1) == 0)
    def _cast_weight_tile():
        w_bf_ref[...] = w_ref[...].astype(_BF16)

    tc = x_ref.shape[0] // row_chunks
    for c in range(row_chunks):
        rows = slice(c * tc, (c + 1) * tc)
        xc = x_ref[rows, :]
        if emit_x_bf16:
            xc = xc.astype(_BF16)
            x_bf_ref[rows, :] = xc
        acc = jnp.dot(xc, w_bf_ref[...], preferred_element_type=_F32)
        if sigmoid:
            acc = _sigmoid(acc)
        if scale is not None:
            acc = acc * scale
        o_ref[rows, :] = acc.astype(o_ref.dtype)


def _project(x, w, *, tm, tn, name, first_col_tile=0, n_col_tiles=None,
             scale=None, sigmoid=False, row_chunks=1, emit_x_bf16=False):
    m, k = x.shape
    if n_col_tiles is None:
        n_col_tiles = w.shape[1] // tn - first_col_tile
    assert m % tm == 0 and tm % row_chunks == 0 and (first_col_tile + n_col_tiles) * tn <= w.shape[1]
    assert not emit_x_bf16 or n_col_tiles == 1
    out_specs = [pl.BlockSpec((tm, tn), lambda j, i: (i, j))]
    out_shape = [jax.ShapeDtypeStruct((m, n_col_tiles * tn), _BF16)]
    if emit_x_bf16:
        out_specs.append(pl.BlockSpec((tm, k), lambda j, i: (i, 0)))
        out_shape.append(jax.ShapeDtypeStruct((m, k), _BF16))
    outs = pl.pallas_call(
        functools.partial(_project_kernel, scale=scale, sigmoid=sigmoid, row_chunks=row_chunks,
                          emit_x_bf16=emit_x_bf16),
        grid=(n_col_tiles, m // tm),
        in_specs=[pl.BlockSpec((tm, k), lambda j, i: (i, 0)),
                  pl.BlockSpec((k, tn), lambda j, i: (0, first_col_tile + j))],
        out_specs=out_specs,
        out_shape=out_shape,
        scratch_shapes=[pltpu.VMEM((k, tn), _BF16)],
        compiler_params=pltpu.CompilerParams(
            dimension_semantics=("arbitrary", "arbitrary"), vmem_limit_bytes=VMEM_LIMIT_BYTES),
        name=name,
    )(x, w)
    return outs if emit_x_bf16 else outs[0]


def _cast_job_plan(arrays, n_steps, linear_step):
    specs, shapes, blocks, first = [], [], [], 0
    for a in arrays:
        assert a.shape[0] % CAST_JOB_ROWS == 0
        n_blocks = a.shape[0] // CAST_JOB_ROWS
        specs.append(pl.BlockSpec(
            (CAST_JOB_ROWS, a.shape[1]),
            lambda *idx, first=first, n_blocks=n_blocks: (jnp.clip(linear_step(*idx) - first, 0, n_blocks - 1), 0)))
        shapes.append(jax.ShapeDtypeStruct(a.shape, _BF16))
        blocks.append(n_blocks)
        first += n_blocks
    assert first <= n_steps
    return specs, shapes, tuple(blocks)


def _run_cast_jobs(step, job_in, job_out, blocks):
    first = 0
    for src, dst, n_blocks in zip(job_in, job_out, blocks):
        @pl.when((step >= first) & (step < first + n_blocks))
        def _cast_job_block(src=src, dst=dst):
            dst[...] = src[...].astype(_BF16)
        first += n_blocks


MOBA_HEADS_PER_STEP = 4
MOBA_Q_BLOCKS_PER_STEP = 2
MOBA_UNROLL = 8
_ACC_ROWS = ATT_HEAD_DIM + 16
MOBA_Q_PRESCALE = ATT_HEAD_DIM ** -0.5 * _LOG2E


def _moba_kernel(q_ref, k_ref, v_ref, z_ref, *rest, n_blocks, heads, cast_job_blocks):
    n_jobs = len(cast_job_blocks)
    job_in, o_ref, job_out = rest[:n_jobs], rest[n_jobs], rest[n_jobs + 1:2 * n_jobs + 1]
    vt_ref, kmean_ref, neg_ref, qt_ref, s_ref, acc_ref, smax_ref = rest[2 * n_jobs + 1:]
    blk = MOBA_BLOCK
    hd = ATT_HEAD_DIM
    step = (pl.program_id(0) * pl.num_programs(1) + pl.program_id(1)) * pl.num_programs(2) + pl.program_id(2)
    _run_cast_jobs(step, job_in, job_out, cast_job_blocks)

    @pl.when(pl.program_id(2) == 0)
    def _per_head_setup():
        for g in range(heads):
            cols = slice(g * hd, (g + 1) * hd)
            for j in range(n_blocks):
                rows = slice(j * blk, (j + 1) * blk)
                vt_ref[g, j, :hd, :] = v_ref[rows, cols].T
                vt_ref[g, j, hd:, :] = jnp.ones((_ACC_ROWS - hd, blk), _BF16)
                kmean_ref[g, j:j + 1, :] = (
                    jnp.sum(k_ref[rows, cols].astype(_F32), axis=0, keepdims=True) * (1.0 / blk))

    for sub in range(MOBA_Q_BLOCKS_PER_STEP):
        rows = slice(sub * blk, (sub + 1) * blk)
        _moba_query_block(pl.program_id(2) * MOBA_Q_BLOCKS_PER_STEP + sub, q_ref.at[rows], z_ref.at[rows],
                          o_ref.at[rows], k_ref, vt_ref, kmean_ref, neg_ref, qt_ref, s_ref, acc_ref, smax_ref,
                          n_blocks=n_blocks, heads=heads)


def _moba_query_block(qi, q_ref, z_ref, o_ref, k_ref, vt_ref, kmean_ref, neg_ref, qt_ref, s_ref, acc_ref, smax_ref,
                      *, n_blocks, heads):
    blk = MOBA_BLOCK
    hd = ATT_HEAD_DIM
    head_cols = [slice(g * hd, (g + 1) * hd) for g in range(heads)]
    qs = [q_ref[:, c] for c in head_cols]
    for g in range(heads):
        qt_ref[g] = qs[g].T

    def issue_scores(j, slot, causal=False):
        start = j * blk if isinstance(j, int) else pl.multiple_of(j * blk, blk)
        for g in range(heads):
            scores = jnp.dot(k_ref[pl.ds(start, blk), head_cols[g]], qt_ref[g],
                             preferred_element_type=_F32)
            if causal:
                key_pos = lax.broadcasted_iota(jnp.int32, scores.shape, 0)
                q_pos = lax.broadcasted_iota(jnp.int32, scores.shape, 1)
                scores = jnp.where(key_pos <= q_pos, scores, -jnp.inf)
            s_ref[slot, g] = scores
            smax_ref[slot * heads + g:slot * heads + g + 1, :] = jnp.max(scores, axis=0, keepdims=True)

    sel_scores = []
    for g in range(heads):
        km = kmean_ref[g]
        km_hi = km.astype(_BF16)
        km_lo = (km - km_hi.astype(_F32)).astype(_BF16)
        sel_scores.append(jnp.dot(km_hi, qt_ref[g], preferred_element_type=_F32)
                          + jnp.dot(km_lo, qt_ref[g], preferred_element_type=_F32))
    issue_scores(qi, 2, causal=True)
    row_id = lax.broadcasted_iota(jnp.int32, (n_blocks, blk), 0)
    row_f = row_id.astype(_F32)
    past = row_id < qi
    for g in range(heads):
        sb = jnp.where(past, sel_scores[g], -jnp.inf)
        taken = jnp.zeros(sb.shape, _F32)
        for _ in range(MOBA_TOPK):
            top = jnp.max(sb, axis=0, keepdims=True)
            first = jnp.min(jnp.where(sb == top, row_f, float(n_blocks)), axis=0, keepdims=True)
            hit = row_f == first
            taken = jnp.where(hit, 1.0, taken)
            sb = jnp.where(hit, -jnp.inf, sb)
        selected = past & (taken > 0.0)
        neg_ref[g] = jnp.where(selected, 0.0, -jnp.inf)

    def accumulate(j, slot, ms, causal):
        probs, m_news = [], []
        for g in range(heads):
            s = s_ref[slot, g]
            block_max = smax_ref[slot * heads + g:slot * heads + g + 1, :]
            if causal:
                m_new = m_sub = block_max
            else:
                unselected = neg_ref[g, pl.ds(j, 1), :]
                m_new = jnp.maximum(ms[g], block_max + unselected)
                m_sub = jnp.where(unselected == 0.0, m_new, jnp.inf)
            probs.append(jnp.exp2(s - m_sub).astype(_BF16))
            m_news.append(m_new)
        for g in range(heads):
            pv = jnp.dot(vt_ref[g, j], probs[g], preferred_element_type=_F32)
            if causal:
                acc_ref[g] = pv
            else:
                acc_ref[g] = jnp.exp2(ms[g] - m_news[g]) * acc_ref[g] + pv
        return m_news

    def run_blocks(first, n, ms, prefetch_after=True):
        for k in range(n):
            if prefetch_after or k + 1 < n:
                issue_scores(first + k + 1, (k + 1) % 2)
            ms = accumulate(first + k, k % 2, ms, causal=False)
        return ms

    issue_scores(0, 0)
    ms = accumulate(qi, 2, None, causal=True)
    ms = lax.fori_loop(0, qi // MOBA_UNROLL, lambda u, ms: run_blocks(MOBA_UNROLL * u, MOBA_UNROLL, ms), ms)

    first, rest = (qi // MOBA_UNROLL) * MOBA_UNROLL, qi % MOBA_UNROLL
    arm = MOBA_UNROLL // 2
    while arm >= 2:
        ms = lax.cond((rest & arm) != 0, functools.partial(run_blocks, first, arm), lambda ms: ms, ms)
        first = first + (rest & arm)
        arm //= 2
    last_block = first

    @pl.when((rest & 1) != 0)
    def _last_past_block():
        run_blocks(last_block, 1, ms, prefetch_after=False)

    for g in range(heads):
        attn = (acc_ref[g, :hd, :] * (1.0 / acc_ref[g, hd:hd + 1, :])).T
        o_ref[:, head_cols[g]] = (attn * _silu(z_ref[:, head_cols[g]].astype(_F32))).astype(o_ref.dtype)


def _moba_attention(q, proj, batch, seq, *, heads=MOBA_HEADS_PER_STEP, cast_jobs=()):
    n_blocks = seq // MOBA_BLOCK
    q3 = q.reshape(batch, seq, ATT_WIDTH)
    proj3 = proj.reshape(batch, seq, PROJ_WIDTH)
    width = heads * ATT_HEAD_DIM
    cols = PROJ_TILE // width
    q_rows = MOBA_Q_BLOCKS_PER_STEP * MOBA_BLOCK
    assert n_blocks % MOBA_Q_BLOCKS_PER_STEP == 0
    grid = (batch, ATT_WIDTH // width, n_blocks // MOBA_Q_BLOCKS_PER_STEP)
    job_specs, job_shapes, job_blocks = _cast_job_plan(
        cast_jobs, grid[0] * grid[1] * grid[2], lambda b, h, i: (b * grid[1] + h) * grid[2] + i)
    return pl.pallas_call(
        functools.partial(_moba_kernel, n_blocks=n_blocks, heads=heads, cast_job_blocks=job_blocks),
        grid=grid,
        in_specs=[pl.BlockSpec((None, q_rows, width), lambda b, h, i: (b, i, h)),
                  pl.BlockSpec((None, seq, width), lambda b, h, i: (b, 0, K_A * cols + h)),
                  pl.BlockSpec((None, seq, width), lambda b, h, i: (b, 0, V_A * cols + h)),
                  pl.BlockSpec((None, q_rows, width), lambda b, h, i: (b, i, Z_A * cols + h)),
                  *job_specs],
        out_specs=[pl.BlockSpec((None, q_rows, width), lambda b, h, i: (b, i, h)), *job_specs],
        out_shape=[jax.ShapeDtypeStruct((batch, seq, ATT_WIDTH), _BF16), *job_shapes],
        scratch_shapes=[pltpu.VMEM((heads, n_blocks, _ACC_ROWS, MOBA_BLOCK), _BF16),
                        pltpu.VMEM((heads, n_blocks, ATT_HEAD_DIM), _F32),
                        pltpu.VMEM((heads, n_blocks, MOBA_BLOCK), _F32),
                        pltpu.VMEM((heads, ATT_HEAD_DIM, MOBA_BLOCK), _BF16),
                        pltpu.VMEM((3, heads, MOBA_BLOCK, MOBA_BLOCK), _F32),
                        pltpu.VMEM((heads, _ACC_ROWS, MOBA_BLOCK), _F32),
                        pltpu.VMEM((3 * heads, MOBA_BLOCK), _F32)],
        compiler_params=pltpu.CompilerParams(
            dimension_semantics=("arbitrary", "arbitrary", "arbitrary"), vmem_limit_bytes=VMEM_LIMIT_BYTES),
        name="moba_attention",
    )(q3, proj3, proj3, proj3, *cast_jobs)


def _sgu_causal_weights(w_ref):
    t_pos = lax.broadcasted_iota(jnp.int32, (SGU_CHUNK, SGU_CHUNK), 0)
    s_pos = lax.broadcasted_iota(jnp.int32, (SGU_CHUNK, SGU_CHUNK), 1)
    return [jnp.where(s_pos <= t_pos, w_ref[g], 0.0).astype(_BF16) for g in range(SGU_GROUPS)]


def _sgu_rows(u_ref, v_ref, z_ref, w_causal, bt_ref, g_ref, b_ref, rows, yg_ref):
    u = _gelu_tanh(u_ref[rows, :].astype(_F32))
    v = _gelu_tanh(v_ref[rows, :].astype(_F32))
    mu = jnp.mean(v, axis=-1, keepdims=True)
    d = v - mu
    var = jnp.mean(d * d, axis=-1, keepdims=True)
    vn = (d * lax.rsqrt(var + LN_EPS) * g_ref[...] + b_ref[...]).astype(_BF16)
    gate = u * _silu(z_ref[rows, :].astype(_F32))
    for g in range(SGU_GROUPS):
        bias = bt_ref[:, g:g + 1]
        cols = slice(g * SGU_GROUP_DIM, (g + 1) * SGU_GROUP_DIM)
        for c in range((rows.stop - rows.start) // SGU_CHUNK):
            sub = slice(c * SGU_CHUNK, (c + 1) * SGU_CHUNK)
            mixed = jnp.dot(w_causal[g], vn[sub, cols], preferred_element_type=_F32) + bias
            yg_ref[rows.start + c * SGU_CHUNK:rows.start + (c + 1) * SGU_CHUNK, cols] = (
                gate[sub, cols] * mixed).astype(yg_ref.dtype)


def _mem_attn_kernel(q_ref, z_ref, mem_ref, wk_ref, wv_ref, o_ref, k_ref, v_ref):
    @pl.when(pl.program_id(1) == 0)
    def _project_memory():
        mem_bf = mem_ref[...].astype(_BF16)
        for w_ref, dst in ((wk_ref, k_ref), (wv_ref, v_ref)):
            for c in range(0, MEM_WIDTH, MEM_HEAD_DIM):
                dst[:, c:c + MEM_HEAD_DIM] = jnp.dot(
                    mem_bf, w_ref[:, c:c + MEM_HEAD_DIM].astype(_BF16), preferred_element_type=_F32
                ).astype(_BF16)

    scale_log2e = MEM_HEAD_DIM ** -0.5 * _LOG2E
    for h in range(MEM_HEADS):
        cols = slice(h * MEM_HEAD_DIM, (h + 1) * MEM_HEAD_DIM)
        s = lax.dot_general(q_ref[:, cols], k_ref[:, cols], _NT, preferred_element_type=_F32)
        p = jnp.exp2((s - jnp.max(s, axis=-1, keepdims=True)) * scale_log2e)
        l = jnp.sum(p, axis=-1, keepdims=True)
        o = jnp.dot(p.astype(_BF16), v_ref[:, cols], preferred_element_type=_F32) / l
        o_ref[:, cols] = (o * _silu(z_ref[:, cols].astype(_F32))).astype(o_ref.dtype)


def _memory_attention(proj, mem, w_mem_k, w_mem_v, seq, *, tm=1024):
    batch, n_mem, d = mem.shape
    proj3 = proj.reshape(batch, seq, PROJ_WIDTH)
    weight = pl.BlockSpec((d, MEM_WIDTH), lambda b, i: (0, 0), pipeline_mode=pl.Buffered(1))
    return pl.pallas_call(
        _mem_attn_kernel,
        grid=(batch, seq // tm),
        in_specs=[pl.BlockSpec((None, tm, MEM_WIDTH), lambda b, i: (b, i, Q_C)),
                  pl.BlockSpec((None, tm, MEM_WIDTH), lambda b, i: (b, i, Z_C)),
                  pl.BlockSpec((None, n_mem, d), lambda b, i: (b, 0, 0)),
                  weight, weight],
        out_specs=pl.BlockSpec((None, tm, MEM_WIDTH), lambda b, i: (b, i, 0)),
        out_shape=jax.ShapeDtypeStruct((batch, seq, MEM_WIDTH), _BF16),
        scratch_shapes=[pltpu.VMEM((n_mem, MEM_WIDTH), _BF16), pltpu.VMEM((n_mem, MEM_WIDTH), _BF16)],
        compiler_params=pltpu.CompilerParams(
            dimension_semantics=("arbitrary", "arbitrary"), vmem_limit_bytes=VMEM_LIMIT_BYTES),
        name="memory_attention",
    )(proj3, proj3, mem, w_mem_k, w_mem_v)


MERGE_TM = 512
OUT_NORM_TM = 512
OUT_NORM_CHUNKS = (256, 128, 128)
MERGE_ROW_CHUNK = 256


def _branch_merge_kernel(ya_ref, yc_ref, uvz_ref, ws_ref, bt_ref, lvg_ref, lvb_ref, ga_ref, gg_ref, gc_ref,
                         wa_ref, wg_ref, wc_ref, o_ref, yg_ref):
    u_ref, v_ref, z_ref = (uvz_ref.at[:, k * PROJ_TILE:(k + 1) * PROJ_TILE] for k in range(3))
    per_branch = D_MODEL // PROJ_TILE
    col_tiles = [slice(c * PROJ_TILE, (c + 1) * PROJ_TILE) for c in range(per_branch)]
    chunks = [slice(r * MERGE_ROW_CHUNK, (r + 1) * MERGE_ROW_CHUNK)
              for r in range(o_ref.shape[0] // MERGE_ROW_CHUNK)]
    w_causal = _sgu_causal_weights(ws_ref)

    def branch_dots(rows):
        n_sub = len(col_tiles)
        sub_rows = (rows.stop - rows.start) // n_sub
        outer = []
        for c, cols in enumerate(col_tiles):
            outer.append((jnp.dot(ya_ref[rows, :], wa_ref[:, cols], preferred_element_type=_F32),
                          jnp.dot(yc_ref[rows, :], wc_ref[:, cols], preferred_element_type=_F32)))
            sub = slice(rows.start + c * sub_rows, rows.start + (c + 1) * sub_rows)
            _sgu_rows(u_ref, v_ref, z_ref, w_causal, bt_ref, lvg_ref, lvb_ref, sub, yg_ref)
        inner = [jnp.dot(yg_ref[rows, :], wg_ref[:, cols], preferred_element_type=_F32) for cols in col_tiles]
        return outer, inner

    dots = branch_dots(chunks[0])
    for k, rows in enumerate(chunks):
        next_dots = branch_dots(chunks[k + 1]) if k + 1 < len(chunks) else None
        outer, inner = dots
        for c, cols in enumerate(col_tiles):
            merged = (ga_ref[rows, cols].astype(_F32) * outer[c][0]
                      + gg_ref[rows, cols].astype(_F32) * inner[c]
                      + gc_ref[rows, cols].astype(_F32) * outer[c][1])
            o_ref[rows, cols] = merged.astype(o_ref.dtype)
        dots = next_dots


def _out_norm_kernel(m_ref, x_ref, wo_ref, lng_ref, lnb_ref, o_ref):
    bounds = [0]
    for size in OUT_NORM_CHUNKS:
        bounds.append(bounds[-1] + size)
    assert bounds[-1] == o_ref.shape[0]
    chunks = [slice(a, b) for a, b in zip(bounds[:-1], bounds[1:])]

    def out_dot(rows):
        return jnp.dot(m_ref[rows, :], wo_ref[...], preferred_element_type=_F32)

    y = out_dot(chunks[0])
    for k, rows in enumerate(chunks):
        y_next = out_dot(chunks[k + 1]) if k + 1 < len(chunks) else None
        h = DN_ALPHA * x_ref[rows, :] + y
        mu = jnp.mean(h, axis=-1, keepdims=True)
        d = h - mu
        var = jnp.mean(d * d, axis=-1, keepdims=True)
        o_ref[rows, :] = d * lax.rsqrt(var + LN_EPS) * lng_ref[...] + lnb_ref[...]
        y = y_next


def _merge_project_norm(ya, yc, proj, sgu_params, gates, x2, wa, wg, wc, wo, ln_g, ln_b, *, tm=MERGE_TM):
    t = x2.shape[0]
    w_s, b_s, ln_v_g, ln_v_b = sgu_params
    assert t % tm == 0 and tm % MERGE_ROW_CHUNK == 0 and t % OUT_NORM_TM == 0 and OUT_NORM_TM % MERGE_ROW_CHUNK == 0
    branch = pl.BlockSpec((tm, ATT_WIDTH), lambda i: (i, 0))
    gate = lambda br: pl.BlockSpec((tm, D_MODEL), lambda i: (i, br))
    rows = pl.BlockSpec((tm, D_MODEL), lambda i: (i, 0))
    resident = lambda shape: pl.BlockSpec(shape, lambda i: (0, 0), pipeline_mode=pl.Buffered(1))
    params = pltpu.CompilerParams(dimension_semantics=("arbitrary",), vmem_limit_bytes=VMEM_LIMIT_BYTES)
    assert (V_G, Z_G) == (U_G + 1, U_G + 2) and U_G % 3 == 0
    whole = lambda shape: pl.BlockSpec(shape, lambda i: (0,) * len(shape))
    merged = pl.pallas_call(
        _branch_merge_kernel,
        grid=(t // tm,),
        in_specs=[branch, branch, pl.BlockSpec((tm, 3 * PROJ_TILE), lambda i: (i, U_G // 3)),
                  whole((SGU_GROUPS, SGU_CHUNK, SGU_CHUNK)), whole((SGU_CHUNK, SGU_GROUPS)),
                  whole((1, SGU_WIDTH)), whole((1, SGU_WIDTH)),
                  *[gate(br) for br in range(N_BRANCH)],
                  resident((ATT_WIDTH, D_MODEL)), resident((SGU_WIDTH, D_MODEL)), resident((MEM_WIDTH, D_MODEL))],
        out_specs=rows,
        out_shape=jax.ShapeDtypeStruct((t, D_MODEL), _BF16),
        scratch_shapes=[pltpu.VMEM((tm, SGU_WIDTH), _BF16)],
        compiler_params=params,
        name="branch_merge",
    )(ya, yc, proj, w_s, b_s.T, ln_v_g.reshape(1, -1), ln_v_b.reshape(1, -1),
      *([gates] * N_BRANCH), wa, wg, wc)
    out_rows = pl.BlockSpec((OUT_NORM_TM, D_MODEL), lambda i: (i, 0))
    return pl.pallas_call(
        _out_norm_kernel,
        grid=(t // OUT_NORM_TM,),
        in_specs=[out_rows, out_rows,
                  resident((D_MODEL, D_MODEL)), resident((1, D_MODEL)), resident((1, D_MODEL))],
        out_specs=out_rows,
        out_shape=jax.ShapeDtypeStruct((t, D_MODEL), _F32),
        compiler_params=params,
        name="out_proj_norm",
    )(merged, x2, wo, ln_g.reshape(1, -1), ln_b.reshape(1, -1))


def kernel(x, mem, w_in, w_mem_k, w_mem_v, w_s, b_s, ln_v_g, ln_v_b,
           w_branch_attn, w_branch_sgu, w_branch_mem, w_out, ln_g, ln_b):
    batch, seq, d = x.shape
    assert d == D_MODEL and seq % MOBA_BLOCK == 0 and mem.shape[1] == N_MEM
    t = batch * seq
    x2 = x.reshape(t, d)

    assert w_in.shape[1] == GATE_TILE0 * PROJ_TILE + N_BRANCH * D_MODEL and ATT_WIDTH == PROJ_TILE
    q_a, x_bf = _project(x2, w_in, tm=1024, tn=PROJ_TILE, name="in_proj_q", n_col_tiles=1,
                         scale=MOBA_Q_PRESCALE, row_chunks=4, emit_x_bf16=True)
    proj = _project(x_bf, w_in, tm=IN_PROJ_TM, tn=PROJ_TILE, name="in_proj_branches", first_col_tile=PROJ_TILE0,
                    n_col_tiles=GATE_TILE0 - PROJ_TILE0, row_chunks=IN_PROJ_ROW_CHUNKS)
    gates = _project(x_bf, w_in, tm=IN_PROJ_TM, tn=PROJ_TILE, name="in_proj_gates", first_col_tile=GATE_TILE0,
                     sigmoid=True, row_chunks=IN_PROJ_ROW_CHUNKS)

    ya, wa, wg, wc, wo = _moba_attention(q_a, proj, batch, seq,
                                         cast_jobs=(w_branch_attn, w_branch_sgu, w_branch_mem, w_out))
    ya = ya.reshape(t, ATT_WIDTH)
    yc = _memory_attention(proj, mem, w_mem_k, w_mem_v, seq).reshape(t, MEM_WIDTH)

    out = _merge_project_norm(ya, yc, proj, (w_s, b_s, ln_v_g, ln_v_b), gates, x2, wa, wg, wc, wo, ln_g, ln_b)
    return out.reshape(batch, seq, d)
```
